```python
import jax, jax.numpy as jnp
from jax import lax
import numpy as np

D_MODEL = 1024
BATCH = 2
SEQ = 8192
DEPTH = 2
DEC_BATCH = 32
DEC_SEQ = 4
PAST_LEN = 16384
PAGE_SIZE = 128

BRANCH_W = 512
N_BRANCH = 3
RET_HEADS = 4
RET_DK = 64
RET_DV = BRANCH_W // RET_HEADS
GLA_HEADS = 4
GLA_DK = 64
GLA_DV = BRANCH_W // GLA_HEADS
GLA_RANK = 16
GLA_TAU = 16.0
MOBA_HEADS = 8
MOBA_HD = BRANCH_W // MOBA_HEADS
MOBA_BLOCK = 256
MOBA_TOPK = 3
MOBA_QBLOCK = 64
ROPE_THETA = 10000.0
CHUNK = 64
D_FF = 2816
CONV_W = 3
EPS = 1e-6
IN_SPLITS = (RET_HEADS * RET_DK, RET_HEADS * RET_DK, BRANCH_W, BRANCH_W,
             GLA_HEADS * GLA_DK, GLA_HEADS * GLA_DK, BRANCH_W, BRANCH_W, GLA_RANK,
             BRANCH_W, BRANCH_W, BRANCH_W, N_BRANCH * D_MODEL)
D_IN = (2 * RET_HEADS * RET_DK + 2 * BRANCH_W + 2 * GLA_HEADS * GLA_DK + 2 * BRANCH_W
        + GLA_RANK + 3 * BRANCH_W + N_BRANCH * D_MODEL)
F32 = jnp.float32

kernel_name = 'hybrid_retention_gla_moba_step'


def _rmsnorm(x, g):
    x32 = x.astype(F32)
    y = x32 * lax.rsqrt(jnp.mean(x32 * x32, axis=-1, keepdims=True) + EPS)
    return (y * g.astype(F32)).astype(x.dtype)


def _head_layernorm(o, g):
    B, L, H, dv = o.shape
    o32 = o.astype(F32)
    c = o32 - jnp.mean(o32, axis=-1, keepdims=True)
    y = c * lax.rsqrt(jnp.mean(c * c, axis=-1, keepdims=True) + EPS)
    return y.reshape(B, L, H * dv) * g.astype(F32)


def _head_rmsnorm(o, g):
    B, L, H, dv = o.shape
    o32 = o.astype(F32)
    y = o32 * lax.rsqrt(jnp.mean(o32 * o32, axis=-1, keepdims=True) + EPS)
    return y.reshape(B, L, H * dv) * g.astype(F32)


def _rope(x, pos):
    d = x.shape[-1]
    half = d // 2
    freqs = jnp.power(ROPE_THETA, -jnp.arange(half, dtype=F32) / half)
    ang = pos.astype(F32)[:, None] * freqs[None, :]
    cos = jnp.cos(ang)[:, None, :]
    sin = jnp.sin(ang)[:, None, :]
    x32 = x.astype(F32)
    x1, x2 = x32[..., :half], x32[..., half:]
    return jnp.concatenate([x1 * cos - x2 * sin, x1 * sin + x2 * cos], axis=-1).astype(x.dtype)


def _split_cols(z, sizes):
    out = []
    start = 0
    for s in sizes:
        out.append(z[..., start:start + s])
        start += s
    return out


def _to_chunks(a, c):
    B, L, H, d = a.shape
    return a.astype(F32).reshape(B, L // c, c, H, d).transpose(1, 0, 3, 2, 4)


def _from_chunks(o):
    n, B, H, c, d = o.shape
    return o.transpose(1, 0, 3, 2, 4).reshape(B, n * c, H, d)


def _retention(q, k, v, s0):
    L, H = q.shape[1], q.shape[2]
    c = CHUNK if L % CHUNK == 0 else L
    log_g = jnp.log1p(-jnp.exp2(-5.0 - jnp.arange(H, dtype=F32)))
    t = jnp.arange(c, dtype=F32)
    rel = t[:, None] - t[None, :]
    causal = rel >= 0
    dmat = jnp.where(causal, jnp.exp(log_g[:, None, None] * jnp.where(causal, rel, 0.0)), 0.0)
    q_dec = jnp.exp(log_g[:, None] * (t + 1.0)[None, :])[:, :, None]
    k_dec = jnp.exp(log_g[:, None] * (c - 1.0 - t)[None, :])[:, :, None]
    c_dec = jnp.exp(log_g * c)[:, None, None]

    def step(S, blk):
        qc, kc, vc = blk
        att = jnp.einsum('bhid,bhjd->bhij', qc, kc) * dmat
        o = jnp.einsum('bhij,bhje->bhie', att, vc) + jnp.einsum('bhid,bhde->bhie', qc * q_dec, S)
        S = c_dec * S + jnp.einsum('bhjd,bhje->bhde', kc * k_dec, vc)
        return S, o

    S, o = lax.scan(step, s0.astype(F32), (_to_chunks(q, c), _to_chunks(k, c), _to_chunks(v, c)))
    return _from_chunks(o), S


def _gla(q, k, v, log_a, s0):
    L = q.shape[1]
    c = CHUNK if L % CHUNK == 0 else L
    causal = (jnp.arange(c)[:, None] >= jnp.arange(c)[None, :])[:, :, None]

    def step(S, blk):
        qc, kc, vc, ac = blk
        b = jnp.cumsum(ac, axis=2)
        b_last = b[:, :, -1:, :]
        inter = jnp.einsum('bhid,bhde->bhie', qc * jnp.exp(b), S)
        diff = b[:, :, :, None, :] - b[:, :, None, :, :]
        dec = jnp.exp(jnp.where(causal, diff, -jnp.inf))
        att = jnp.einsum('bhid,bhjd,bhijd->bhij', qc, kc, dec)
        o = inter + jnp.einsum('bhij,bhje->bhie', att, vc)
        S = jnp.exp(b_last[:, :, 0, :])[..., None] * S + jnp.einsum('bhjd,bhje->bhde', kc * jnp.exp(b_last - b), vc)
        return S, o

    S, o = lax.scan(step, s0.astype(F32),
                    (_to_chunks(q, c), _to_chunks(k, c), _to_chunks(v, c), _to_chunks(log_a, c)))
    return _from_chunks(o), S


def _moba_prompt(q, k, v):
    B, S, H, d = q.shape
    scale = d ** -0.5
    nb = -(-S // MOBA_BLOCK)
    pad = nb * MOBA_BLOCK - S
    kt = jnp.pad(k, ((0, 0), (0, pad), (0, 0), (0, 0))).transpose(0, 2, 1, 3)
    vt = jnp.pad(v, ((0, 0), (0, pad), (0, 0), (0, 0))).transpose(0, 2, 1, 3)
    kb = kt.reshape(B, H, nb, MOBA_BLOCK, d)
    vb = vt.reshape(B, H, nb, MOBA_BLOCK, d)
    kmean = jnp.mean(kb.astype(F32), axis=3)
    topk = min(MOBA_TOPK, nb)
    nq = S // MOBA_QBLOCK
    qb = q.transpose(0, 2, 1, 3).reshape(B, H, nq, MOBA_QBLOCK, d).transpose(2, 0, 1, 3, 4)
    bi = jnp.arange(B)[:, None, None, None]
    hi = jnp.arange(H)[None, :, None, None]
    blk_ids = jnp.arange(nb)
    n_sel = topk * MOBA_BLOCK

    def one_block(args):
        qc, i = args
        q0 = i * MOBA_QBLOCK
        cur = q0 // MOBA_BLOCK
        qf = qc.astype(F32)
        gate = jnp.einsum('bhqd,bhnd->bhqn', qf, kmean)
        gate = jnp.where(blk_ids < cur, gate, -jnp.inf)
        _, idx = lax.top_k(gate, topk)
        valid = jnp.arange(topk) < cur
        k_sel = kb[bi, hi, idx]
        v_sel = vb[bi, hi, idx]
        s_sel = jnp.einsum('bhqd,bhqnkd->bhqnk', qf, k_sel.astype(F32)) * scale
        s_sel = jnp.where(valid[:, None], s_sel, -jnp.inf).reshape(B, H, MOBA_QBLOCK, n_sel)
        k_own = lax.dynamic_slice_in_dim(kt, cur * MOBA_BLOCK, MOBA_BLOCK, axis=2)
        v_own = lax.dynamic_slice_in_dim(vt, cur * MOBA_BLOCK, MOBA_BLOCK, axis=2)
        qpos = q0 + jnp.arange(MOBA_QBLOCK)
        kpos = cur * MOBA_BLOCK + jnp.arange(MOBA_BLOCK)
        s_own = jnp.einsum('bhqd,bhkd->bhqk', qf, k_own.astype(F32)) * scale
        s_own = jnp.where(kpos[None, :] <= qpos[:, None], s_own, -jnp.inf)
        pr = jax.nn.softmax(jnp.concatenate([s_sel, s_own], axis=-1), axis=-1)
        p_sel = pr[..., :n_sel].reshape(B, H, MOBA_QBLOCK, topk, MOBA_BLOCK)
        o = (jnp.einsum('bhqnk,bhqnkd->bhqd', p_sel, v_sel.astype(F32))
             + jnp.einsum('bhqk,bhkd->bhqd', pr[..., n_sel:], v_own.astype(F32)))
        return o.astype(q.dtype)

    o = lax.map(one_block, (qb, jnp.arange(nq)))
    return o.transpose(1, 0, 3, 2, 4).reshape(B, S, H, d)


def _moba_sample(q, k_new, v_new, cache_k, cache_v, page_table, layer):
    Bd, T, H, d = q.shape
    scale = d ** -0.5
    n_pages = page_table.shape[1]
    past_len = n_pages * PAGE_SIZE
    n_full = past_len // MOBA_BLOCK
    own_start = n_full * MOBA_BLOCK
    n_own_past = past_len - own_start
    qf = q.astype(F32).transpose(0, 2, 1, 3)
    k_past = cache_k[page_table[:, :, None], jnp.arange(PAGE_SIZE)[None, None, :], layer]
    k_past = k_past.reshape(Bd, past_len, H, d)
    logits = []
    parts = []
    if n_full > 0:
        topk = min(MOBA_TOPK, n_full)
        kmean = jnp.mean(k_past[:, :own_start].astype(F32).reshape(Bd, n_full, MOBA_BLOCK, H, d), axis=2)
        gate = jnp.einsum('bhtd,bnhd->bhtn', qf, kmean)
        _, idx = lax.top_k(gate, topk)
        pos = idx[..., None] * MOBA_BLOCK + jnp.arange(MOBA_BLOCK)
        bi = jnp.arange(Bd)[:, None, None, None, None]
        hi = jnp.arange(H)[None, :, None, None, None]
        k_sel = k_past[bi, pos, hi]
        v_sel = cache_v[page_table[bi, pos // PAGE_SIZE], pos % PAGE_SIZE, layer, hi]
        s_sel = jnp.einsum('bhtd,bhtnkd->bhtnk', qf, k_sel.astype(F32)) * scale
        logits.append(s_sel.reshape(Bd, H, T, topk * MOBA_BLOCK))
        parts.append(('sel', topk, v_sel))
    if n_own_past > 0:
        own_pos = own_start + jnp.arange(n_own_past)
        k_own = k_past[:, own_start:]
        v_own = cache_v[page_table[:, own_pos // PAGE_SIZE], (own_pos % PAGE_SIZE)[None, :], layer]
        logits.append(jnp.einsum('bhtd,bkhd->bhtk', qf, k_own.astype(F32)) * scale)
        parts.append(('own', n_own_past, v_own))
    s_new = jnp.einsum('bhtd,bshd->bhts', qf, k_new.astype(F32)) * scale
    s_new = jnp.where(jnp.arange(T)[:, None] >= jnp.arange(T)[None, :], s_new, -jnp.inf)
    logits.append(s_new)
    parts.append(('new', T, v_new))
    pr = jax.nn.softmax(jnp.concatenate(logits, axis=-1), axis=-1)
    o = jnp.zeros((Bd, H, T, d), F32)
    start = 0
    for kind, n, vals in parts:
        if kind == 'sel':
            w = n * MOBA_BLOCK
            p_sel = pr[..., start:start + w].reshape(Bd, H, T, n, MOBA_BLOCK)
            o = o + jnp.einsum('bhtnk,bhtnkd->bhtd', p_sel, vals.astype(F32))
        else:
            w = n
            o = o + jnp.einsum('bhtk,bkhd->bhtd', pr[..., start:start + w], vals.astype(F32))
        start += w
    return o.transpose(0, 2, 1, 3).astype(q.dtype)


def _layer(x, pos, ret_s0, gla_s0, conv_s0, attn_fn, p):
    B, L, _ = x.shape
    h = _rmsnorm(x, p['g_pre_mix'])
    z = h @ p['w_in']
    rq, rk, rv, rg, gq, gk, gv, gg, ga, mq, mk, mv, gl = _split_cols(z, IN_SPLITS)
    rq = _rope(rq.reshape(B, L, RET_HEADS, RET_DK), pos)
    rk = _rope(rk.reshape(B, L, RET_HEADS, RET_DK), pos) * (RET_DK ** -0.5)
    o_ret, ret_s = _retention(rq, rk, rv.reshape(B, L, RET_HEADS, RET_DV), ret_s0)
    o_ret = _head_layernorm(o_ret, p['g_ret_norm']).astype(x.dtype) * jax.nn.silu(rg)
    log_a = jax.nn.log_sigmoid((ga @ p['w_gla_a2'] + p['b_gla_a']).astype(F32)) / GLA_TAU
    o_gla, gla_s = _gla(gq.reshape(B, L, GLA_HEADS, GLA_DK),
                        gk.reshape(B, L, GLA_HEADS, GLA_DK) * (GLA_DK ** -0.5),
                        gv.reshape(B, L, GLA_HEADS, GLA_DV),
                        log_a.reshape(B, L, GLA_HEADS, GLA_DK), gla_s0)
    o_gla = _head_rmsnorm(o_gla, p['g_gla_norm']).astype(x.dtype) * jax.nn.silu(gg)
    mq = _rope(mq.reshape(B, L, MOBA_HEADS, MOBA_HD), pos)
    mk = _rope(mk.reshape(B, L, MOBA_HEADS, MOBA_HD), pos)
    mv = mv.reshape(B, L, MOBA_HEADS, MOBA_HD)
    o_moba = attn_fn(mq, mk, mv).reshape(B, L, BRANCH_W)
    branches = jnp.stack([o_ret, o_gla, o_moba], axis=2)
    proj = jnp.einsum('blnc,ncd->blnd', branches, p['w_branch'])
    gates = jax.nn.sigmoid(gl.astype(F32)).reshape(B, L, N_BRANCH, D_MODEL)
    merged = jnp.sum(gates * proj.astype(F32), axis=2).astype(x.dtype)
    x = x + _rmsnorm(merged @ p['w_out'], p['g_post_mix'])
    h = _rmsnorm(x, p['g_pre_ffn'])
    u = h @ p['w_up']
    ucat = jnp.concatenate([conv_s0.astype(u.dtype), u], axis=1)
    conv = p['b_conv'] + ucat[:, 0:L] * p['w_conv'][0]
    for j in range(1, CONV_W):
        conv = conv + ucat[:, j:j + L] * p['w_conv'][j]
    a, b = jnp.split(conv, 2, axis=-1)
    f = (jax.nn.gelu(a, approximate=True) * b) @ p['w_down']
    x = x + _rmsnorm(f, p['g_post_ffn'])
    return x, ret_s, gla_s, mk, mv, ucat[:, L:]


def setup_inputs(seed: int = 0) -> dict:
    key = jax.random.key(seed)
    ks = jax.random.split(key, 24)
    n_pages = PAST_LEN // PAGE_SIZE
    n_pool = (DEC_BATCH * n_pages * 5) // 4

    def nrm(k, shape, s=1.0):
        return jax.random.normal(k, shape, jnp.float32) * s

    def gain(k, shape):
        return 1.0 + 0.01 * jax.random.normal(k, shape, jnp.float32)

    perm = jax.random.permutation(ks[4], n_pool)
    page_table = perm[:DEC_BATCH * n_pages].reshape(DEC_BATCH, n_pages).astype(jnp.int32)
    return {
        'x_prompt': nrm(ks[0], (BATCH, SEQ, D_MODEL)),
        'x_sample': nrm(ks[1], (DEC_BATCH, DEC_SEQ, D_MODEL)),
        'cache_k': nrm(ks[2], (n_pool, PAGE_SIZE, DEPTH, MOBA_HEADS, MOBA_HD)),
        'cache_v': nrm(ks[3], (n_pool, PAGE_SIZE, DEPTH, MOBA_HEADS, MOBA_HD)),
        'page_table': page_table,
        'state_ret': nrm(ks[5], (DEPTH, DEC_BATCH, RET_HEADS, RET_DK, RET_DV)),
        'state_gla': nrm(ks[6], (DEPTH, DEC_BATCH, GLA_HEADS, GLA_DK, GLA_DV)),
        'state_conv': nrm(ks[7], (DEPTH, DEC_BATCH, CONV_W - 1, 2 * D_FF)),
        'g_pre_mix': gain(ks[8], (DEPTH, D_MODEL)),
        'w_in': nrm(ks[9], (DEPTH, D_MODEL, D_IN), D_MODEL ** -0.5),
        'w_gla_a2': nrm(ks[10], (DEPTH, GLA_RANK, GLA_HEADS * GLA_DK), GLA_RANK ** -0.5),
        'b_gla_a': nrm(ks[11], (DEPTH, GLA_HEADS * GLA_DK), 0.01),
        'g_ret_norm': gain(ks[12], (DEPTH, BRANCH_W)),
        'g_gla_norm': gain(ks[13], (DEPTH, BRANCH_W)),
        'w_branch': nrm(ks[14], (DEPTH, N_BRANCH, BRANCH_W, D_MODEL), BRANCH_W ** -0.5),
        'w_out': nrm(ks[15], (DEPTH, D_MODEL, D_MODEL), D_MODEL ** -0.5),
        'g_post_mix': gain(ks[16], (DEPTH, D_MODEL)),
        'g_pre_ffn': gain(ks[17], (DEPTH, D_MODEL)),
        'w_up': nrm(ks[18], (DEPTH, D_MODEL, 2 * D_FF), D_MODEL ** -0.5),
        'w_conv': nrm(ks[19], (DEPTH, CONV_W, 2 * D_FF), CONV_W ** -0.5),
        'b_conv': nrm(ks[20], (DEPTH, 2 * D_FF), 0.01),
        'w_down': nrm(ks[21], (DEPTH, D_FF, D_MODEL), D_FF ** -0.5),
        'g_post_ffn': gain(ks[22], (DEPTH, D_MODEL)),
    }


def reference(x_prompt, x_sample, cache_k, cache_v, page_table, state_ret, state_gla, state_conv,
              g_pre_mix, w_in, w_gla_a2, b_gla_a, g_ret_norm, g_gla_norm, w_branch, w_out,
              g_post_mix, g_pre_ffn, w_up, w_conv, b_conv, w_down, g_post_ffn):
    B, S, _ = x_prompt.shape
    Bd, T, _ = x_sample.shape
    past_len = page_table.shape[1] * PAGE_SIZE
    pos_p = jnp.arange(S, dtype=jnp.int32)
    pos_s = past_len + jnp.arange(T, dtype=jnp.int32)
    yp, ys = x_prompt, x_sample
    ret_p, ret_s, gla_p, gla_s = [], [], [], []
    k_p, v_p, k_s, v_s, conv_p, conv_s = [], [], [], [], [], []
    for l in range(DEPTH):
        p = dict(g_pre_mix=g_pre_mix[l], w_in=w_in[l], w_gla_a2=w_gla_a2[l], b_gla_a=b_gla_a[l],
                 g_ret_norm=g_ret_norm[l], g_gla_norm=g_gla_norm[l], w_branch=w_branch[l],
                 w_out=w_out[l], g_post_mix=g_post_mix[l], g_pre_ffn=g_pre_ffn[l], w_up=w_up[l],
                 w_conv=w_conv[l], b_conv=b_conv[l], w_down=w_down[l], g_post_ffn=g_post_ffn[l])
        yp, rs, gs, kk, vv, cs = _layer(
            yp, pos_p,
            jnp.zeros((B, RET_HEADS, RET_DK, RET_DV), F32),
            jnp.zeros((B, GLA_HEADS, GLA_DK, GLA_DV), F32),
            jnp.zeros((B, CONV_W - 1, 2 * D_FF), yp.dtype),
            _moba_prompt, p)
        ret_p.append(rs.astype(state_ret.dtype))
        gla_p.append(gs.astype(state_gla.dtype))
        k_p.append(kk)
        v_p.append(vv)
        conv_p.append(cs.astype(state_conv.dtype))
        attn_s = lambda q, k, v, l=l: _moba_sample(q, k, v, cache_k, cache_v, page_table, l)
        ys, rs, gs, kk, vv, cs = _layer(ys, pos_s, state_ret[l], state_gla[l], state_conv[l], attn_s, p)
        ret_s.append(rs.astype(state_ret.dtype))
        gla_s.append(gs.astype(state_gla.dtype))
        k_s.append(kk)
        v_s.append(vv)
        conv_s.append(cs.astype(state_conv.dtype))
    return (yp, ys,
            jnp.stack(ret_p, axis=0), jnp.stack(ret_s, axis=0),
            jnp.stack(gla_p, axis=0), jnp.stack(gla_s, axis=0),
            jnp.stack(k_p, axis=2), jnp.stack(v_p, axis=2),
            jnp.stack(k_s, axis=2), jnp.stack(v_s, axis=2),
            jnp.stack(conv_p, axis=0), jnp.stack(conv_s, axis=0))
```

```python
import functools

import numpy as np
import jax
import jax.numpy as jnp
from jax import lax
from jax.experimental import pallas as pl
from jax.experimental.pallas import tpu as pltpu

F32 = jnp.float32
BF16 = jnp.bfloat16

D_MODEL = 1024
BRANCH_W = 512
N_BRANCH = 3
RET_HEADS = 4
RET_DK = 64
RET_DV = 128
GLA_HEADS = 4
GLA_DK = 64
GLA_DV = 128
GLA_RANK = 16
GLA_TAU = 16.0
MOBA_HEADS = 8
MOBA_HD = 64
MOBA_BLOCK = 256
MOBA_TOPK = 3
ROPE_THETA = 10000.0
CHUNK = 64
SUBCHUNK = 16
D_FF = 2816
CONV_W = 3
EPS = 1e-6
PAGE_SIZE = 128

LANES = 128
SUBLANES = 8
VMEM_LIMIT = 56 * 1024 * 1024

Z_GL = 0
Z_RQ = 3072
Z_RK = 3328
Z_RV = 3584
Z_RG = 4096
Z_GQ = 4608
Z_GK = 4864
Z_GV = 5120
Z_GG = 5632
Z_MQ = 6144
Z_MK = 6656
Z_MV = 7168
Z_W = 7680
SEG_W = 512
SEG_RQK = Z_RQ // SEG_W
SEG_GQK = Z_GQ // SEG_W
SEG_MQ = Z_MQ // SEG_W
SEG_MK = Z_MK // SEG_W

SAMPLE_ROWS = 8
NEG_BIG = -1e30
MASK_BIG = 2.0 ** 100

NT = (((1,), (1,)), ((), ()))
TN = (((0,), (0,)), ((), ()))


def _dot(a, b, dims=None, precision=None):
    if dims is None:
        return jnp.dot(a, b, preferred_element_type=F32, precision=precision)
    return lax.dot_general(a, b, dims, preferred_element_type=F32, precision=precision)


def _rms(x, g):
    return x * lax.rsqrt(jnp.mean(x * x, axis=-1, keepdims=True) + EPS) * g


def _silu(x):
    return x / (1.0 + jnp.exp(-x))


def _sigmoid(x):
    return 1.0 / (1.0 + jnp.exp(-x))


def _cparams(sem):
    return pltpu.CompilerParams(dimension_semantics=sem, vmem_limit_bytes=VMEM_LIMIT)


def _inproj_kernel(x_ref, g_ref, w_ref, wga_ref, w2_ref, ba_ref, cos_ref, sin_ref,
                   z_ref, la_ref, h_scr, *, sample):
    j = pl.program_id(1)

    @pl.when(j == 0)
    def _():
        hb = _rms(x_ref[...], g_ref[...]).astype(BF16)
        h_scr[...] = hb
        ga = _dot(hb, wga_ref[...])
        u = _dot(ga.astype(BF16), w2_ref[...]) + ba_ref[...]
        la = (jnp.minimum(u, 0.0) - jnp.log(1.0 + jnp.exp(-jnp.abs(u)))) * (1.0 / GLA_TAU)
        if sample:
            row = lax.broadcasted_iota(jnp.int32, (la.shape[0], 1), 0)
            la = jnp.where((row % SAMPLE_ROWS) >= SAMPLE_ROWS // 2, la, 0.0)
        la_ref[...] = la

    acc = _dot(h_scr[...], w_ref[...])
    lane = lax.broadcasted_iota(jnp.int32, (1, LANES), 1)
    first = (lane % MOBA_HD) < (MOBA_HD // 2)

    def rope(a):
        sw = jnp.where(first, pltpu.roll(a, LANES - MOBA_HD // 2, 1), pltpu.roll(a, MOBA_HD // 2, 1))
        return a * cos_ref[...] + sw * sin_ref[...]

    def store_rope(scales):
        for c, s in enumerate(scales):
            r = rope(acc[:, c * LANES:(c + 1) * LANES])
            z_ref[:, c * LANES:(c + 1) * LANES] = r if s == 1.0 else r * s

    ks = RET_DK ** -0.5

    @pl.when(j == SEG_RQK)
    def _():
        store_rope((1.0, 1.0, ks, ks))

    @pl.when(j == SEG_GQK)
    def _():
        z_ref[:, :SEG_W // 2] = acc[:, :SEG_W // 2]
        z_ref[:, SEG_W // 2:] = acc[:, SEG_W // 2:] * (GLA_DK ** -0.5)

    @pl.when(j == SEG_MQ)
    def _():
        store_rope((MOBA_HD ** -0.5,) * 4)

    @pl.when(j == SEG_MK)
    def _():
        store_rope((1.0,) * 4)

    plain = (j != SEG_RQK) & (j != SEG_GQK) & (j != SEG_MQ) & (j != SEG_MK)

    @pl.when(plain)
    def _():
        z_ref[...] = acc


def _inproj(x, g, w_main, w_ga, w2, ba, cos, sin, *, tm, sample):
    n = x.shape[0]
    nseg = Z_W // SEG_W
    ntab = cos.shape[0] // tm
    return pl.pallas_call(
        functools.partial(_inproj_kernel, sample=sample),
        grid=(n // tm, nseg),
        in_specs=[
            pl.BlockSpec((tm, D_MODEL), lambda i, j: (i, 0)),
            pl.BlockSpec((1, D_MODEL), lambda i, j: (0, 0)),
            pl.BlockSpec((D_MODEL, SEG_W), lambda i, j: (0, j)),
            pl.BlockSpec((D_MODEL, LANES), lambda i, j: (0, 0)),
            pl.BlockSpec((LANES, GLA_HEADS * GLA_DK), lambda i, j: (0, 0)),
            pl.BlockSpec((1, GLA_HEADS * GLA_DK), lambda i, j: (0, 0)),
            pl.BlockSpec((tm, LANES), lambda i, j: (i % ntab, 0)),
            pl.BlockSpec((tm, LANES), lambda i, j: (i % ntab, 0)),
        ],
        out_specs=[
            pl.BlockSpec((tm, SEG_W), lambda i, j: (i, j)),
            pl.BlockSpec((tm, GLA_HEADS * GLA_DK), lambda i, j: (i, 0)),
        ],
        out_shape=[jax.ShapeDtypeStruct((n, Z_W), F32),
                   jax.ShapeDtypeStruct((n, GLA_HEADS * GLA_DK), F32)],
        scratch_shapes=[pltpu.VMEM((tm, D_MODEL), BF16)],
        compiler_params=_cparams(("parallel", "arbitrary")),
        name="inproj",
    )(x, g, w_main, w_ga, w2, ba, cos, sin)


def _decay_tables(c, valid):
    log_g = np.log1p(-np.exp2(-5.0 - np.arange(RET_HEADS, dtype=np.float64)))
    b = log_g[:, None] * np.cumsum(np.asarray(valid, np.float64))[None, :]
    diff = b[:, :, None] - b[:, None, :]
    causal = np.arange(c)[:, None] >= np.arange(c)[None, :]
    dmat = np.where(causal[None], np.exp(np.where(causal[None], diff, 0.0)), 0.0)
    qdec = np.broadcast_to(np.exp(b)[:, :, None], (RET_HEADS, c, RET_DK))
    kdec = np.broadcast_to(np.exp(b[:, -1:] - b)[:, :, None], (RET_HEADS, c, RET_DK))
    cdec = tuple(float(v) for v in np.exp(b[:, -1]))
    tri = causal.astype(np.float32)
    return (jnp.asarray(dmat, F32), jnp.asarray(qdec, F32), jnp.asarray(kdec, F32),
            jnp.asarray(tri, F32), cdec)


def _scan_kernel(*refs, C, SC, nchunk, cdec, zero_init):
    if zero_init:
        (zr_ref, zg_ref, la_ref, dmat_ref, qdec_ref, kdec_ref, tri_ref, gr_ref, gg_ref,
         o_ref, sr_ref, sg_ref, Sr, SgT) = refs
    else:
        (zr_ref, zg_ref, la_ref, dmat_ref, qdec_ref, kdec_ref, tri_ref, gr_ref, gg_ref,
         s0r_ref, s0g_ref, o_ref, sr_ref, sg_ref, Sr, SgT) = refs
    t = pl.program_id(1)
    nsub = C // SC
    HK = RET_HEADS * RET_DK

    @pl.when(t == 0)
    def _():
        if zero_init:
            Sr[...] = jnp.zeros_like(Sr)
            SgT[...] = jnp.zeros_like(SgT)
        else:
            Sr[...] = s0r_ref[0]
            for h in range(GLA_HEADS):
                SgT[h] = s0g_ref[0, h].T

    row = lax.broadcasted_iota(jnp.int32, (C, 1), 0)
    rsub = row % SC
    lane_c = lax.broadcasted_iota(jnp.int32, (1, C), 1)

    def bcast_rows(x, jj):
        parts = [jnp.broadcast_to(x[i * SC + jj:i * SC + jj + 1, :], (SC, x.shape[1]))
                 for i in range(nsub)]
        return parts[0] if nsub == 1 else jnp.concatenate(parts, axis=0)

    def chunk(ci, carry):
        r0 = pl.multiple_of(ci * C, C)
        rows = pl.ds(r0, C)

        q = zr_ref[rows, 0:HK]
        k = zr_ref[rows, HK:2 * HK]
        v = zr_ref[rows, 2 * HK:2 * HK + BRANCH_W]
        rg = zr_ref[rows, 2 * HK + BRANCH_W:2 * HK + 2 * BRANCH_W]
        for h in range(RET_HEADS):
            qh = q[:, h * RET_DK:(h + 1) * RET_DK]
            kh = k[:, h * RET_DK:(h + 1) * RET_DK]
            vh = v[:, h * RET_DV:(h + 1) * RET_DV].astype(BF16)
            att = _dot(qh.astype(BF16), kh.astype(BF16), NT) * dmat_ref[h]
            s_old = Sr[h]
            o = _dot(att.astype(BF16), vh) + _dot((qh * qdec_ref[h]).astype(BF16), s_old.astype(BF16))
            Sr[h] = cdec[h] * s_old + _dot((kh * kdec_ref[h]).T.astype(BF16), vh)
            cen = o - jnp.mean(o, axis=-1, keepdims=True)
            y = cen * lax.rsqrt(jnp.mean(cen * cen, axis=-1, keepdims=True) + EPS)
            y = y * gr_ref[:, h * RET_DV:(h + 1) * RET_DV] * _silu(rg[:, h * RET_DV:(h + 1) * RET_DV])
            o_ref[rows, h * RET_DV:(h + 1) * RET_DV] = y.astype(o_ref.dtype)

        q = zg_ref[rows, 0:HK]
        k = zg_ref[rows, HK:2 * HK]
        v = zg_ref[rows, 2 * HK:2 * HK + BRANCH_W]
        gg = zg_ref[rows, 2 * HK + BRANCH_W:2 * HK + 2 * BRANCH_W]
        b = _dot(tri_ref[...], la_ref[rows, :], precision=lax.Precision.HIGHEST)
        blast = b[C - 1:C, :]
        qe = q * jnp.exp(b)
        kd = k * jnp.exp(blast - b)
        eblast = jnp.exp(blast)

        att = [jnp.zeros((C, C), F32) for _ in range(GLA_HEADS)]
        for jj in range(SC):
            kb = bcast_rows(k, jj)
            bb = bcast_rows(b, jj)
            w = q * kb * jnp.exp(jnp.where(rsub >= jj, b - bb, NEG_BIG))
            hit = lane_c == (row - rsub + jj)
            for h in range(GLA_HEADS):
                col = jnp.sum(w[:, h * GLA_DK:(h + 1) * GLA_DK], axis=-1, keepdims=True)
                att[h] = jnp.where(hit, col, att[h])
        for i in range(1, nsub):
            bi = b[i * SC - 1:i * SC, :]
            in_i = (row >= i * SC) & (row < (i + 1) * SC)
            qi = (q * jnp.exp(jnp.where(in_i, b - bi, NEG_BIG))).astype(BF16)
            ki = (k * jnp.exp(jnp.where(row < i * SC, bi - b, NEG_BIG))).astype(BF16)
            for h in range(GLA_HEADS):
                att[h] = att[h] + _dot(qi[:, h * GLA_DK:(h + 1) * GLA_DK],
                                       ki[:, h * GLA_DK:(h + 1) * GLA_DK], NT)
        for h in range(GLA_HEADS):
            vh = v[:, h * GLA_DV:(h + 1) * GLA_DV]
            vhb = vh.astype(BF16)
            st_old = SgT[h]
            o = _dot(att[h].astype(BF16), vhb) + _dot(
                qe[:, h * GLA_DK:(h + 1) * GLA_DK].astype(BF16), st_old.astype(BF16), NT)
            SgT[h] = st_old * eblast[:, h * GLA_DK:(h + 1) * GLA_DK] + _dot(
                vh.T.astype(BF16), kd[:, h * GLA_DK:(h + 1) * GLA_DK].astype(BF16))
            y = _rms(o, gg_ref[:, h * GLA_DV:(h + 1) * GLA_DV]) * _silu(gg[:, h * GLA_DV:(h + 1) * GLA_DV])
            o_ref[rows, BRANCH_W + h * GLA_DV:BRANCH_W + (h + 1) * GLA_DV] = y.astype(o_ref.dtype)
        return carry

    lax.fori_loop(0, nchunk, chunk, 0)

    @pl.when(t == pl.num_programs(1) - 1)
    def _():
        sr_ref[0] = Sr[...]
        for h in range(GLA_HEADS):
            sg_ref[0, h] = SgT[h].T


def _scan(z, la, g_ret, g_gla, tables, s0, *, nseq, seq_rows, T, C, SC, out_dtype):
    dmat, qdec, kdec, tri, cdec = tables
    nT = seq_rows // T
    zero_init = s0 is None
    zw = 2 * RET_HEADS * RET_DK + 2 * BRANCH_W
    const3 = lambda s, t: (0, 0, 0)
    const2 = lambda s, t: (0, 0)
    in_specs = [
        pl.BlockSpec((T, zw), lambda s, t: (s * nT + t, Z_RQ // zw)),
        pl.BlockSpec((T, zw), lambda s, t: (s * nT + t, Z_GQ // zw)),
        pl.BlockSpec((T, GLA_HEADS * GLA_DK), lambda s, t: (s * nT + t, 0)),
        pl.BlockSpec(dmat.shape, const3),
        pl.BlockSpec(qdec.shape, const3),
        pl.BlockSpec(kdec.shape, const3),
        pl.BlockSpec(tri.shape, const2),
        pl.BlockSpec((1, BRANCH_W), const2),
        pl.BlockSpec((1, BRANCH_W), const2),
    ]
    args = [z, z, la, dmat, qdec, kdec, tri, g_ret, g_gla]
    st_spec = pl.BlockSpec((1, RET_HEADS, RET_DK, RET_DV), lambda s, t: (s, 0, 0, 0))
    if not zero_init:
        in_specs += [st_spec, st_spec]
        args += [s0[0], s0[1]]
    st_shape = jax.ShapeDtypeStruct((nseq, RET_HEADS, RET_DK, RET_DV), F32)
    return pl.pallas_call(
        functools.partial(_scan_kernel, C=C, SC=SC, nchunk=T // C, cdec=cdec, zero_init=zero_init),
        grid=(nseq, nT),
        in_specs=in_specs,
        out_specs=[pl.BlockSpec((T, 2 * BRANCH_W), lambda s, t: (s * nT + t, 0)), st_spec, st_spec],
        out_shape=[jax.ShapeDtypeStruct((nseq * seq_rows, 2 * BRANCH_W), out_dtype), st_shape, st_shape],
        scratch_shapes=[pltpu.VMEM((RET_HEADS, RET_DK, RET_DV), F32),
                        pltpu.VMEM((GLA_HEADS, GLA_DV, GLA_DK), F32)],
        compiler_params=_cparams(("parallel", "arbitrary")),
        name="scan",
    )(*args)


def _top3_select(g, lane):
    sel = jnp.zeros(g.shape, jnp.bool_)
    for _ in range(MOBA_TOPK):
        m = jnp.max(g, axis=-1, keepdims=True)
        idx = jnp.min(jnp.where(g == m, lane, 4 * LANES), axis=-1, keepdims=True)
        pick = (lane == idx) & (m > -jnp.inf)
        sel = sel | pick
        g = jnp.where(pick, -jnp.inf, g)
    return sel


def _moba_prompt_kernel(q_ref, k_ref, v_ref, o_ref, km_scr, *, nblk):
    t = pl.program_id(2)
    BLK = MOBA_BLOCK
    lane = lax.broadcasted_iota(jnp.int32, (1, LANES), 1)

    @pl.when(t == 0)
    def _():
        km_scr[...] = jnp.zeros_like(km_scr)
        for blk in range(nblk):
            m = jnp.mean(k_ref[blk * BLK:(blk + 1) * BLK, :], axis=0, keepdims=True)
            km_scr[0, MOBA_HD + blk:MOBA_HD + blk + 1, :] = m
            km_scr[1, blk:blk + 1, :] = m

    q = q_ref[...]
    t0 = pl.multiple_of(t * BLK, BLK)
    kt = k_ref[pl.ds(t0, BLK), :].astype(BF16)
    vt = v_ref[pl.ds(t0, BLK), :].astype(BF16)
    rowi = lax.broadcasted_iota(jnp.int32, (BLK, 1), 0)
    coli = lax.broadcasted_iota(jnp.int32, (1, BLK), 1)
    outs = []
    for e in range(2):
        in_e = (lane >= e * MOBA_HD) & (lane < (e + 1) * MOBA_HD)
        off = MOBA_HD if e == 0 else 0
        qm = jnp.where(in_e, q, 0.0)
        gate = _dot(qm, km_scr[e], NT, precision=lax.Precision.HIGHEST)
        blkid = lane - off
        g = jnp.where((blkid >= 0) & (blkid < t), gate, -jnp.inf)
        sel = _top3_select(g, lane)
        in_blk = (blkid >= 0) & (blkid < nblk)
        qaug = jnp.where(in_e, q, jnp.where(in_blk & jnp.logical_not(sel), -MASK_BIG, 0.0)).astype(BF16)

        s = _dot(qm.astype(BF16), kt, NT)
        s = jnp.where(coli <= rowi, s, NEG_BIG)
        m0 = jnp.max(s, axis=-1, keepdims=True)
        p = jnp.exp(s - m0)
        l0 = jnp.sum(p, axis=-1, keepdims=True)
        acc0 = _dot(p.astype(BF16), vt)

        def body(j, carry, in_e=in_e, off=off, qaug=qaug):
            m, l, acc = carry
            j0 = pl.multiple_of(j * BLK, BLK)
            kj = k_ref[pl.ds(j0, BLK), :]
            vj = v_ref[pl.ds(j0, BLK), :].astype(BF16)
            kaug = jnp.where(in_e, kj, jnp.where(lane == off + j, 1.0, 0.0)).astype(BF16)
            s = _dot(qaug, kaug, NT)
            mn = jnp.maximum(m, jnp.max(s, axis=-1, keepdims=True))
            a = jnp.exp(m - mn)
            p = jnp.exp(s - mn)
            l = a * l + jnp.sum(p, axis=-1, keepdims=True)
            acc = a * acc + _dot(p.astype(BF16), vj)
            return mn, l, acc

        m, l, acc = lax.fori_loop(0, t, body, (m0, l0, acc0))
        outs.append(acc / l)
    o_ref[...] = jnp.where(lane < MOBA_HD, outs[0], outs[1]).astype(o_ref.dtype)


def _moba_prompt(z, *, nseq, seq_rows):
    nblk = seq_rows // MOBA_BLOCK
    npair = MOBA_HEADS // 2
    return pl.pallas_call(
        functools.partial(_moba_prompt_kernel, nblk=nblk),
        grid=(nseq, npair, nblk),
        in_specs=[
            pl.BlockSpec((MOBA_BLOCK, LANES), lambda b, hp, t: (b * nblk + t, Z_MQ // LANES + hp)),
            pl.BlockSpec((seq_rows, LANES), lambda b, hp, t: (b, Z_MK // LANES + hp)),
            pl.BlockSpec((seq_rows, LANES), lambda b, hp, t: (b, Z_MV // LANES + hp)),
        ],
        out_specs=pl.BlockSpec((MOBA_BLOCK, LANES), lambda b, hp, t: (b * nblk + t, hp)),
        out_shape=jax.ShapeDtypeStruct((nseq * seq_rows, BRANCH_W), BF16),
        scratch_shapes=[pltpu.VMEM((2, LANES, LANES), F32)],
        compiler_params=_cparams(("parallel", "parallel", "arbitrary")),
        name="moba_prompt",
    )(z, z, z)


PAGES_PER_STEP = 16
PAGES_PER_BLOCK = MOBA_BLOCK // PAGE_SIZE


def _moba_select_kernel(pt_ref, q_ref, *rest, nfull):
    page_refs = rest[:PAGES_PER_STEP]
    idx_ref = rest[PAGES_PER_STEP]
    km_scr = rest[PAGES_PER_STEP + 1]
    g = pl.program_id(1)
    bps = PAGES_PER_STEP // PAGES_PER_BLOCK
    W = MOBA_HEADS * MOBA_HD
    rows = []
    for blk in range(bps):
        tot = None
        for p in range(PAGES_PER_BLOCK):
            s = jnp.sum(page_refs[blk * PAGES_PER_BLOCK + p][0], axis=0, keepdims=True)
            tot = s if tot is None else tot + s
        rows.append(tot * (1.0 / MOBA_BLOCK))
    km_scr[pl.ds(pl.multiple_of(g * bps, bps), bps), :] = jnp.concatenate(rows, axis=0)

    @pl.when(g == pl.num_programs(1) - 1)
    def _():
        q = q_ref[...]
        km = km_scr[...]
        lane_w = lax.broadcasted_iota(jnp.int32, (1, W), 1)
        lane = lax.broadcasted_iota(jnp.int32, (1, LANES), 1)
        lane_b = lax.broadcasted_iota(jnp.int32, (1, nfull), 1)
        out = jnp.zeros((SAMPLE_ROWS, LANES), jnp.int32)
        for h in range(MOBA_HEADS):
            qm = jnp.where((lane_w >= h * MOBA_HD) & (lane_w < (h + 1) * MOBA_HD), q, 0.0)
            gsc = _dot(qm, km, NT, precision=lax.Precision.HIGHEST)
            for r in range(MOBA_TOPK):
                m = jnp.max(gsc, axis=-1, keepdims=True)
                idx = jnp.min(jnp.where(gsc == m, lane_b, nfull), axis=-1, keepdims=True)
                out = jnp.where(lane == h * MOBA_TOPK + r, idx, out)
                gsc = jnp.where(lane_b == idx, -jnp.inf, gsc)
        idx_ref[0] = out


def _moba_select(z_s, cache_k3, pt_flat, *, layer, nseq, npages):
    nfull = npages // PAGES_PER_BLOCK
    W = MOBA_HEADS * MOBA_HD
    nstep = npages // PAGES_PER_STEP

    def page_spec(i):
        return pl.BlockSpec((1, PAGE_SIZE, W),
                            lambda b, g, pt, i=i: (pt[b * npages + g * PAGES_PER_STEP + i], 0, layer))

    grid_spec = pltpu.PrefetchScalarGridSpec(
        num_scalar_prefetch=1,
        grid=(nseq, nstep),
        in_specs=[pl.BlockSpec((SAMPLE_ROWS, W), lambda b, g, pt: (b, Z_MQ // W))]
        + [page_spec(i) for i in range(PAGES_PER_STEP)],
        out_specs=pl.BlockSpec((1, SAMPLE_ROWS, LANES), lambda b, g, pt: (b, 0, 0)),
        scratch_shapes=[pltpu.VMEM((nfull, W), F32)],
    )
    return pl.pallas_call(
        functools.partial(_moba_select_kernel, nfull=nfull),
        grid_spec=grid_spec,
        out_shape=jax.ShapeDtypeStruct((nseq, SAMPLE_ROWS, LANES), jnp.int32),
        compiler_params=_cparams(("parallel", "arbitrary")),
        name="moba_select",
    )(pt_flat, z_s, *([cache_k3] * PAGES_PER_STEP))


def _moba_attend_kernel(idx_ref, pt_ref, q_ref, kn_ref, vn_ref, *rest):
    npick = 2 * MOBA_TOPK
    kv_refs = rest[:npick * 2 * PAGES_PER_BLOCK]
    o_ref = rest[npick * 2 * PAGES_PER_BLOCK]
    t = pl.program_id(2)
    half = SAMPLE_ROWS // 2
    qrow = half + t
    lane = lax.broadcasted_iota(jnp.int32, (1, LANES), 1)
    rowi = lax.broadcasted_iota(jnp.int32, (SAMPLE_ROWS, 1), 0)
    coln = lax.broadcasted_iota(jnp.int32, (1, SAMPLE_ROWS), 1)
    q = q_ref[...]
    kn = kn_ref[...].astype(BF16)
    vn = vn_ref[...].astype(BF16)
    outs = []
    for e in range(2):
        in_e = (lane >= e * MOBA_HD) & (lane < (e + 1) * MOBA_HD)
        qm = jnp.where(in_e, q, 0.0).astype(BF16)
        s_new = _dot(qm, kn, NT)
        s_new = jnp.where((coln >= half) & (coln <= qrow), s_new, NEG_BIG)
        ss, vs = [], []
        for r in range(MOBA_TOPK):
            base = (e * MOBA_TOPK + r) * 2 * PAGES_PER_BLOCK
            kb = jnp.concatenate([kv_refs[base + p][0] for p in range(PAGES_PER_BLOCK)], axis=0)
            vb = jnp.concatenate([kv_refs[base + PAGES_PER_BLOCK + p][0] for p in range(PAGES_PER_BLOCK)], axis=0)
            ss.append(_dot(qm, kb.astype(BF16), NT))
            vs.append(vb.astype(BF16))
        m = jnp.max(s_new, axis=-1, keepdims=True)
        for s in ss:
            m = jnp.maximum(m, jnp.max(s, axis=-1, keepdims=True))
        p_new = jnp.exp(s_new - m)
        l = jnp.sum(p_new, axis=-1, keepdims=True)
        o = _dot(p_new.astype(BF16), vn)
        for s, vb in zip(ss, vs):
            p = jnp.exp(s - m)
            l = l + jnp.sum(p, axis=-1, keepdims=True)
            o = o + _dot(p.astype(BF16), vb)
        outs.append(o / l)
    out = jnp.where(lane < MOBA_HD, outs[0], outs[1])

    @pl.when(t == 0)
    def _():
        o_ref[...] = jnp.where(rowi == qrow, out, 0.0)

    @pl.when(t != 0)
    def _():
        o_ref[...] = jnp.where(rowi == qrow, out, o_ref[...])


def _moba_attend(z_s, cache_k3, cache_v3, idx_flat, pt_flat, *, layer, nseq, npages, ntok):
    npair = MOBA_HEADS // 2

    def kv_spec(e, r, p):
        def imap(b, hp, t, idx, pt):
            h = hp * 2 + e
            blk = idx[((b * ntok + t) * MOBA_HEADS + h) * MOBA_TOPK + r]
            return (pt[b * npages + blk * PAGES_PER_BLOCK + p], 0, layer * npair + hp)
        return pl.BlockSpec((1, PAGE_SIZE, LANES), imap)

    kv_specs, kv_args = [], []
    for e in range(2):
        for r in range(MOBA_TOPK):
            for arr in (cache_k3, cache_v3):
                for p in range(PAGES_PER_BLOCK):
                    kv_specs.append(kv_spec(e, r, p))
                    kv_args.append(arr)
    grid_spec = pltpu.PrefetchScalarGridSpec(
        num_scalar_prefetch=2,
        grid=(nseq, npair, ntok),
        in_specs=[
            pl.BlockSpec((SAMPLE_ROWS, LANES), lambda b, hp, t, idx, pt: (b, Z_MQ // LANES + hp)),
            pl.BlockSpec((SAMPLE_ROWS, LANES), lambda b, hp, t, idx, pt: (b, Z_MK // LANES + hp)),
            pl.BlockSpec((SAMPLE_ROWS, LANES), lambda b, hp, t, idx, pt: (b, Z_MV // LANES + hp)),
        ] + kv_specs,
        out_specs=pl.BlockSpec((SAMPLE_ROWS, LANES), lambda b, hp, t, idx, pt: (b, hp)),
    )
    return pl.pallas_call(
        _moba_attend_kernel,
        grid_spec=grid_spec,
        out_shape=jax.ShapeDtypeStruct((nseq * SAMPLE_ROWS, BRANCH_W), F32),
        compiler_params=_cparams(("parallel", "parallel", "arbitrary")),
        name="moba_attend",
    )(idx_flat, pt_flat, z_s, z_s, z_s, *kv_args)


def _post_kernel(x_ref, org_ref, om_ref, gl_ref, wb_ref, wo_ref, gpost_ref, gpre_ref, x1_ref, h2_ref):
    merged = None
    for n in range(N_BRANCH):
        if n < 2:
            br = org_ref[:, n * BRANCH_W:(n + 1) * BRANCH_W]
        else:
            br = om_ref[...]
        proj = _dot(br.astype(BF16), wb_ref[n])
        term = _sigmoid(gl_ref[:, n * D_MODEL:(n + 1) * D_MODEL]) * proj
        merged = term if merged is None else merged + term
    y = _dot(merged.astype(BF16), wo_ref[...])
    x1 = x_ref[...] + _rms(y, gpost_ref[...])
    x1_ref[...] = x1
    h2_ref[...] = _rms(x1, gpre_ref[...]).astype(BF16)


def _post(x, org, om, z, wb, wo, gpost, gpre, *, tm):
    n = x.shape[0]
    c2 = lambda i: (0, 0)
    return pl.pallas_call(
        _post_kernel,
        grid=(n // tm,),
        in_specs=[
            pl.BlockSpec((tm, D_MODEL), lambda i: (i, 0)),
            pl.BlockSpec((tm, 2 * BRANCH_W), lambda i: (i, 0)),
            pl.BlockSpec((tm, BRANCH_W), lambda i: (i, 0)),
            pl.BlockSpec((tm, N_BRANCH * D_MODEL), lambda i: (i, 0)),
            pl.BlockSpec((N_BRANCH, BRANCH_W, D_MODEL), lambda i: (0, 0, 0)),
            pl.BlockSpec((D_MODEL, D_MODEL), c2),
            pl.BlockSpec((1, D_MODEL), c2),
            pl.BlockSpec((1, D_MODEL), c2),
        ],
        out_specs=[pl.BlockSpec((tm, D_MODEL), lambda i: (i, 0)),
                   pl.BlockSpec((tm, D_MODEL), lambda i: (i, 0))],
        out_shape=[jax.ShapeDtypeStruct((n, D_MODEL), F32), jax.ShapeDtypeStruct((n, D_MODEL), BF16)],
        compiler_params=_cparams(("parallel",)),
        name="post",
    )(x, org, om, z, wb, wo, gpost, gpre)


FF_CHUNK = 256


def _ffn_kernel(*refs, sample, tiles_per_seq, tail_rows):
    if sample:
        (x1_ref, h2_ref, st_ref, wup_ref, wc_ref, bc_ref, wd_ref, g_ref, x2_ref, tail_ref, acc_scr) = refs
    else:
        (x1_ref, h2_ref, halo_ref, wup_ref, wc_ref, bc_ref, wd_ref, g_ref, x2_ref, tail_ref, acc_scr) = refs
    i = pl.program_id(0)
    tm = x1_ref.shape[0]
    h2 = h2_ref[...]
    row = lax.broadcasted_iota(jnp.int32, (tm, 1), 0)
    row8 = lax.broadcasted_iota(jnp.int32, (SUBLANES, 1), 0)
    if not sample:
        halo = jnp.where((i % tiles_per_seq) == 0, jnp.zeros_like(halo_ref[...]), halo_ref[...])

    def shifted(u, uh, sh):
        us = pltpu.roll(u, sh, 0)
        if uh is None:
            return us
        hs = pltpu.roll(uh, sh, 0)
        head = jnp.where(row8 < sh, hs, us[:SUBLANES])
        return jnp.concatenate([head, us[SUBLANES:]], axis=0)

    for c in range(D_FF // FF_CHUNK):
        halves = []
        for half in range(2):
            c0 = half * D_FF + c * FF_CHUNK
            w = wup_ref[:, c0:c0 + FF_CHUNK]
            u = _dot(h2, w)
            if sample:
                inj = ((row % SAMPLE_ROWS) >= SAMPLE_ROWS // 2 - (CONV_W - 1)) & ((row % SAMPLE_ROWS) < SAMPLE_ROWS // 2)
                u = jnp.where(inj, st_ref[:, c0:c0 + FF_CHUNK], u)
                uh = None
            else:
                uh = _dot(halo, w)
            tail_ref[0, :, c0:c0 + FF_CHUNK] = u[tm - tail_rows:]
            conv = bc_ref[:, c0:c0 + FF_CHUNK] + u * wc_ref[CONV_W - 1:CONV_W, c0:c0 + FF_CHUNK]
            for sh in range(1, CONV_W):
                conv = conv + shifted(u, uh, sh) * wc_ref[CONV_W - 1 - sh:CONV_W - sh, c0:c0 + FF_CHUNK]
            halves.append(conv)
        a, b = halves
        gelu = 0.5 * a * (1.0 + jnp.tanh(0.7978845608028654 * (a + 0.044715 * a * a * a)))
        part = _dot((gelu * b).astype(BF16), wd_ref[c * FF_CHUNK:(c + 1) * FF_CHUNK, :])
        if c == 0:
            acc_scr[...] = part
        else:
            acc_scr[...] += part
    x2 = x1_ref[...] + _rms(acc_scr[...], g_ref[...])
    if sample:
        x2 = jnp.where((row % SAMPLE_ROWS) >= SAMPLE_ROWS // 2, x2, 0.0)
    x2_ref[...] = x2


def _ffn(x1, h2, extra, wup, wc, bc, wd, g, *, tm, sample, nseq):
    n = x1.shape[0]
    ntile = n // tm
    tiles_per_seq = ntile // nseq if not sample else 1
    tail_rows = tm if sample else SUBLANES
    ntail = ntile if sample else nseq
    c2 = lambda i: (0, 0)
    if sample:
        extra_spec = pl.BlockSpec((tm, 2 * D_FF), lambda i: (i, 0))
        extra_arg = extra
    else:
        hb = tm // SUBLANES
        extra_spec = pl.BlockSpec((SUBLANES, D_MODEL), lambda i: (jnp.maximum(i * hb - 1, 0), 0))
        extra_arg = h2
    return pl.pallas_call(
        functools.partial(_ffn_kernel, sample=sample, tiles_per_seq=tiles_per_seq, tail_rows=tail_rows),
        grid=(ntile,),
        in_specs=[
            pl.BlockSpec((tm, D_MODEL), lambda i: (i, 0)),
            pl.BlockSpec((tm, D_MODEL), lambda i: (i, 0)),
            extra_spec,
            pl.BlockSpec((D_MODEL, 2 * D_FF), c2, pipeline_mode=pl.Buffered(1)),
            pl.BlockSpec((CONV_W, 2 * D_FF), c2),
            pl.BlockSpec((1, 2 * D_FF), c2),
            pl.BlockSpec((D_FF, D_MODEL), c2, pipeline_mode=pl.Buffered(1)),
            pl.BlockSpec((1, D_MODEL), c2),
        ],
        out_specs=[pl.BlockSpec((tm, D_MODEL), lambda i: (i, 0)),
                   pl.BlockSpec((1, tail_rows, 2 * D_FF), lambda i: (i // tiles_per_seq, 0, 0))],
        out_shape=[jax.ShapeDtypeStruct((n, D_MODEL), F32),
                   jax.ShapeDtypeStruct((ntail, tail_rows, 2 * D_FF), F32)],
        scratch_shapes=[pltpu.VMEM((tm, D_MODEL), F32)],
        compiler_params=_cparams(("arbitrary",)),
        name="ffn",
    )(x1, h2, extra_arg, wup, wc, bc, wd, g)


def _rope_tables(pos):
    half = MOBA_HD // 2
    freqs = jnp.power(ROPE_THETA, -jnp.arange(half, dtype=F32) / half)
    ang = pos.astype(F32)[:, None] * freqs[None, :]
    cos, sin = jnp.cos(ang), jnp.sin(ang)
    return (jnp.concatenate([cos, cos, cos, cos], axis=1),
            jnp.concatenate([-sin, sin, -sin, sin], axis=1))


def kernel(x_prompt, x_sample, cache_k, cache_v, page_table, state_ret, state_gla, state_conv, g_pre_mix, w_in, w_gla_a2, b_gla_a, g_ret_norm, g_gla_norm, w_branch, w_out, g_post_mix, g_pre_ffn, w_up, w_conv, b_conv, w_down, g_post_ffn):
    B, S, D = x_prompt.shape
    Bd, T, _ = x_sample.shape
    depth = w_in.shape[0]
    npages = page_table.shape[1]
    past_len = npages * PAGE_SIZE
    half = SAMPLE_ROWS // 2
    assert D == D_MODEL and T == half and S % 1024 == 0
    assert past_len % MOBA_BLOCK == 0 and npages % PAGES_PER_STEP == 0
    assert past_len // MOBA_BLOCK >= MOBA_TOPK
    n_pool = cache_k.shape[0]
    W = MOBA_HEADS * MOBA_HD

    xp = x_prompt.reshape(B * S, D)
    xs = jnp.pad(x_sample, ((0, 0), (half, 0), (0, 0))).reshape(Bd * SAMPLE_ROWS, D)
    ns = Bd * SAMPLE_ROWS

    cos_p, sin_p = _rope_tables(jnp.arange(S, dtype=jnp.int32))
    pos_s = past_len + jnp.tile(jnp.arange(SAMPLE_ROWS, dtype=jnp.int32) - half, Bd)
    cos_s, sin_s = _rope_tables(pos_s)

    tab_p = _decay_tables(CHUNK, np.ones(CHUNK))
    tab_s = _decay_tables(SAMPLE_ROWS, np.concatenate([np.zeros(half), np.ones(half)]))

    cache_k3 = cache_k.reshape(n_pool, PAGE_SIZE, depth * W)
    cache_v3 = cache_v.reshape(n_pool, PAGE_SIZE, depth * W)
    pt_flat = page_table.reshape(-1).astype(jnp.int32)

    outs = {k: [] for k in ("ret_p", "ret_s", "gla_p", "gla_s", "k_p", "v_p", "k_s", "v_s", "conv_p", "conv_s")}
    for l in range(depth):
        w = w_in[l]
        o_ga = 2 * RET_HEADS * RET_DK + 2 * BRANCH_W + 2 * GLA_HEADS * GLA_DK + 2 * BRANCH_W
        o_m = o_ga + GLA_RANK
        o_gl = o_m + 3 * BRANCH_W
        w_main = jnp.concatenate([w[:, o_gl:], w[:, :o_ga], w[:, o_m:o_gl]], axis=1).astype(BF16)
        w_ga = jnp.pad(w[:, o_ga:o_m], ((0, 0), (0, LANES - GLA_RANK))).astype(BF16)
        w2 = jnp.pad(w_gla_a2[l], ((0, LANES - GLA_RANK), (0, 0))).astype(BF16)
        ba = b_gla_a[l][None, :]
        g_pre = g_pre_mix[l][None, :]
        g_ret = g_ret_norm[l][None, :]
        g_gla = g_gla_norm[l][None, :]
        wb = w_branch[l].astype(BF16)
        wo = w_out[l].astype(BF16)
        gpost = g_post_mix[l][None, :]
        gpre2 = g_pre_ffn[l][None, :]
        wup = w_up[l].astype(BF16)
        wd = w_down[l].astype(BF16)
        wc = w_conv[l]
        bc = b_conv[l][None, :]
        gffn = g_post_ffn[l][None, :]

        zp, lap = _inproj(xp, g_pre, w_main, w_ga, w2, ba, cos_p, sin_p, tm=1024, sample=False)
        orgp, retp, glap = _scan(zp, lap, g_ret, g_gla, tab_p, None, nseq=B, seq_rows=S, T=512,
                                 C=CHUNK, SC=SUBCHUNK, out_dtype=BF16)
        omp = _moba_prompt(zp, nseq=B, seq_rows=S)
        x1p, h2p = _post(xp, orgp, omp, zp, wb, wo, gpost, gpre2, tm=512)
        xp, tailp = _ffn(x1p, h2p, None, wup, wc, bc, wd, gffn, tm=512, sample=False, nseq=B)
        outs["ret_p"].append(retp)
        outs["gla_p"].append(glap)
        outs["k_p"].append(zp[:, Z_MK:Z_MK + W].reshape(B, S, MOBA_HEADS, MOBA_HD))
        outs["v_p"].append(zp[:, Z_MV:Z_MV + W].reshape(B, S, MOBA_HEADS, MOBA_HD))
        outs["conv_p"].append(tailp[:, SUBLANES - (CONV_W - 1):, :])

        zs, las = _inproj(xs, g_pre, w_main, w_ga, w2, ba, cos_s, sin_s, tm=ns, sample=True)
        orgs, rets, glas = _scan(zs, las, g_ret, g_gla, tab_s, (state_ret[l], state_gla[l]), nseq=Bd,
                                 seq_rows=SAMPLE_ROWS, T=SAMPLE_ROWS, C=SAMPLE_ROWS, SC=SAMPLE_ROWS,
                                 out_dtype=F32)
        idx = _moba_select(zs, cache_k3, pt_flat, layer=l, nseq=Bd, npages=npages)
        idx_flat = idx[:, half:, :MOBA_HEADS * MOBA_TOPK].reshape(-1)
        oms = _moba_attend(zs, cache_k3, cache_v3, idx_flat, pt_flat, layer=l, nseq=Bd, npages=npages, ntok=T)
        x1s, h2s = _post(xs, orgs, oms, zs, wb, wo, gpost, gpre2, tm=ns)
        st = jnp.pad(state_conv[l], ((0, 0), (half - (CONV_W - 1), half), (0, 0))).reshape(ns, 2 * D_FF)
        xs, tails = _ffn(x1s, h2s, st, wup, wc, bc, wd, gffn, tm=ns, sample=True, nseq=Bd)
        zs3 = zs.reshape(Bd, SAMPLE_ROWS, Z_W)
        outs["ret_s"].append(rets)
        outs["gla_s"].append(glas)
        outs["k_s"].append(zs3[:, half:, Z_MK:Z_MK + W].reshape(Bd, T, MOBA_HEADS, MOBA_HD))
        outs["v_s"].append(zs3[:, half:, Z_MV:Z_MV + W].reshape(Bd, T, MOBA_HEADS, MOBA_HD))
        outs["conv_s"].append(tails.reshape(Bd, SAMPLE_ROWS, 2 * D_FF)[:, SAMPLE_ROWS - (CONV_W - 1):, :])

    yp = xp.reshape(B, S, D)
    ys = xs.reshape(Bd, SAMPLE_ROWS, D)[:, half:, :]
    return (yp, ys,
            jnp.stack(outs["ret_p"], axis=0), jnp.stack(outs["ret_s"], axis=0),
            jnp.stack(outs["gla_p"], axis=0), jnp.stack(outs["gla_s"], axis=0),
            jnp.stack(outs["k_p"], axis=2), jnp.stack(outs["v_p"], axis=2),
            jnp.stack(outs["k_s"], axis=2), jnp.stack(outs["v_s"], axis=2),
            jnp.stack(outs["conv_p"], axis=0), jnp.stack(outs["conv_s"], axis=0))
```

```python
import functools

import numpy as np
import jax
import jax.numpy as jnp
from jax import lax
from jax.experimental import pallas as pl
from jax.experimental.pallas import tpu as pltpu

F32 = jnp.float32
BF16 = jnp.bfloat16

D_MODEL = 1024
BRANCH_W = 512
N_BRANCH = 3
RET_HEADS = 4
RET_DK = 64
RET_DV = 128
GLA_HEADS = 4
GLA_DK = 64
GLA_DV = 128
GLA_RANK = 16
GLA_TAU = 16.0
MOBA_HEADS = 8
MOBA_HD = 64
MOBA_BLOCK = 256
MOBA_TOPK = 3
ROPE_THETA = 10000.0
CHUNK = 64
SUBCHUNK = 16
D_FF = 2816
CONV_W = 3
EPS = 1e-6
PAGE_SIZE = 128

LANES = 128
SUBLANES = 8
VMEM_LIMIT = 56 * 1024 * 1024

Z_GL = 0
Z_RQ = 3072
Z_RK = 3328
Z_RV = 3584
Z_RG = 4096
Z_GQ = 4608
Z_GK = 4864
Z_GV = 5120
Z_GG = 5632
Z_MQ = 6144
Z_MK = 6656
Z_MV = 7168
Z_W = 7680
SEG_W = 512
SEG_RQK = Z_RQ // SEG_W
SEG_GQK = Z_GQ // SEG_W
SEG_MQ = Z_MQ // SEG_W
SEG_MK = Z_MK // SEG_W

SAMPLE_ROWS = 8
NEG_BIG = -1e30
MASK_BIG = 2.0 ** 100
LOG2E = 1.4426950408889634
MOBA_GROUP = 4

NT = (((1,), (1,)), ((), ()))
TN = (((0,), (0,)), ((), ()))


def _dot(a, b, dims=None, precision=None):
    if dims is None:
        return jnp.dot(a, b, preferred_element_type=F32, precision=precision)
    return lax.dot_general(a, b, dims, preferred_element_type=F32, precision=precision)


def _rms(x, g):
    return x * lax.rsqrt(jnp.mean(x * x, axis=-1, keepdims=True) + EPS) * g


def _silu(x):
    return x / (1.0 + jnp.exp(-x))


def _sigmoid(x):
    return 1.0 / (1.0 + jnp.exp(-x))


def _cparams(sem):
    return pltpu.CompilerParams(dimension_semantics=sem, vmem_limit_bytes=VMEM_LIMIT)


def _inproj_kernel(x_ref, g_ref, w_ref, wga_ref, w2_ref, ba_ref, cos_ref, sin_ref,
                   z_ref, la_ref, h_scr, *, sample):
    j = pl.program_id(1)

    @pl.when(j == 0)
    def _():
        hb = _rms(x_ref[...], g_ref[...]).astype(BF16)
        h_scr[...] = hb
        ga = _dot(hb, wga_ref[...])
        u = _dot(ga.astype(BF16), w2_ref[...]) + ba_ref[...]
        la = (jnp.minimum(u, 0.0) - jnp.log(1.0 + jnp.exp(-jnp.abs(u)))) * (1.0 / GLA_TAU)
        if sample:
            row = lax.broadcasted_iota(jnp.int32, (la.shape[0], 1), 0)
            la = jnp.where((row % SAMPLE_ROWS) >= SAMPLE_ROWS // 2, la, 0.0)
        la_ref[...] = la

    acc = _dot(h_scr[...], w_ref[...])
    lane = lax.broadcasted_iota(jnp.int32, (1, LANES), 1)
    first = (lane % MOBA_HD) < (MOBA_HD // 2)

    def rope(a):
        sw = jnp.where(first, pltpu.roll(a, LANES - MOBA_HD // 2, 1), pltpu.roll(a, MOBA_HD // 2, 1))
        return a * cos_ref[...] + sw * sin_ref[...]

    def store_rope(scales):
        for c, s in enumerate(scales):
            r = rope(acc[:, c * LANES:(c + 1) * LANES])
            z_ref[:, c * LANES:(c + 1) * LANES] = r if s == 1.0 else r * s

    ks = RET_DK ** -0.5

    @pl.when(j == SEG_RQK)
    def _():
        store_rope((1.0, 1.0, ks, ks))

    @pl.when(j == SEG_GQK)
    def _():
        z_ref[:, :SEG_W // 2] = acc[:, :SEG_W // 2]
        z_ref[:, SEG_W // 2:] = acc[:, SEG_W // 2:] * (GLA_DK ** -0.5)

    @pl.when(j == SEG_MQ)
    def _():
        store_rope((MOBA_HD ** -0.5,) * 4)

    @pl.when(j == SEG_MK)
    def _():
        store_rope((1.0,) * 4)

    plain = (j != SEG_RQK) & (j != SEG_GQK) & (j != SEG_MQ) & (j != SEG_MK)

    @pl.when(plain)
    def _():
        z_ref[...] = acc


def _inproj(x, g, w_main, w_ga, w2, ba, cos, sin, *, tm, sample):
    n = x.shape[0]
    nseg = Z_W // SEG_W
    ntab = cos.shape[0] // tm
    return pl.pallas_call(
        functools.partial(_inproj_kernel, sample=sample),
        grid=(n // tm, nseg),
        in_specs=[
            pl.BlockSpec((tm, D_MODEL), lambda i, j: (i, 0)),
            pl.BlockSpec((1, D_MODEL), lambda i, j: (0, 0)),
            pl.BlockSpec((D_MODEL, SEG_W), lambda i, j: (0, j)),
            pl.BlockSpec((D_MODEL, LANES), lambda i, j: (0, 0)),
            pl.BlockSpec((LANES, GLA_HEADS * GLA_DK), lambda i, j: (0, 0)),
            pl.BlockSpec((1, GLA_HEADS * GLA_DK), lambda i, j: (0, 0)),
            pl.BlockSpec((tm, LANES), lambda i, j: (i % ntab, 0)),
            pl.BlockSpec((tm, LANES), lambda i, j: (i % ntab, 0)),
        ],
        out_specs=[
            pl.BlockSpec((tm, SEG_W), lambda i, j: (i, j)),
            pl.BlockSpec((tm, GLA_HEADS * GLA_DK), lambda i, j: (i, 0)),
        ],
        out_shape=[jax.ShapeDtypeStruct((n, Z_W), F32),
                   jax.ShapeDtypeStruct((n, GLA_HEADS * GLA_DK), F32)],
        scratch_shapes=[pltpu.VMEM((tm, D_MODEL), BF16)],
        compiler_params=_cparams(("parallel", "arbitrary")),
        name="inproj",
    )(x, g, w_main, w_ga, w2, ba, cos, sin)


def _decay_tables(c, valid):
    log_g = np.log1p(-np.exp2(-5.0 - np.arange(RET_HEADS, dtype=np.float64)))
    b = log_g[:, None] * np.cumsum(np.asarray(valid, np.float64))[None, :]
    diff = b[:, :, None] - b[:, None, :]
    causal = np.arange(c)[:, None] >= np.arange(c)[None, :]
    dmat = np.where(causal[None], np.exp(np.where(causal[None], diff, 0.0)), 0.0)
    qdec = np.broadcast_to(np.exp(b)[:, :, None], (RET_HEADS, c, RET_DK))
    kdec = np.broadcast_to(np.exp(b[:, -1:] - b)[:, :, None], (RET_HEADS, c, RET_DK))
    cdec = tuple(float(v) for v in np.exp(b[:, -1]))
    tri = causal.astype(np.float32)
    return (jnp.asarray(dmat, F32), jnp.asarray(qdec, F32), jnp.asarray(kdec, F32),
            jnp.asarray(tri, F32), cdec)


def _scan_kernel(*refs, C, SC, nchunk, cdec, zero_init):
    if zero_init:
        (zr_ref, zg_ref, la_ref, dmat_ref, qdec_ref, kdec_ref, tri_ref, gr_ref, gg_ref,
         o_ref, sr_ref, sg_ref, Sr, SgT) = refs
    else:
        (zr_ref, zg_ref, la_ref, dmat_ref, qdec_ref, kdec_ref, tri_ref, gr_ref, gg_ref,
         s0r_ref, s0g_ref, o_ref, sr_ref, sg_ref, Sr, SgT) = refs
    t = pl.program_id(1)
    nsub = C // SC
    HK = RET_HEADS * RET_DK

    @pl.when(t == 0)
    def _():
        if zero_init:
            Sr[...] = jnp.zeros_like(Sr)
            SgT[...] = jnp.zeros_like(SgT)
        else:
            Sr[...] = s0r_ref[0]
            for h in range(GLA_HEADS):
                SgT[h] = s0g_ref[0, h].T

    row = lax.broadcasted_iota(jnp.int32, (C, 1), 0)
    rsub = row % SC
    lane_c = lax.broadcasted_iota(jnp.int32, (1, C), 1)

    def bcast_rows(x, jj):
        parts = [jnp.broadcast_to(x[i * SC + jj:i * SC + jj + 1, :], (SC, x.shape[1]))
                 for i in range(nsub)]
        return parts[0] if nsub == 1 else jnp.concatenate(parts, axis=0)

    def chunk(ci, carry):
        r0 = pl.multiple_of(ci * C, C)
        rows = pl.ds(r0, C)

        q = zr_ref[rows, 0:HK]
        k = zr_ref[rows, HK:2 * HK]
        v = zr_ref[rows, 2 * HK:2 * HK + BRANCH_W]
        rg = zr_ref[rows, 2 * HK + BRANCH_W:2 * HK + 2 * BRANCH_W]
        for h in range(RET_HEADS):
            qh = q[:, h * RET_DK:(h + 1) * RET_DK]
            kh = k[:, h * RET_DK:(h + 1) * RET_DK]
            vh = v[:, h * RET_DV:(h + 1) * RET_DV].astype(BF16)
            att = _dot(qh.astype(BF16), kh.astype(BF16), NT) * dmat_ref[h]
            s_old = Sr[h]
            o = _dot(att.astype(BF16), vh) + _dot((qh * qdec_ref[h]).astype(BF16), s_old.astype(BF16))
            Sr[h] = cdec[h] * s_old + _dot((kh * kdec_ref[h]).T.astype(BF16), vh)
            cen = o - jnp.mean(o, axis=-1, keepdims=True)
            y = cen * lax.rsqrt(jnp.mean(cen * cen, axis=-1, keepdims=True) + EPS)
            y = y * gr_ref[:, h * RET_DV:(h + 1) * RET_DV] * _silu(rg[:, h * RET_DV:(h + 1) * RET_DV])
            o_ref[rows, h * RET_DV:(h + 1) * RET_DV] = y.astype(o_ref.dtype)

        q = zg_ref[rows, 0:HK]
        k = zg_ref[rows, HK:2 * HK]
        v = zg_ref[rows, 2 * HK:2 * HK + BRANCH_W]
        gg = zg_ref[rows, 2 * HK + BRANCH_W:2 * HK + 2 * BRANCH_W]
        b = _dot(tri_ref[...], la_ref[rows, :], precision=lax.Precision.HIGHEST)
        blast = b[C - 1:C, :]
        qe = q * jnp.exp(b)
        kd = k * jnp.exp(blast - b)
        eblast = jnp.exp(blast)

        att = [jnp.zeros((C, C), F32) for _ in range(GLA_HEADS)]
        for jj in range(SC):
            kb = bcast_rows(k, jj)
            bb = bcast_rows(b, jj)
            w = q * kb * jnp.exp(jnp.where(rsub >= jj, b - bb, NEG_BIG))
            hit = lane_c == (row - rsub + jj)
            for h in range(GLA_HEADS):
                col = jnp.sum(w[:, h * GLA_DK:(h + 1) * GLA_DK], axis=-1, keepdims=True)
                att[h] = jnp.where(hit, col, att[h])
        for i in range(1, nsub):
            bi = b[i * SC - 1:i * SC, :]
            in_i = (row >= i * SC) & (row < (i + 1) * SC)
            qi = (q * jnp.exp(jnp.where(in_i, b - bi, NEG_BIG))).astype(BF16)
            ki = (k * jnp.exp(jnp.where(row < i * SC, bi - b, NEG_BIG))).astype(BF16)
            for h in range(GLA_HEADS):
                att[h] = att[h] + _dot(qi[:, h * GLA_DK:(h + 1) * GLA_DK],
                                       ki[:, h * GLA_DK:(h + 1) * GLA_DK], NT)
        for h in range(GLA_HEADS):
            vh = v[:, h * GLA_DV:(h + 1) * GLA_DV]
            vhb = vh.astype(BF16)
            st_old = SgT[h]
            o = _dot(att[h].astype(BF16), vhb) + _dot(
                qe[:, h * GLA_DK:(h + 1) * GLA_DK].astype(BF16), st_old.astype(BF16), NT)
            SgT[h] = st_old * eblast[:, h * GLA_DK:(h + 1) * GLA_DK] + _dot(
                vh.T.astype(BF16), kd[:, h * GLA_DK:(h + 1) * GLA_DK].astype(BF16))
            y = _rms(o, gg_ref[:, h * GLA_DV:(h + 1) * GLA_DV]) * _silu(gg[:, h * GLA_DV:(h + 1) * GLA_DV])
            o_ref[rows, BRANCH_W + h * GLA_DV:BRANCH_W + (h + 1) * GLA_DV] = y.astype(o_ref.dtype)
        return carry

    lax.fori_loop(0, nchunk, chunk, 0)

    @pl.when(t == pl.num_programs(1) - 1)
    def _():
        sr_ref[0] = Sr[...]
        for h in range(GLA_HEADS):
            sg_ref[0, h] = SgT[h].T


def _scan(z, la, g_ret, g_gla, tables, s0, *, nseq, seq_rows, T, C, SC, out_dtype):
    dmat, qdec, kdec, tri, cdec = tables
    nT = seq_rows // T
    zero_init = s0 is None
    zw = 2 * RET_HEADS * RET_DK + 2 * BRANCH_W
    const3 = lambda s, t: (0, 0, 0)
    const2 = lambda s, t: (0, 0)
    in_specs = [
        pl.BlockSpec((T, zw), lambda s, t: (s * nT + t, Z_RQ // zw)),
        pl.BlockSpec((T, zw), lambda s, t: (s * nT + t, Z_GQ // zw)),
        pl.BlockSpec((T, GLA_HEADS * GLA_DK), lambda s, t: (s * nT + t, 0)),
        pl.BlockSpec(dmat.shape, const3),
        pl.BlockSpec(qdec.shape, const3),
        pl.BlockSpec(kdec.shape, const3),
        pl.BlockSpec(tri.shape, const2),
        pl.BlockSpec((1, BRANCH_W), const2),
        pl.BlockSpec((1, BRANCH_W), const2),
    ]
    args = [z, z, la, dmat, qdec, kdec, tri, g_ret, g_gla]
    st_spec = pl.BlockSpec((1, RET_HEADS, RET_DK, RET_DV), lambda s, t: (s, 0, 0, 0))
    if not zero_init:
        in_specs += [st_spec, st_spec]
        args += [s0[0], s0[1]]
    st_shape = jax.ShapeDtypeStruct((nseq, RET_HEADS, RET_DK, RET_DV), F32)
    return pl.pallas_call(
        functools.partial(_scan_kernel, C=C, SC=SC, nchunk=T // C, cdec=cdec, zero_init=zero_init),
        grid=(nseq, nT),
        in_specs=in_specs,
        out_specs=[pl.BlockSpec((T, 2 * BRANCH_W), lambda s, t: (s * nT + t, 0)), st_spec, st_spec],
        out_shape=[jax.ShapeDtypeStruct((nseq * seq_rows, 2 * BRANCH_W), out_dtype), st_shape, st_shape],
        scratch_shapes=[pltpu.VMEM((RET_HEADS, RET_DK, RET_DV), F32),
                        pltpu.VMEM((GLA_HEADS, GLA_DV, GLA_DK), F32)],
        compiler_params=_cparams(("parallel", "arbitrary")),
        name="scan",
    )(*args)


def _top3_select(g, lane):
    sel = jnp.zeros(g.shape, jnp.bool_)
    for _ in range(MOBA_TOPK):
        m = jnp.max(g, axis=-1, keepdims=True)
        idx = jnp.min(jnp.where(g == m, lane, 4 * LANES), axis=-1, keepdims=True)
        pick = (lane == idx) & (m > -jnp.inf)
        sel = sel | pick
        g = jnp.where(pick, -jnp.inf, g)
    return sel


def _moba_prompt_kernel(q_ref, k_ref, v_ref, o_ref, km_scr, ka_scr, vb_scr, *, nblk):
    t = pl.program_id(2)
    BLK = MOBA_BLOCK
    G = MOBA_GROUP
    lane = lax.broadcasted_iota(jnp.int32, (1, LANES), 1)
    in_e = [(lane >= e * MOBA_HD) & (lane < (e + 1) * MOBA_HD) for e in range(2)]
    off = [MOBA_HD, 0]

    @pl.when(t == 0)
    def _():
        km_scr[...] = jnp.zeros_like(km_scr)
        for blk in range(nblk):
            m = jnp.mean(k_ref[blk * BLK:(blk + 1) * BLK, :], axis=0, keepdims=True)
            km_scr[0, MOBA_HD + blk:MOBA_HD + blk + 1, :] = m
            km_scr[1, blk:blk + 1, :] = m

        def prep(blk, c):
            r0 = pl.multiple_of(blk * BLK, BLK)
            kj = k_ref[pl.ds(r0, BLK), :]
            for e in range(2):
                ka_scr[e, pl.ds(r0, BLK), :] = jnp.where(
                    in_e[e], kj, jnp.where(lane == off[e] + blk, 1.0, 0.0)).astype(BF16)
            vb_scr[pl.ds(r0, BLK), :] = v_ref[pl.ds(r0, BLK), :].astype(BF16)
            return c
        lax.fori_loop(0, nblk, prep, 0)

    q = q_ref[...]
    q2 = q * LOG2E
    t0 = pl.multiple_of(t * BLK, BLK)
    vt = vb_scr[pl.ds(t0, BLK), :]
    rowi = lax.broadcasted_iota(jnp.int32, (BLK, 1), 0)
    coli = lax.broadcasted_iota(jnp.int32, (1, BLK), 1)
    qaug, init = [], []
    for e in range(2):
        qm = jnp.where(in_e[e], q, 0.0)
        gate = _dot(qm, km_scr[e], NT, precision=lax.Precision.HIGHEST)
        blkid = lane - off[e]
        g = jnp.where((blkid >= 0) & (blkid < t), gate, -jnp.inf)
        sel = _top3_select(g, lane)
        in_blk = (blkid >= 0) & (blkid < nblk)
        qaug.append(jnp.where(in_e[e], q2, jnp.where(in_blk & jnp.logical_not(sel), -MASK_BIG, 0.0)).astype(BF16))
        s = _dot(jnp.where(in_e[e], q2, 0.0).astype(BF16), ka_scr[e, pl.ds(t0, BLK), :], NT)
        s = jnp.where(coli <= rowi, s, NEG_BIG)
        m0 = jnp.max(s, axis=-1, keepdims=True)
        p = jnp.exp2(s - m0)
        init += [m0, jnp.sum(p, axis=-1, keepdims=True), _dot(p.astype(BF16), vt)]

    def body(g, carry):
        r0 = pl.multiple_of(g * (G * BLK), G * BLK)
        vg = vb_scr[pl.ds(r0, G * BLK), :]
        out = []
        for e in range(2):
            m, l, acc = carry[3 * e:3 * e + 3]
            s = _dot(qaug[e], ka_scr[e, pl.ds(r0, G * BLK), :], NT)
            mn = jnp.maximum(m, jnp.max(s, axis=-1, keepdims=True))
            a = jnp.exp2(m - mn)
            p = jnp.exp2(s - mn)
            out += [mn, a * l + jnp.sum(p, axis=-1, keepdims=True), a * acc + _dot(p.astype(BF16), vg)]
        return tuple(out)

    res = lax.fori_loop(0, (t + G - 1) // G, body, tuple(init))
    o_ref[...] = jnp.where(lane < MOBA_HD, res[2] / res[1], res[5] / res[4]).astype(o_ref.dtype)


def _moba_prompt(z, *, nseq, seq_rows):
    nblk = seq_rows // MOBA_BLOCK
    npair = MOBA_HEADS // 2
    assert nblk % MOBA_GROUP == 0 and nblk <= MOBA_HD
    return pl.pallas_call(
        functools.partial(_moba_prompt_kernel, nblk=nblk),
        grid=(nseq, npair, nblk),
        in_specs=[
            pl.BlockSpec((MOBA_BLOCK, LANES), lambda b, hp, t: (b * nblk + t, Z_MQ // LANES + hp)),
            pl.BlockSpec((seq_rows, LANES), lambda b, hp, t: (b, Z_MK // LANES + hp)),
            pl.BlockSpec((seq_rows, LANES), lambda b, hp, t: (b, Z_MV // LANES + hp)),
        ],
        out_specs=pl.BlockSpec((MOBA_BLOCK, LANES), lambda b, hp, t: (b * nblk + t, hp)),
        out_shape=jax.ShapeDtypeStruct((nseq * seq_rows, BRANCH_W), BF16),
        scratch_shapes=[pltpu.VMEM((2, LANES, LANES), F32),
                        pltpu.VMEM((2, seq_rows, LANES), BF16),
                        pltpu.VMEM((seq_rows, LANES), BF16)],
        compiler_params=_cparams(("parallel", "parallel", "arbitrary")),
        name="moba_prompt",
    )(z, z, z)


PAGES_PER_STEP = 16
PAGES_PER_BLOCK = MOBA_BLOCK // PAGE_SIZE


def _kmean_kernel(pt_ref, *refs):
    page_refs = refs[:PAGES_PER_STEP]
    km_ref = refs[PAGES_PER_STEP]
    depth = km_ref.shape[1]
    for blk in range(PAGES_PER_STEP // PAGES_PER_BLOCK):
        tot = None
        for p in range(PAGES_PER_BLOCK):
            s = jnp.sum(page_refs[blk * PAGES_PER_BLOCK + p][...], axis=0)
            tot = s if tot is None else tot + s
        tot = tot * (1.0 / MOBA_BLOCK)
        for l in range(depth):
            for h in range(MOBA_HEADS):
                km_ref[0, l, h, blk:blk + 1, :] = tot[l, h:h + 1, :]


def _kmean(cache_k, pt_flat, *, nseq, npages):
    depth = cache_k.shape[2]
    nfull = npages // PAGES_PER_BLOCK
    bps = PAGES_PER_STEP // PAGES_PER_BLOCK

    def page_spec(i):
        return pl.BlockSpec((None, PAGE_SIZE, depth, MOBA_HEADS, MOBA_HD),
                            lambda b, g, pt, i=i: (pt[b * npages + g * PAGES_PER_STEP + i], 0, 0, 0, 0))

    grid_spec = pltpu.PrefetchScalarGridSpec(
        num_scalar_prefetch=1,
        grid=(nseq, npages // PAGES_PER_STEP),
        in_specs=[page_spec(i) for i in range(PAGES_PER_STEP)],
        out_specs=pl.BlockSpec((1, depth, MOBA_HEADS, bps, MOBA_HD), lambda b, g, pt: (b, 0, 0, g, 0)),
    )
    return pl.pallas_call(
        _kmean_kernel,
        grid_spec=grid_spec,
        out_shape=jax.ShapeDtypeStruct((nseq, depth, MOBA_HEADS, nfull, MOBA_HD), F32),
        compiler_params=_cparams(("parallel", "arbitrary")),
        name="kmean",
    )(pt_flat, *([cache_k] * PAGES_PER_STEP))


def _moba_topk_kernel(q_ref, km_ref, idx_ref):
    nfull = km_ref.shape[1]
    lane = lax.broadcasted_iota(jnp.int32, (1, LANES), 1)
    lane_b = lax.broadcasted_iota(jnp.int32, (1, nfull), 1)
    out = jnp.zeros((SAMPLE_ROWS, LANES), jnp.int32)
    for h in range(MOBA_HEADS):
        gsc = _dot(q_ref[h], km_ref[h], NT, precision=lax.Precision.HIGHEST)
        for r in range(MOBA_TOPK):
            m = jnp.max(gsc, axis=-1, keepdims=True)
            idx = jnp.min(jnp.where(gsc == m, lane_b, nfull), axis=-1, keepdims=True)
            out = jnp.where(lane == h * MOBA_TOPK + r, idx, out)
            gsc = jnp.where(lane_b == idx, -jnp.inf, gsc)
    idx_ref[0] = out


def _moba_topk(qh, kmean, *, layer, nseq):
    nfull = kmean.shape[3]
    return pl.pallas_call(
        _moba_topk_kernel,
        grid=(nseq,),
        in_specs=[
            pl.BlockSpec((MOBA_HEADS, SAMPLE_ROWS, MOBA_HD), lambda b: (0, b, 0)),
            pl.BlockSpec((None, None, MOBA_HEADS, nfull, MOBA_HD), lambda b: (b, layer, 0, 0, 0)),
        ],
        out_specs=pl.BlockSpec((1, SAMPLE_ROWS, LANES), lambda b: (b, 0, 0)),
        out_shape=jax.ShapeDtypeStruct((nseq, SAMPLE_ROWS, LANES), jnp.int32),
        compiler_params=_cparams(("parallel",)),
        name="moba_topk",
    )(qh, kmean)


def _moba_attend_kernel(idx_ref, pt_ref, q_ref, kn_ref, vn_ref, ck_hbm, cv_hbm, o_ref, kbuf, vbuf, sem,
                        *, layer, npages, ntok):
    b = pl.program_id(0)
    h = pl.program_id(1)
    nh = pl.num_programs(1)
    step = b * nh + h
    nstep = pl.num_programs(0) * nh
    slot = step % 2

    def copies(bb, hh, sl):
        out = []
        for t in range(ntok):
            for r in range(MOBA_TOPK):
                blk = idx_ref[((bb * ntok + t) * MOBA_HEADS + hh) * MOBA_TOPK + r]
                for p in range(PAGES_PER_BLOCK):
                    page = pt_ref[bb * npages + blk * PAGES_PER_BLOCK + p]
                    i = (t * MOBA_TOPK + r) * PAGES_PER_BLOCK + p
                    out.append(pltpu.make_async_copy(ck_hbm.at[page, :, layer, hh, :], kbuf.at[sl, i], sem.at[0, sl]))
                    out.append(pltpu.make_async_copy(cv_hbm.at[page, :, layer, hh, :], vbuf.at[sl, i], sem.at[1, sl]))
        return out

    @pl.when(step == 0)
    def _():
        for c in copies(b, h, slot):
            c.start()

    @pl.when(step + 1 < nstep)
    def _():
        nxt = step + 1
        for c in copies(nxt // nh, nxt % nh, 1 - slot):
            c.start()

    for c in copies(b, h, slot):
        c.wait()

    half = SAMPLE_ROWS // 2
    rowi = lax.broadcasted_iota(jnp.int32, (SAMPLE_ROWS, 1), 0)
    coln = lax.broadcasted_iota(jnp.int32, (1, SAMPLE_ROWS), 1)
    qb = q_ref[...].astype(BF16)
    vn = vn_ref[...].astype(BF16)
    s_new_all = _dot(qb, kn_ref[...].astype(BF16), NT)
    out = jnp.zeros((SAMPLE_ROWS, MOBA_HD), F32)
    for t in range(ntok):
        qrow = half + t
        s_new = jnp.where((coln >= half) & (coln <= qrow), s_new_all, NEG_BIG)
        ss, vs = [], []
        for r in range(MOBA_TOPK):
            i0 = (t * MOBA_TOPK + r) * PAGES_PER_BLOCK
            kb = jnp.concatenate([kbuf[slot, i0 + p] for p in range(PAGES_PER_BLOCK)], axis=0)
            vb = jnp.concatenate([vbuf[slot, i0 + p] for p in range(PAGES_PER_BLOCK)], axis=0)
            ss.append(_dot(qb, kb.astype(BF16), NT))
            vs.append(vb.astype(BF16))
        m = jnp.max(s_new, axis=-1, keepdims=True)
        for s in ss:
            m = jnp.maximum(m, jnp.max(s, axis=-1, keepdims=True))
        p_new = jnp.exp(s_new - m)
        l = jnp.sum(p_new, axis=-1, keepdims=True)
        o = _dot(p_new.astype(BF16), vn)
        for s, vb in zip(ss, vs):
            p = jnp.exp(s - m)
            l = l + jnp.sum(p, axis=-1, keepdims=True)
            o = o + _dot(p.astype(BF16), vb)
        out = jnp.where(rowi == qrow, o / l, out)
    o_ref[...] = out


def _moba_attend(qh, knh, vnh, cache_k, cache_v, idx_flat, pt_flat, *, layer, nseq, npages, ntok):
    nbuf = ntok * MOBA_TOPK * PAGES_PER_BLOCK
    head_spec = pl.BlockSpec((None, SAMPLE_ROWS, MOBA_HD), lambda b, h, idx, pt: (h, b, 0))
    grid_spec = pltpu.PrefetchScalarGridSpec(
        num_scalar_prefetch=2,
        grid=(nseq, MOBA_HEADS),
        in_specs=[head_spec, head_spec, head_spec,
                  pl.BlockSpec(memory_space=pl.ANY), pl.BlockSpec(memory_space=pl.ANY)],
        out_specs=head_spec,
        scratch_shapes=[pltpu.VMEM((2, nbuf, PAGE_SIZE, MOBA_HD), F32),
                        pltpu.VMEM((2, nbuf, PAGE_SIZE, MOBA_HD), F32),
                        pltpu.SemaphoreType.DMA((2, 2))],
    )
    return pl.pallas_call(
        functools.partial(_moba_attend_kernel, layer=layer, npages=npages, ntok=ntok),
        grid_spec=grid_spec,
        out_shape=jax.ShapeDtypeStruct((MOBA_HEADS, nseq * SAMPLE_ROWS, MOBA_HD), F32),
        compiler_params=_cparams(("arbitrary", "arbitrary")),
        name="moba_attend",
    )(idx_flat, pt_flat, qh, knh, vnh, cache_k, cache_v)


def _post_kernel(x_ref, org_ref, om_ref, gl_ref, wb_ref, wo_ref, gpost_ref, gpre_ref, x1_ref, h2_ref):
    merged = None
    for n in range(N_BRANCH):
        if n < 2:
            br = org_ref[:, n * BRANCH_W:(n + 1) * BRANCH_W]
        else:
            br = om_ref[...]
        proj = _dot(br.astype(BF16), wb_ref[n])
        term = _sigmoid(gl_ref[:, n * D_MODEL:(n + 1) * D_MODEL]) * proj
        merged = term if merged is None else merged + term
    y = _dot(merged.astype(BF16), wo_ref[...])
    x1 = x_ref[...] + _rms(y, gpost_ref[...])
    x1_ref[...] = x1
    h2_ref[...] = _rms(x1, gpre_ref[...]).astype(BF16)


def _post(x, org, om, z, wb, wo, gpost, gpre, *, tm):
    n = x.shape[0]
    c2 = lambda i: (0, 0)
    return pl.pallas_call(
        _post_kernel,
        grid=(n // tm,),
        in_specs=[
            pl.BlockSpec((tm, D_MODEL), lambda i: (i, 0)),
            pl.BlockSpec((tm, 2 * BRANCH_W), lambda i: (i, 0)),
            pl.BlockSpec((tm, BRANCH_W), lambda i: (i, 0)),
            pl.BlockSpec((tm, N_BRANCH * D_MODEL), lambda i: (i, 0)),
            pl.BlockSpec((N_BRANCH, BRANCH_W, D_MODEL), lambda i: (0, 0, 0)),
            pl.BlockSpec((D_MODEL, D_MODEL), c2),
            pl.BlockSpec((1, D_MODEL), c2),
            pl.BlockSpec((1, D_MODEL), c2),
        ],
        out_specs=[pl.BlockSpec((tm, D_MODEL), lambda i: (i, 0)),
                   pl.BlockSpec((tm, D_MODEL), lambda i: (i, 0))],
        out_shape=[jax.ShapeDtypeStruct((n, D_MODEL), F32), jax.ShapeDtypeStruct((n, D_MODEL), BF16)],
        compiler_params=_cparams(("parallel",)),
        name="post",
    )(x, org, om, z, wb, wo, gpost, gpre)


FF_CHUNK = 256


def _ffn_kernel(*refs, sample, tiles_per_seq, tail_rows):
    if sample:
        (x1_ref, h2_ref, st_ref, wup_ref, wc_ref, bc_ref, wd_ref, g_ref, x2_ref, tail_ref, acc_scr) = refs
    else:
        (x1_ref, h2_ref, halo_ref, wup_ref, wc_ref, bc_ref, wd_ref, g_ref, x2_ref, tail_ref, acc_scr) = refs
    i = pl.program_id(0)
    tm = x1_ref.shape[0]
    h2 = h2_ref[...]
    row = lax.broadcasted_iota(jnp.int32, (tm, 1), 0)
    row8 = lax.broadcasted_iota(jnp.int32, (SUBLANES, 1), 0)
    if not sample:
        halo = jnp.where((i % tiles_per_seq) == 0, jnp.zeros_like(halo_ref[...]), halo_ref[...])

    def shifted(u, uh, sh):
        us = pltpu.roll(u, sh, 0)
        if uh is None:
            return us
        hs = pltpu.roll(uh, sh, 0)
        head = jnp.where(row8 < sh, hs, us[:SUBLANES])
        return jnp.concatenate([head, us[SUBLANES:]], axis=0)

    for c in range(D_FF // FF_CHUNK):
        halves = []
        for half in range(2):
            c0 = half * D_FF + c * FF_CHUNK
            w = wup_ref[:, c0:c0 + FF_CHUNK]
            u = _dot(h2, w)
            if sample:
                inj = ((row % SAMPLE_ROWS) >= SAMPLE_ROWS // 2 - (CONV_W - 1)) & ((row % SAMPLE_ROWS) < SAMPLE_ROWS // 2)
                u = jnp.where(inj, st_ref[:, c0:c0 + FF_CHUNK], u)
                uh = None
            else:
                uh = _dot(halo, w)
            tail_ref[0, :, c0:c0 + FF_CHUNK] = u[tm - tail_rows:]
            conv = bc_ref[:, c0:c0 + FF_CHUNK] + u * wc_ref[CONV_W - 1:CONV_W, c0:c0 + FF_CHUNK]
            for sh in range(1, CONV_W):
                conv = conv + shifted(u, uh, sh) * wc_ref[CONV_W - 1 - sh:CONV_W - sh, c0:c0 + FF_CHUNK]
            halves.append(conv)
        a, b = halves
        gelu = 0.5 * a * (1.0 + jnp.tanh(0.7978845608028654 * (a + 0.044715 * a * a * a)))
        part = _dot((gelu * b).astype(BF16), wd_ref[c * FF_CHUNK:(c + 1) * FF_CHUNK, :])
        if c == 0:
            acc_scr[...] = part
        else:
            acc_scr[...] += part
    x2 = x1_ref[...] + _rms(acc_scr[...], g_ref[...])
    if sample:
        x2 = jnp.where((row % SAMPLE_ROWS) >= SAMPLE_ROWS // 2, x2, 0.0)
    x2_ref[...] = x2


def _ffn(x1, h2, extra, wup, wc, bc, wd, g, *, tm, sample, nseq):
    n = x1.shape[0]
    ntile = n // tm
    tiles_per_seq = ntile // nseq if not sample else 1
    tail_rows = tm if sample else SUBLANES
    ntail = ntile if sample else nseq
    c2 = lambda i: (0, 0)
    if sample:
        extra_spec = pl.BlockSpec((tm, 2 * D_FF), lambda i: (i, 0))
        extra_arg = extra
    else:
        hb = tm // SUBLANES
        extra_spec = pl.BlockSpec((SUBLANES, D_MODEL), lambda i: (jnp.maximum(i * hb - 1, 0), 0))
        extra_arg = h2
    return pl.pallas_call(
        functools.partial(_ffn_kernel, sample=sample, tiles_per_seq=tiles_per_seq, tail_rows=tail_rows),
        grid=(ntile,),
        in_specs=[
            pl.BlockSpec((tm, D_MODEL), lambda i: (i, 0)),
            pl.BlockSpec((tm, D_MODEL), lambda i: (i, 0)),
            extra_spec,
            pl.BlockSpec((D_MODEL, 2 * D_FF), c2, pipeline_mode=pl.Buffered(1)),
            pl.BlockSpec((CONV_W, 2 * D_FF), c2),
            pl.BlockSpec((1, 2 * D_FF), c2),
            pl.BlockSpec((D_FF, D_MODEL), c2, pipeline_mode=pl.Buffered(1)),
            pl.BlockSpec((1, D_MODEL), c2),
        ],
        out_specs=[pl.BlockSpec((tm, D_MODEL), lambda i: (i, 0)),
                   pl.BlockSpec((1, tail_rows, 2 * D_FF), lambda i: (i // tiles_per_seq, 0, 0))],
        out_shape=[jax.ShapeDtypeStruct((n, D_MODEL), F32),
                   jax.ShapeDtypeStruct((ntail, tail_rows, 2 * D_FF), F32)],
        scratch_shapes=[pltpu.VMEM((tm, D_MODEL), F32)],
        compiler_params=_cparams(("arbitrary",)),
        name="ffn",
    )(x1, h2, extra_arg, wup, wc, bc, wd, g)


def _rope_tables(pos):
    half = MOBA_HD // 2
    freqs = jnp.power(ROPE_THETA, -jnp.arange(half, dtype=F32) / half)
    ang = pos.astype(F32)[:, None] * freqs[None, :]
    cos, sin = jnp.cos(ang), jnp.sin(ang)
    return (jnp.concatenate([cos, cos, cos, cos], axis=1),
            jnp.concatenate([-sin, sin, -sin, sin], axis=1))


def kernel(x_prompt, x_sample, cache_k, cache_v, page_table, state_ret, state_gla, state_conv, g_pre_mix, w_in, w_gla_a2, b_gla_a, g_ret_norm, g_gla_norm, w_branch, w_out, g_post_mix, g_pre_ffn, w_up, w_conv, b_conv, w_down, g_post_ffn):
    B, S, D = x_prompt.shape
    Bd, T, _ = x_sample.shape
    depth = w_in.shape[0]
    npages = page_table.shape[1]
    past_len = npages * PAGE_SIZE
    half = SAMPLE_ROWS // 2
    assert D == D_MODEL and T == half and S % 1024 == 0
    assert past_len % MOBA_BLOCK == 0 and npages % PAGES_PER_STEP == 0
    assert past_len // MOBA_BLOCK >= MOBA_TOPK
    n_pool = cache_k.shape[0]
    W = MOBA_HEADS * MOBA_HD

    xp = x_prompt.reshape(B * S, D)
    xs = jnp.pad(x_sample, ((0, 0), (half, 0), (0, 0))).reshape(Bd * SAMPLE_ROWS, D)
    ns = Bd * SAMPLE_ROWS

    cos_p, sin_p = _rope_tables(jnp.arange(S, dtype=jnp.int32))
    pos_s = past_len + jnp.tile(jnp.arange(SAMPLE_ROWS, dtype=jnp.int32) - half, Bd)
    cos_s, sin_s = _rope_tables(pos_s)

    tab_p = _decay_tables(CHUNK, np.ones(CHUNK))
    tab_s = _decay_tables(SAMPLE_ROWS, np.concatenate([np.zeros(half), np.ones(half)]))

    pt_flat = page_table.reshape(-1).astype(jnp.int32)
    kmean = _kmean(cache_k, pt_flat, nseq=Bd, npages=npages)

    def per_head(a):
        return a.reshape(a.shape[0], MOBA_HEADS, MOBA_HD).transpose(1, 0, 2)

    outs = {k: [] for k in ("ret_p", "ret_s", "gla_p", "gla_s", "k_p", "v_p", "k_s", "v_s", "conv_p", "conv_s")}
    for l in range(depth):
        w = w_in[l]
        o_ga = 2 * RET_HEADS * RET_DK + 2 * BRANCH_W + 2 * GLA_HEADS * GLA_DK + 2 * BRANCH_W
        o_m = o_ga + GLA_RANK
        o_gl = o_m + 3 * BRANCH_W
        w_main = jnp.concatenate([w[:, o_gl:], w[:, :o_ga], w[:, o_m:o_gl]], axis=1).astype(BF16)
        w_ga = jnp.pad(w[:, o_ga:o_m], ((0, 0), (0, LANES - GLA_RANK))).astype(BF16)
        w2 = jnp.pad(w_gla_a2[l], ((0, LANES - GLA_RANK), (0, 0))).astype(BF16)
        ba = b_gla_a[l][None, :]
        g_pre = g_pre_mix[l][None, :]
        g_ret = g_ret_norm[l][None, :]
        g_gla = g_gla_norm[l][None, :]
        wb = w_branch[l].astype(BF16)
        wo = w_out[l].astype(BF16)
        gpost = g_post_mix[l][None, :]
        gpre2 = g_pre_ffn[l][None, :]
        wup = w_up[l].astype(BF16)
        wd = w_down[l].astype(BF16)
        wc = w_conv[l]
        bc = b_conv[l][None, :]
        gffn = g_post_ffn[l][None, :]

        zp, lap = _inproj(xp, g_pre, w_main, w_ga, w2, ba, cos_p, sin_p, tm=1024, sample=False)
        orgp, retp, glap = _scan(zp, lap, g_ret, g_gla, tab_p, None, nseq=B, seq_rows=S, T=512,
                                 C=CHUNK, SC=SUBCHUNK, out_dtype=BF16)
        omp = _moba_prompt(zp, nseq=B, seq_rows=S)
        x1p, h2p = _post(xp, orgp, omp, zp, wb, wo, gpost, gpre2, tm=512)
        xp, tailp = _ffn(x1p, h2p, None, wup, wc, bc, wd, gffn, tm=512, sample=False, nseq=B)
        outs["ret_p"].append(retp)
        outs["gla_p"].append(glap)
        outs["k_p"].append(zp[:, Z_MK:Z_MK + W].reshape(B, S, MOBA_HEADS, MOBA_HD))
        outs["v_p"].append(zp[:, Z_MV:Z_MV + W].reshape(B, S, MOBA_HEADS, MOBA_HD))
        outs["conv_p"].append(tailp[:, SUBLANES - (CONV_W - 1):, :])

        zs, las = _inproj(xs, g_pre, w_main, w_ga, w2, ba, cos_s, sin_s, tm=ns, sample=True)
        orgs, rets, glas = _scan(zs, las, g_ret, g_gla, tab_s, (state_ret[l], state_gla[l]), nseq=Bd,
                                 seq_rows=SAMPLE_ROWS, T=SAMPLE_ROWS, C=SAMPLE_ROWS, SC=SAMPLE_ROWS,
                                 out_dtype=F32)
        qh = per_head(zs[:, Z_MQ:Z_MQ + W])
        idx = _moba_topk(qh, kmean, layer=l, nseq=Bd)
        idx_flat = idx[:, half:, :MOBA_HEADS * MOBA_TOPK].reshape(-1)
        oh = _moba_attend(qh, per_head(zs[:, Z_MK:Z_MK + W]), per_head(zs[:, Z_MV:Z_MV + W]), cache_k, cache_v,
                          idx_flat, pt_flat, layer=l, nseq=Bd, npages=npages, ntok=T)
        oms = oh.transpose(1, 0, 2).reshape(ns, W)
        x1s, h2s = _post(xs, orgs, oms, zs, wb, wo, gpost, gpre2, tm=ns)
        st = jnp.pad(state_conv[l], ((0, 0), (half - (CONV_W - 1), half), (0, 0))).reshape(ns, 2 * D_FF)
        xs, tails = _ffn(x1s, h2s, st, wup, wc, bc, wd, gffn, tm=ns, sample=True, nseq=Bd)
        zs3 = zs.reshape(Bd, SAMPLE_ROWS, Z_W)
        outs["ret_s"].append(rets)
        outs["gla_s"].append(glas)
        outs["k_s"].append(zs3[:, half:, Z_MK:Z_MK + W].reshape(Bd, T, MOBA_HEADS, MOBA_HD))
        outs["v_s"].append(zs3[:, half:, Z_MV:Z_MV + W].reshape(Bd, T, MOBA_HEADS, MOBA_HD))
        outs["conv_s"].append(tails.reshape(Bd, SAMPLE_ROWS, 2 * D_FF)[:, SAMPLE_ROWS - (CONV_W - 1):, :])

    yp = xp.reshape(B, S, D)
    ys = xs.reshape(Bd, SAMPLE_ROWS, D)[:, half:, :]
    return (yp, ys,
            jnp.stack(outs["ret_p"], axis=0), jnp.stack(outs["ret_s"], axis=0),
            jnp.stack(outs["gla_p"], axis=0), jnp.stack(outs["gla_s"], axis=0),
            jnp.stack(outs["k_p"], axis=2), jnp.stack(outs["v_p"], axis=2),
            jnp.stack(outs["k_s"], axis=2), jnp.stack(outs["v_s"], axis=2),
            jnp.stack(outs["conv_p"], axis=0), jnp.stack(outs["conv_s"], axis=0))
```

```python
import functools

import numpy as np
import jax
import jax.numpy as jnp
from jax import lax
from jax.experimental import pallas as pl
from jax.experimental.pallas import tpu as pltpu

F32 = jnp.float32
BF16 = jnp.bfloat16

D_MODEL = 1024
BRANCH_W = 512
N_BRANCH = 3
RET_HEADS = 4
RET_DK = 64
RET_DV = 128
GLA_HEADS = 4
GLA_DK = 64
GLA_DV = 128
GLA_RANK = 16
GLA_TAU = 16.0
MOBA_HEADS = 8
MOBA_HD = 64
MOBA_BLOCK = 256
MOBA_TOPK = 3
ROPE_THETA = 10000.0
CHUNK = 64
SUBCHUNK = 16
D_FF = 2816
CONV_W = 3
EPS = 1e-6
PAGE_SIZE = 128

LANES = 128
SUBLANES = 8
VMEM_LIMIT = 56 * 1024 * 1024

Z_GL = 0
Z_RQ = 3072
Z_RK = 3328
Z_RV = 3584
Z_RG = 4096
Z_GQ = 4608
Z_GK = 4864
Z_GV = 5120
Z_GG = 5632
Z_MQ = 6144
Z_MK = 6656
Z_MV = 7168
Z_W = 7680
SEG_W = 512
SEG_RQK = Z_RQ // SEG_W
SEG_GQK = Z_GQ // SEG_W
SEG_MQ = Z_MQ // SEG_W
SEG_MK = Z_MK // SEG_W

SAMPLE_ROWS = 8
NEG_BIG = -1e30
MASK_BIG = 2.0 ** 100
LOG2E = 1.4426950408889634
MOBA_GROUP = 4

NT = (((1,), (1,)), ((), ()))
TN = (((0,), (0,)), ((), ()))


def _dot(a, b, dims=None, precision=None):
    if dims is None:
        return jnp.dot(a, b, preferred_element_type=F32, precision=precision)
    return lax.dot_general(a, b, dims, preferred_element_type=F32, precision=precision)


def _rms(x, g):
    return x * lax.rsqrt(jnp.mean(x * x, axis=-1, keepdims=True) + EPS) * g


def _silu(x):
    return x / (1.0 + jnp.exp(-x))


def _sigmoid(x):
    return 1.0 / (1.0 + jnp.exp(-x))


def _cparams(sem):
    return pltpu.CompilerParams(dimension_semantics=sem, vmem_limit_bytes=VMEM_LIMIT)


def _inproj_kernel(x_ref, g_ref, w_ref, wga_ref, w2_ref, ba_ref, cos_ref, sin_ref,
                   z_ref, la_ref, h_scr, *, sample):
    j = pl.program_id(1)

    @pl.when(j == 0)
    def _():
        hb = _rms(x_ref[...], g_ref[...]).astype(BF16)
        h_scr[...] = hb
        ga = _dot(hb, wga_ref[...])
        u = _dot(ga.astype(BF16), w2_ref[...]) + ba_ref[...]
        la = (jnp.minimum(u, 0.0) - jnp.log(1.0 + jnp.exp(-jnp.abs(u)))) * (1.0 / GLA_TAU)
        if sample:
            row = lax.broadcasted_iota(jnp.int32, (la.shape[0], 1), 0)
            la = jnp.where((row % SAMPLE_ROWS) >= SAMPLE_ROWS // 2, la, 0.0)
        la_ref[...] = la

    acc = _dot(h_scr[...], w_ref[...])
    lane = lax.broadcasted_iota(jnp.int32, (1, LANES), 1)
    first = (lane % MOBA_HD) < (MOBA_HD // 2)

    def rope(a):
        sw = jnp.where(first, pltpu.roll(a, LANES - MOBA_HD // 2, 1), pltpu.roll(a, MOBA_HD // 2, 1))
        return a * cos_ref[...] + sw * sin_ref[...]

    def store_rope(scales):
        for c, s in enumerate(scales):
            r = rope(acc[:, c * LANES:(c + 1) * LANES])
            z_ref[:, c * LANES:(c + 1) * LANES] = r if s == 1.0 else r * s

    ks = RET_DK ** -0.5

    @pl.when(j == SEG_RQK)
    def _():
        store_rope((1.0, 1.0, ks, ks))

    @pl.when(j == SEG_GQK)
    def _():
        z_ref[:, :SEG_W // 2] = acc[:, :SEG_W // 2]
        z_ref[:, SEG_W // 2:] = acc[:, SEG_W // 2:] * (GLA_DK ** -0.5)

    @pl.when(j == SEG_MQ)
    def _():
        store_rope((MOBA_HD ** -0.5,) * 4)

    @pl.when(j == SEG_MK)
    def _():
        store_rope((1.0,) * 4)

    plain = (j != SEG_RQK) & (j != SEG_GQK) & (j != SEG_MQ) & (j != SEG_MK)

    @pl.when(plain)
    def _():
        z_ref[...] = acc


def _inproj(x, g, w_main, w_ga, w2, ba, cos, sin, *, tm, sample):
    n = x.shape[0]
    nseg = Z_W // SEG_W
    ntab = cos.shape[0] // tm
    return pl.pallas_call(
        functools.partial(_inproj_kernel, sample=sample),
        grid=(n // tm, nseg),
        in_specs=[
            pl.BlockSpec((tm, D_MODEL), lambda i, j: (i, 0)),
            pl.BlockSpec((1, D_MODEL), lambda i, j: (0, 0)),
            pl.BlockSpec((D_MODEL, SEG_W), lambda i, j: (0, j)),
            pl.BlockSpec((D_MODEL, LANES), lambda i, j: (0, 0)),
            pl.BlockSpec((LANES, GLA_HEADS * GLA_DK), lambda i, j: (0, 0)),
            pl.BlockSpec((1, GLA_HEADS * GLA_DK), lambda i, j: (0, 0)),
            pl.BlockSpec((tm, LANES), lambda i, j: (i % ntab, 0)),
            pl.BlockSpec((tm, LANES), lambda i, j: (i % ntab, 0)),
        ],
        out_specs=[
            pl.BlockSpec((tm, SEG_W), lambda i, j: (i, j)),
            pl.BlockSpec((tm, GLA_HEADS * GLA_DK), lambda i, j: (i, 0)),
        ],
        out_shape=[jax.ShapeDtypeStruct((n, Z_W), F32),
                   jax.ShapeDtypeStruct((n, GLA_HEADS * GLA_DK), F32)],
        scratch_shapes=[pltpu.VMEM((tm, D_MODEL), BF16)],
        compiler_params=_cparams(("parallel", "arbitrary")),
        name="inproj",
    )(x, g, w_main, w_ga, w2, ba, cos, sin)


def _decay_tables(c, valid):
    log_g = np.log1p(-np.exp2(-5.0 - np.arange(RET_HEADS, dtype=np.float64)))
    b = log_g[:, None] * np.cumsum(np.asarray(valid, np.float64))[None, :]
    diff = b[:, :, None] - b[:, None, :]
    causal = np.arange(c)[:, None] >= np.arange(c)[None, :]
    dmat = np.where(causal[None], np.exp(np.where(causal[None], diff, 0.0)), 0.0)
    qdec = np.broadcast_to(np.exp(b)[:, :, None], (RET_HEADS, c, RET_DK))
    kdec = np.broadcast_to(np.exp(b[:, -1:] - b)[:, :, None], (RET_HEADS, c, RET_DK))
    cdec = tuple(float(v) for v in np.exp(b[:, -1]))
    tri = causal.astype(np.float32)
    return (jnp.asarray(dmat, F32), jnp.asarray(qdec, F32), jnp.asarray(kdec, F32),
            jnp.asarray(tri, F32), cdec)


def _scan_kernel(*refs, C, SC, nchunk, cdec, zero_init):
    if zero_init:
        (zr_ref, zg_ref, la_ref, dmat_ref, qdec_ref, kdec_ref, tri_ref, gr_ref, gg_ref,
         o_ref, sr_ref, sg_ref, Sr, SgT) = refs
    else:
        (zr_ref, zg_ref, la_ref, dmat_ref, qdec_ref, kdec_ref, tri_ref, gr_ref, gg_ref,
         s0r_ref, s0g_ref, o_ref, sr_ref, sg_ref, Sr, SgT) = refs
    t = pl.program_id(1)
    nsub = C // SC
    HK = RET_HEADS * RET_DK

    @pl.when(t == 0)
    def _():
        if zero_init:
            Sr[...] = jnp.zeros_like(Sr)
            SgT[...] = jnp.zeros_like(SgT)
        else:
            Sr[...] = s0r_ref[0]
            for h in range(GLA_HEADS):
                SgT[h] = s0g_ref[0, h].T

    row = lax.broadcasted_iota(jnp.int32, (C, 1), 0)
    rsub = row % SC
    lane_c = lax.broadcasted_iota(jnp.int32, (1, C), 1)

    def bcast_rows(x, jj):
        parts = [jnp.broadcast_to(x[i * SC + jj:i * SC + jj + 1, :], (SC, x.shape[1]))
                 for i in range(nsub)]
        return parts[0] if nsub == 1 else jnp.concatenate(parts, axis=0)

    def chunk(ci, carry):
        r0 = pl.multiple_of(ci * C, C)
        rows = pl.ds(r0, C)

        q = zr_ref[rows, 0:HK]
        k = zr_ref[rows, HK:2 * HK]
        v = zr_ref[rows, 2 * HK:2 * HK + BRANCH_W]
        rg = zr_ref[rows, 2 * HK + BRANCH_W:2 * HK + 2 * BRANCH_W]
        for h in range(RET_HEADS):
            qh = q[:, h * RET_DK:(h + 1) * RET_DK]
            kh = k[:, h * RET_DK:(h + 1) * RET_DK]
            vh = v[:, h * RET_DV:(h + 1) * RET_DV].astype(BF16)
            att = _dot(qh.astype(BF16), kh.astype(BF16), NT) * dmat_ref[h]
            s_old = Sr[h]
            o = _dot(att.astype(BF16), vh) + _dot((qh * qdec_ref[h]).astype(BF16), s_old.astype(BF16))
            Sr[h] = cdec[h] * s_old + _dot((kh * kdec_ref[h]).T.astype(BF16), vh)
            cen = o - jnp.mean(o, axis=-1, keepdims=True)
            y = cen * lax.rsqrt(jnp.mean(cen * cen, axis=-1, keepdims=True) + EPS)
            y = y * gr_ref[:, h * RET_DV:(h + 1) * RET_DV] * _silu(rg[:, h * RET_DV:(h + 1) * RET_DV])
            o_ref[rows, h * RET_DV:(h + 1) * RET_DV] = y.astype(o_ref.dtype)

        q = zg_ref[rows, 0:HK]
        k = zg_ref[rows, HK:2 * HK]
        v = zg_ref[rows, 2 * HK:2 * HK + BRANCH_W]
        gg = zg_ref[rows, 2 * HK + BRANCH_W:2 * HK + 2 * BRANCH_W]
        b = _dot(tri_ref[...], la_ref[rows, :], precision=lax.Precision.HIGHEST)
        blast = b[C - 1:C, :]
        qe = q * jnp.exp(b)
        kd = k * jnp.exp(blast - b)
        eblast = jnp.exp(blast)

        att = [jnp.zeros((C, C), F32) for _ in range(GLA_HEADS)]
        for jj in range(SC):
            kb = bcast_rows(k, jj)
            bb = bcast_rows(b, jj)
            w = q * kb * jnp.exp(jnp.where(rsub >= jj, b - bb, NEG_BIG))
            hit = lane_c == (row - rsub + jj)
            for h in range(GLA_HEADS):
                col = jnp.sum(w[:, h * GLA_DK:(h + 1) * GLA_DK], axis=-1, keepdims=True)
                att[h] = jnp.where(hit, col, att[h])
        for i in range(1, nsub):
            bi = b[i * SC - 1:i * SC, :]
            in_i = (row >= i * SC) & (row < (i + 1) * SC)
            qi = (q * jnp.exp(jnp.where(in_i, b - bi, NEG_BIG))).astype(BF16)
            ki = (k * jnp.exp(jnp.where(row < i * SC, bi - b, NEG_BIG))).astype(BF16)
            for h in range(GLA_HEADS):
                att[h] = att[h] + _dot(qi[:, h * GLA_DK:(h + 1) * GLA_DK],
                                       ki[:, h * GLA_DK:(h + 1) * GLA_DK], NT)
        for h in range(GLA_HEADS):
            vh = v[:, h * GLA_DV:(h + 1) * GLA_DV]
            vhb = vh.astype(BF16)
            st_old = SgT[h]
            o = _dot(att[h].astype(BF16), vhb) + _dot(
                qe[:, h * GLA_DK:(h + 1) * GLA_DK].astype(BF16), st_old.astype(BF16), NT)
            SgT[h] = st_old * eblast[:, h * GLA_DK:(h + 1) * GLA_DK] + _dot(
                vh.T.astype(BF16), kd[:, h * GLA_DK:(h + 1) * GLA_DK].astype(BF16))
            y = _rms(o, gg_ref[:, h * GLA_DV:(h + 1) * GLA_DV]) * _silu(gg[:, h * GLA_DV:(h + 1) * GLA_DV])
            o_ref[rows, BRANCH_W + h * GLA_DV:BRANCH_W + (h + 1) * GLA_DV] = y.astype(o_ref.dtype)
        return carry

    lax.fori_loop(0, nchunk, chunk, 0)

    @pl.when(t == pl.num_programs(1) - 1)
    def _():
        sr_ref[0] = Sr[...]
        for h in range(GLA_HEADS):
            sg_ref[0, h] = SgT[h].T


def _scan(z, la, g_ret, g_gla, tables, s0, *, nseq, seq_rows, T, C, SC, out_dtype):
    dmat, qdec, kdec, tri, cdec = tables
    nT = seq_rows // T
    zero_init = s0 is None
    zw = 2 * RET_HEADS * RET_DK + 2 * BRANCH_W
    const3 = lambda s, t: (0, 0, 0)
    const2 = lambda s, t: (0, 0)
    in_specs = [
        pl.BlockSpec((T, zw), lambda s, t: (s * nT + t, Z_RQ // zw)),
        pl.BlockSpec((T, zw), lambda s, t: (s * nT + t, Z_GQ // zw)),
        pl.BlockSpec((T, GLA_HEADS * GLA_DK), lambda s, t: (s * nT + t, 0)),
        pl.BlockSpec(dmat.shape, const3),
        pl.BlockSpec(qdec.shape, const3),
        pl.BlockSpec(kdec.shape, const3),
        pl.BlockSpec(tri.shape, const2),
        pl.BlockSpec((1, BRANCH_W), const2),
        pl.BlockSpec((1, BRANCH_W), const2),
    ]
    args = [z, z, la, dmat, qdec, kdec, tri, g_ret, g_gla]
    st_spec = pl.BlockSpec((1, RET_HEADS, RET_DK, RET_DV), lambda s, t: (s, 0, 0, 0))
    if not zero_init:
        in_specs += [st_spec, st_spec]
        args += [s0[0], s0[1]]
    st_shape = jax.ShapeDtypeStruct((nseq, RET_HEADS, RET_DK, RET_DV), F32)
    return pl.pallas_call(
        functools.partial(_scan_kernel, C=C, SC=SC, nchunk=T // C, cdec=cdec, zero_init=zero_init),
        grid=(nseq, nT),
        in_specs=in_specs,
        out_specs=[pl.BlockSpec((T, 2 * BRANCH_W), lambda s, t: (s * nT + t, 0)), st_spec, st_spec],
        out_shape=[jax.ShapeDtypeStruct((nseq * seq_rows, 2 * BRANCH_W), out_dtype), st_shape, st_shape],
        scratch_shapes=[pltpu.VMEM((RET_HEADS, RET_DK, RET_DV), F32),
                        pltpu.VMEM((GLA_HEADS, GLA_DV, GLA_DK), F32)],
        compiler_params=_cparams(("parallel", "arbitrary")),
        name="scan",
    )(*args)


def _top3_select(g, lane):
    sel = jnp.zeros(g.shape, jnp.bool_)
    for _ in range(MOBA_TOPK):
        m = jnp.max(g, axis=-1, keepdims=True)
        idx = jnp.min(jnp.where(g == m, lane, 4 * LANES), axis=-1, keepdims=True)
        pick = (lane == idx) & (m > -jnp.inf)
        sel = sel | pick
        g = jnp.where(pick, -jnp.inf, g)
    return sel


def _moba_prompt_kernel(q_ref, k_ref, v_ref, o_ref, km_scr, ka_scr, vb_scr, *, nblk):
    t = pl.program_id(2)
    BLK = MOBA_BLOCK
    G = MOBA_GROUP
    lane = lax.broadcasted_iota(jnp.int32, (1, LANES), 1)
    in_e = [(lane >= e * MOBA_HD) & (lane < (e + 1) * MOBA_HD) for e in range(2)]
    off = [MOBA_HD, 0]

    @pl.when(t == 0)
    def _():
        km_scr[...] = jnp.zeros_like(km_scr)
        for blk in range(nblk):
            m = jnp.mean(k_ref[blk * BLK:(blk + 1) * BLK, :], axis=0, keepdims=True)
            km_scr[0, MOBA_HD + blk:MOBA_HD + blk + 1, :] = m
            km_scr[1, blk:blk + 1, :] = m

        def prep(blk, c):
            r0 = pl.multiple_of(blk * BLK, BLK)
            kj = k_ref[pl.ds(r0, BLK), :]
            for e in range(2):
                ka_scr[e, pl.ds(r0, BLK), :] = jnp.where(
                    in_e[e], kj, jnp.where(lane == off[e] + blk, 1.0, 0.0)).astype(BF16)
            vb_scr[pl.ds(r0, BLK), :] = v_ref[pl.ds(r0, BLK), :].astype(BF16)
            return c
        lax.fori_loop(0, nblk, prep, 0)

    q = q_ref[...]
    q2 = q * LOG2E
    t0 = pl.multiple_of(t * BLK, BLK)
    vt = vb_scr[pl.ds(t0, BLK), :]
    rowi = lax.broadcasted_iota(jnp.int32, (BLK, 1), 0)
    coli = lax.broadcasted_iota(jnp.int32, (1, BLK), 1)
    qaug, init = [], []
    for e in range(2):
        qm = jnp.where(in_e[e], q, 0.0)
        gate = _dot(qm, km_scr[e], NT, precision=lax.Precision.HIGHEST)
        blkid = lane - off[e]
        g = jnp.where((blkid >= 0) & (blkid < t), gate, -jnp.inf)
        sel = _top3_select(g, lane)
        in_blk = (blkid >= 0) & (blkid < nblk)
        qaug.append(jnp.where(in_e[e], q2, jnp.where(in_blk & jnp.logical_not(sel), -MASK_BIG, 0.0)).astype(BF16))
        s = _dot(jnp.where(in_e[e], q2, 0.0).astype(BF16), ka_scr[e, pl.ds(t0, BLK), :], NT)
        s = jnp.where(coli <= rowi, s, NEG_BIG)
        m0 = jnp.max(s, axis=-1, keepdims=True)
        p = jnp.exp2(s - m0)
        init += [m0, jnp.sum(p, axis=-1, keepdims=True), _dot(p.astype(BF16), vt)]

    def body(g, carry):
        r0 = pl.multiple_of(g * (G * BLK), G * BLK)
        vg = vb_scr[pl.ds(r0, G * BLK), :]
        out = []
        for e in range(2):
            m, l, acc = carry[3 * e:3 * e + 3]
            s = _dot(qaug[e], ka_scr[e, pl.ds(r0, G * BLK), :], NT)
            mn = jnp.maximum(m, jnp.max(s, axis=-1, keepdims=True))
            a = jnp.exp2(m - mn)
            p = jnp.exp2(s - mn)
            out += [mn, a * l + jnp.sum(p, axis=-1, keepdims=True), a * acc + _dot(p.astype(BF16), vg)]
        return tuple(out)

    res = lax.fori_loop(0, (t + G - 1) // G, body, tuple(init))
    o_ref[...] = jnp.where(lane < MOBA_HD, res[2] / res[1], res[5] / res[4]).astype(o_ref.dtype)


def _moba_prompt(z, *, nseq, seq_rows):
    nblk = seq_rows // MOBA_BLOCK
    npair = MOBA_HEADS // 2
    assert nblk % MOBA_GROUP == 0 and nblk <= MOBA_HD
    return pl.pallas_call(
        functools.partial(_moba_prompt_kernel, nblk=nblk),
        grid=(nseq, npair, nblk),
        in_specs=[
            pl.BlockSpec((MOBA_BLOCK, LANES), lambda b, hp, t: (b * nblk + t, Z_MQ // LANES + hp)),
            pl.BlockSpec((seq_rows, LANES), lambda b, hp, t: (b, Z_MK // LANES + hp)),
            pl.BlockSpec((seq_rows, LANES), lambda b, hp, t: (b, Z_MV // LANES + hp)),
        ],
        out_specs=pl.BlockSpec((MOBA_BLOCK, LANES), lambda b, hp, t: (b * nblk + t, hp)),
        out_shape=jax.ShapeDtypeStruct((nseq * seq_rows, BRANCH_W), BF16),
        scratch_shapes=[pltpu.VMEM((2, LANES, LANES), F32),
                        pltpu.VMEM((2, seq_rows, LANES), BF16),
                        pltpu.VMEM((seq_rows, LANES), BF16)],
        compiler_params=_cparams(("parallel", "parallel", "arbitrary")),
        name="moba_prompt",
    )(z, z, z)


PAGES_PER_STEP = 16
PAGES_PER_BLOCK = MOBA_BLOCK // PAGE_SIZE


def _kmean_kernel(pt_ref, *refs):
    page_refs = refs[:PAGES_PER_STEP]
    km_ref = refs[PAGES_PER_STEP]
    for blk in range(PAGES_PER_STEP // PAGES_PER_BLOCK):
        tot = page_refs[blk * PAGES_PER_BLOCK][...]
        for p in range(1, PAGES_PER_BLOCK):
            tot = tot + page_refs[blk * PAGES_PER_BLOCK + p][...]
        km_ref[0, 0, :, :, :, blk:blk + 1] = jnp.sum(tot, axis=-1, keepdims=True) * (1.0 / MOBA_BLOCK)


def _kmean(cache_kt, pt_flat, *, nseq, npages):
    depth = cache_kt.shape[1]
    nfull = npages // PAGES_PER_BLOCK
    bps = PAGES_PER_STEP // PAGES_PER_BLOCK
    nstep = npages // PAGES_PER_STEP

    def page_spec(i):
        return pl.BlockSpec((None, depth, MOBA_HEADS, MOBA_HD, PAGE_SIZE),
                            lambda b, g, pt, i=i: (pt[b * npages + g * PAGES_PER_STEP + i], 0, 0, 0, 0))

    grid_spec = pltpu.PrefetchScalarGridSpec(
        num_scalar_prefetch=1,
        grid=(nseq, nstep),
        in_specs=[page_spec(i) for i in range(PAGES_PER_STEP)],
        out_specs=pl.BlockSpec((1, 1, depth, MOBA_HEADS, MOBA_HD, bps), lambda b, g, pt: (b, g, 0, 0, 0, 0)),
    )
    km = pl.pallas_call(
        _kmean_kernel,
        grid_spec=grid_spec,
        out_shape=jax.ShapeDtypeStruct((nseq, nstep, depth, MOBA_HEADS, MOBA_HD, bps), F32),
        compiler_params=_cparams(("parallel", "arbitrary")),
        name="kmean",
    )(pt_flat, *([cache_kt] * PAGES_PER_STEP))
    return km.transpose(0, 2, 3, 4, 1, 5).reshape(nseq, depth, MOBA_HEADS, MOBA_HD, nfull)


def _moba_topk_kernel(q_ref, km_ref, idx_ref):
    nfull = km_ref.shape[2]
    lane = lax.broadcasted_iota(jnp.int32, (1, LANES), 1)
    lane_b = lax.broadcasted_iota(jnp.int32, (1, nfull), 1)
    out = jnp.zeros((SAMPLE_ROWS, LANES), jnp.int32)
    for h in range(MOBA_HEADS):
        gsc = _dot(q_ref[h], km_ref[h], precision=lax.Precision.HIGHEST)
        for r in range(MOBA_TOPK):
            m = jnp.max(gsc, axis=-1, keepdims=True)
            idx = jnp.min(jnp.where(gsc == m, lane_b, nfull), axis=-1, keepdims=True)
            out = jnp.where(lane == h * MOBA_TOPK + r, idx, out)
            gsc = jnp.where(lane_b == idx, -jnp.inf, gsc)
    idx_ref[0] = out


def _moba_topk(qh, kmean, *, layer, nseq):
    nfull = kmean.shape[4]
    return pl.pallas_call(
        _moba_topk_kernel,
        grid=(nseq,),
        in_specs=[
            pl.BlockSpec((MOBA_HEADS, SAMPLE_ROWS, MOBA_HD), lambda b: (0, b, 0)),
            pl.BlockSpec((None, None, MOBA_HEADS, MOBA_HD, nfull), lambda b: (b, layer, 0, 0, 0)),
        ],
        out_specs=pl.BlockSpec((1, SAMPLE_ROWS, LANES), lambda b: (b, 0, 0)),
        out_shape=jax.ShapeDtypeStruct((nseq, SAMPLE_ROWS, LANES), jnp.int32),
        compiler_params=_cparams(("parallel",)),
        name="moba_topk",
    )(qh, kmean)


def _moba_attend_kernel(idx_ref, pt_ref, q_ref, kn_ref, vn_ref, ck_hbm, cv_hbm, o_ref, kbuf, vbuf, sem,
                        *, layer, npages, ntok):
    b = pl.program_id(0)
    h = pl.program_id(1)
    nh = pl.num_programs(1)
    step = b * nh + h
    nstep = pl.num_programs(0) * nh
    slot = step % 2

    def copies(bb, hh, sl):
        out = []
        for t in range(ntok):
            for r in range(MOBA_TOPK):
                blk = idx_ref[((bb * ntok + t) * MOBA_HEADS + hh) * MOBA_TOPK + r]
                for p in range(PAGES_PER_BLOCK):
                    page = pt_ref[bb * npages + blk * PAGES_PER_BLOCK + p]
                    i = (t * MOBA_TOPK + r) * PAGES_PER_BLOCK + p
                    out.append(pltpu.make_async_copy(ck_hbm.at[page, layer, hh], kbuf.at[sl, i], sem.at[0, sl]))
                    out.append(pltpu.make_async_copy(cv_hbm.at[page, layer, hh], vbuf.at[sl, i], sem.at[1, sl]))
        return out

    @pl.when(step == 0)
    def _():
        for c in copies(b, h, slot):
            c.start()

    @pl.when(step + 1 < nstep)
    def _():
        nxt = step + 1
        for c in copies(nxt // nh, nxt % nh, 1 - slot):
            c.start()

    for c in copies(b, h, slot):
        c.wait()

    half = SAMPLE_ROWS // 2
    rowi = lax.broadcasted_iota(jnp.int32, (SAMPLE_ROWS, 1), 0)
    coln = lax.broadcasted_iota(jnp.int32, (1, SAMPLE_ROWS), 1)
    qb = q_ref[...].astype(BF16)
    vn = vn_ref[...].astype(BF16)
    s_new_all = _dot(qb, kn_ref[...].astype(BF16), NT)
    out = jnp.zeros((SAMPLE_ROWS, MOBA_HD), F32)
    for t in range(ntok):
        qrow = half + t
        s_new = jnp.where((coln >= half) & (coln <= qrow), s_new_all, NEG_BIG)
        ss, vs = [], []
        for i in range(t * MOBA_TOPK * PAGES_PER_BLOCK, (t + 1) * MOBA_TOPK * PAGES_PER_BLOCK):
            ss.append(_dot(qb, kbuf[slot, i].astype(BF16)))
            vs.append(vbuf[slot, i].astype(BF16))
        m = jnp.max(s_new, axis=-1, keepdims=True)
        for s in ss:
            m = jnp.maximum(m, jnp.max(s, axis=-1, keepdims=True))
        p_new = jnp.exp(s_new - m)
        l = jnp.sum(p_new, axis=-1, keepdims=True)
        o = _dot(p_new.astype(BF16), vn)
        for s, vb in zip(ss, vs):
            p = jnp.exp(s - m)
            l = l + jnp.sum(p, axis=-1, keepdims=True)
            o = o + _dot(p.astype(BF16), vb, NT)
        out = jnp.where(rowi == qrow, o / l, out)
    o_ref[...] = out


def _moba_attend(qh, knh, vnh, cache_k, cache_v, idx_flat, pt_flat, *, layer, nseq, npages, ntok):
    nbuf = ntok * MOBA_TOPK * PAGES_PER_BLOCK
    head_spec = pl.BlockSpec((None, SAMPLE_ROWS, MOBA_HD), lambda b, h, idx, pt: (h, b, 0))
    grid_spec = pltpu.PrefetchScalarGridSpec(
        num_scalar_prefetch=2,
        grid=(nseq, MOBA_HEADS),
        in_specs=[head_spec, head_spec, head_spec,
                  pl.BlockSpec(memory_space=pl.ANY), pl.BlockSpec(memory_space=pl.ANY)],
        out_specs=head_spec,
        scratch_shapes=[pltpu.VMEM((2, nbuf, MOBA_HD, PAGE_SIZE), F32),
                        pltpu.VMEM((2, nbuf, MOBA_HD, PAGE_SIZE), F32),
                        pltpu.SemaphoreType.DMA((2, 2))],
    )
    return pl.pallas_call(
        functools.partial(_moba_attend_kernel, layer=layer, npages=npages, ntok=ntok),
        grid_spec=grid_spec,
        out_shape=jax.ShapeDtypeStruct((MOBA_HEADS, nseq * SAMPLE_ROWS, MOBA_HD), F32),
        compiler_params=_cparams(("arbitrary", "arbitrary")),
        name="moba_attend",
    )(idx_flat, pt_flat, qh, knh, vnh, cache_k, cache_v)


def _post_kernel(x_ref, org_ref, om_ref, gl_ref, wb_ref, wo_ref, gpost_ref, gpre_ref, x1_ref, h2_ref):
    merged = None
    for n in range(N_BRANCH):
        if n < 2:
            br = org_ref[:, n * BRANCH_W:(n + 1) * BRANCH_W]
        else:
            br = om_ref[...]
        proj = _dot(br.astype(BF16), wb_ref[n])
        term = _sigmoid(gl_ref[:, n * D_MODEL:(n + 1) * D_MODEL]) * proj
        merged = term if merged is None else merged + term
    y = _dot(merged.astype(BF16), wo_ref[...])
    x1 = x_ref[...] + _rms(y, gpost_ref[...])
    x1_ref[...] = x1
    h2_ref[...] = _rms(x1, gpre_ref[...]).astype(BF16)


def _post(x, org, om, z, wb, wo, gpost, gpre, *, tm):
    n = x.shape[0]
    c2 = lambda i: (0, 0)
    return pl.pallas_call(
        _post_kernel,
        grid=(n // tm,),
        in_specs=[
            pl.BlockSpec((tm, D_MODEL), lambda i: (i, 0)),
            pl.BlockSpec((tm, 2 * BRANCH_W), lambda i: (i, 0)),
            pl.BlockSpec((tm, BRANCH_W), lambda i: (i, 0)),
            pl.BlockSpec((tm, N_BRANCH * D_MODEL), lambda i: (i, 0)),
            pl.BlockSpec((N_BRANCH, BRANCH_W, D_MODEL), lambda i: (0, 0, 0)),
            pl.BlockSpec((D_MODEL, D_MODEL), c2),
            pl.BlockSpec((1, D_MODEL), c2),
            pl.BlockSpec((1, D_MODEL), c2),
        ],
        out_specs=[pl.BlockSpec((tm, D_MODEL), lambda i: (i, 0)),
                   pl.BlockSpec((tm, D_MODEL), lambda i: (i, 0))],
        out_shape=[jax.ShapeDtypeStruct((n, D_MODEL), F32), jax.ShapeDtypeStruct((n, D_MODEL), BF16)],
        compiler_params=_cparams(("parallel",)),
        name="post",
    )(x, org, om, z, wb, wo, gpost, gpre)


FF_CHUNK = 256


def _ffn_kernel(*refs, sample, tiles_per_seq, tail_rows):
    if sample:
        (x1_ref, h2_ref, st_ref, wup_ref, wc_ref, bc_ref, wd_ref, g_ref, x2_ref, tail_ref, acc_scr) = refs
    else:
        (x1_ref, h2_ref, halo_ref, wup_ref, wc_ref, bc_ref, wd_ref, g_ref, x2_ref, tail_ref, acc_scr) = refs
    i = pl.program_id(0)
    tm = x1_ref.shape[0]
    h2 = h2_ref[...]
    row = lax.broadcasted_iota(jnp.int32, (tm, 1), 0)
    row8 = lax.broadcasted_iota(jnp.int32, (SUBLANES, 1), 0)
    if not sample:
        halo = jnp.where((i % tiles_per_seq) == 0, jnp.zeros_like(halo_ref[...]), halo_ref[...])

    def shifted(u, uh, sh):
        us = pltpu.roll(u, sh, 0)
        if uh is None:
            return us
        hs = pltpu.roll(uh, sh, 0)
        head = jnp.where(row8 < sh, hs, us[:SUBLANES])
        return jnp.concatenate([head, us[SUBLANES:]], axis=0)

    for c in range(D_FF // FF_CHUNK):
        halves = []
        for half in range(2):
            c0 = half * D_FF + c * FF_CHUNK
            w = wup_ref[:, c0:c0 + FF_CHUNK]
            u = _dot(h2, w)
            if sample:
                inj = ((row % SAMPLE_ROWS) >= SAMPLE_ROWS // 2 - (CONV_W - 1)) & ((row % SAMPLE_ROWS) < SAMPLE_ROWS // 2)
                u = jnp.where(inj, st_ref[:, c0:c0 + FF_CHUNK], u)
                uh = None
            else:
                uh = _dot(halo, w)
            tail_ref[0, :, c0:c0 + FF_CHUNK] = u[tm - tail_rows:]
            conv = bc_ref[:, c0:c0 + FF_CHUNK] + u * wc_ref[CONV_W - 1:CONV_W, c0:c0 + FF_CHUNK]
            for sh in range(1, CONV_W):
                conv = conv + shifted(u, uh, sh) * wc_ref[CONV_W - 1 - sh:CONV_W - sh, c0:c0 + FF_CHUNK]
            halves.append(conv)
        a, b = halves
        gelu = 0.5 * a * (1.0 + jnp.tanh(0.7978845608028654 * (a + 0.044715 * a * a * a)))
        part = _dot((gelu * b).astype(BF16), wd_ref[c * FF_CHUNK:(c + 1) * FF_CHUNK, :])
        if c == 0:
            acc_scr[...] = part
        else:
            acc_scr[...] += part
    x2 = x1_ref[...] + _rms(acc_scr[...], g_ref[...])
    if sample:
        x2 = jnp.where((row % SAMPLE_ROWS) >= SAMPLE_ROWS // 2, x2, 0.0)
    x2_ref[...] = x2


def _ffn(x1, h2, extra, wup, wc, bc, wd, g, *, tm, sample, nseq):
    n = x1.shape[0]
    ntile = n // tm
    tiles_per_seq = ntile // nseq if not sample else 1
    tail_rows = tm if sample else SUBLANES
    ntail = ntile if sample else nseq
    c2 = lambda i: (0, 0)
    if sample:
        extra_spec = pl.BlockSpec((tm, 2 * D_FF), lambda i: (i, 0))
        extra_arg = extra
    else:
        hb = tm // SUBLANES
        extra_spec = pl.BlockSpec((SUBLANES, D_MODEL), lambda i: (jnp.maximum(i * hb - 1, 0), 0))
        extra_arg = h2
    return pl.pallas_call(
        functools.partial(_ffn_kernel, sample=sample, tiles_per_seq=tiles_per_seq, tail_rows=tail_rows),
        grid=(ntile,),
        in_specs=[
            pl.BlockSpec((tm, D_MODEL), lambda i: (i, 0)),
            pl.BlockSpec((tm, D_MODEL), lambda i: (i, 0)),
            extra_spec,
            pl.BlockSpec((D_MODEL, 2 * D_FF), c2, pipeline_mode=pl.Buffered(1)),
            pl.BlockSpec((CONV_W, 2 * D_FF), c2),
            pl.BlockSpec((1, 2 * D_FF), c2),
            pl.BlockSpec((D_FF, D_MODEL), c2, pipeline_mode=pl.Buffered(1)),
            pl.BlockSpec((1, D_MODEL), c2),
        ],
        out_specs=[pl.BlockSpec((tm, D_MODEL), lambda i: (i, 0)),
                   pl.BlockSpec((1, tail_rows, 2 * D_FF), lambda i: (i // tiles_per_seq, 0, 0))],
        out_shape=[jax.ShapeDtypeStruct((n, D_MODEL), F32),
                   jax.ShapeDtypeStruct((ntail, tail_rows, 2 * D_FF), F32)],
        scratch_shapes=[pltpu.VMEM((tm, D_MODEL), F32)],
        compiler_params=_cparams(("arbitrary",)),
        name="ffn",
    )(x1, h2, extra_arg, wup, wc, bc, wd, g)


def _rope_tables(pos):
    half = MOBA_HD // 2
    freqs = jnp.power(ROPE_THETA, -jnp.arange(half, dtype=F32) / half)
    ang = pos.astype(F32)[:, None] * freqs[None, :]
    cos, sin = jnp.cos(ang), jnp.sin(ang)
    return (jnp.concatenate([cos, cos, cos, cos], axis=1),
            jnp.concatenate([-sin, sin, -sin, sin], axis=1))


def kernel(x_prompt, x_sample, cache_k, cache_v, page_table, state_ret, state_gla, state_conv, g_pre_mix, w_in, w_gla_a2, b_gla_a, g_ret_norm, g_gla_norm, w_branch, w_out, g_post_mix, g_pre_ffn, w_up, w_conv, b_conv, w_down, g_post_ffn):
    B, S, D = x_prompt.shape
    Bd, T, _ = x_sample.shape
    depth = w_in.shape[0]
    npages = page_table.shape[1]
    past_len = npages * PAGE_SIZE
    half = SAMPLE_ROWS // 2
    assert D == D_MODEL and T == half and S % 1024 == 0
    assert past_len % MOBA_BLOCK == 0 and npages % PAGES_PER_STEP == 0
    assert past_len // MOBA_BLOCK >= MOBA_TOPK
    n_pool = cache_k.shape[0]
    W = MOBA_HEADS * MOBA_HD

    xp = x_prompt.reshape(B * S, D)
    xs = jnp.pad(x_sample, ((0, 0), (half, 0), (0, 0))).reshape(Bd * SAMPLE_ROWS, D)
    ns = Bd * SAMPLE_ROWS

    cos_p, sin_p = _rope_tables(jnp.arange(S, dtype=jnp.int32))
    pos_s = past_len + jnp.tile(jnp.arange(SAMPLE_ROWS, dtype=jnp.int32) - half, Bd)
    cos_s, sin_s = _rope_tables(pos_s)

    tab_p = _decay_tables(CHUNK, np.ones(CHUNK))
    tab_s = _decay_tables(SAMPLE_ROWS, np.concatenate([np.zeros(half), np.ones(half)]))

    pt_flat = page_table.reshape(-1).astype(jnp.int32)
    cache_kt = cache_k.transpose(0, 2, 3, 4, 1)
    cache_vt = cache_v.transpose(0, 2, 3, 4, 1)
    kmean = _kmean(cache_kt, pt_flat, nseq=Bd, npages=npages)

    def per_head(a):
        return a.reshape(a.shape[0], MOBA_HEADS, MOBA_HD).transpose(1, 0, 2)

    outs = {k: [] for k in ("ret_p", "ret_s", "gla_p", "gla_s", "k_p", "v_p", "k_s", "v_s", "conv_p", "conv_s")}
    for l in range(depth):
        w = w_in[l]
        o_ga = 2 * RET_HEADS * RET_DK + 2 * BRANCH_W + 2 * GLA_HEADS * GLA_DK + 2 * BRANCH_W
        o_m = o_ga + GLA_RANK
        o_gl = o_m + 3 * BRANCH_W
        w_main = jnp.concatenate([w[:, o_gl:], w[:, :o_ga], w[:, o_m:o_gl]], axis=1).astype(BF16)
        w_ga = jnp.pad(w[:, o_ga:o_m], ((0, 0), (0, LANES - GLA_RANK))).astype(BF16)
        w2 = jnp.pad(w_gla_a2[l], ((0, LANES - GLA_RANK), (0, 0))).astype(BF16)
        ba = b_gla_a[l][None, :]
        g_pre = g_pre_mix[l][None, :]
        g_ret = g_ret_norm[l][None, :]
        g_gla = g_gla_norm[l][None, :]
        wb = w_branch[l].astype(BF16)
        wo = w_out[l].astype(BF16)
        gpost = g_post_mix[l][None, :]
        gpre2 = g_pre_ffn[l][None, :]
        wup = w_up[l].astype(BF16)
        wd = w_down[l].astype(BF16)
        wc = w_conv[l]
        bc = b_conv[l][None, :]
        gffn = g_post_ffn[l][None, :]

        zp, lap = _inproj(xp, g_pre, w_main, w_ga, w2, ba, cos_p, sin_p, tm=1024, sample=False)
        orgp, retp, glap = _scan(zp, lap, g_ret, g_gla, tab_p, None, nseq=B, seq_rows=S, T=512,
                                 C=CHUNK, SC=SUBCHUNK, out_dtype=BF16)
        omp = _moba_prompt(zp, nseq=B, seq_rows=S)
        x1p, h2p = _post(xp, orgp, omp, zp, wb, wo, gpost, gpre2, tm=512)
        xp, tailp = _ffn(x1p, h2p, None, wup, wc, bc, wd, gffn, tm=512, sample=False, nseq=B)
        outs["ret_p"].append(retp)
        outs["gla_p"].append(glap)
        outs["k_p"].append(zp[:, Z_MK:Z_MK + W].reshape(B, S, MOBA_HEADS, MOBA_HD))
        outs["v_p"].append(zp[:, Z_MV:Z_MV + W].reshape(B, S, MOBA_HEADS, MOBA_HD))
        outs["conv_p"].append(tailp[:, SUBLANES - (CONV_W - 1):, :])

        zs, las = _inproj(xs, g_pre, w_main, w_ga, w2, ba, cos_s, sin_s, tm=ns, sample=True)
        orgs, rets, glas = _scan(zs, las, g_ret, g_gla, tab_s, (state_ret[l], state_gla[l]), nseq=Bd,
                                 seq_rows=SAMPLE_ROWS, T=SAMPLE_ROWS, C=SAMPLE_ROWS, SC=SAMPLE_ROWS,
                                 out_dtype=F32)
        qh = per_head(zs[:, Z_MQ:Z_MQ + W])
        idx = _moba_topk(qh, kmean, layer=l, nseq=Bd)
        idx_flat = idx[:, half:, :MOBA_HEADS * MOBA_TOPK].reshape(-1)
        oh = _moba_attend(qh, per_head(zs[:, Z_MK:Z_MK + W]), per_head(zs[:, Z_MV:Z_MV + W]), cache_kt, cache_vt,
                          idx_flat, pt_flat, layer=l, nseq=Bd, npages=npages, ntok=T)
        oms = oh.transpose(1, 0, 2).reshape(ns, W)
        x1s, h2s = _post(xs, orgs, oms, zs, wb, wo, gpost, gpre2, tm=ns)
        st = jnp.pad(state_conv[l], ((0, 0), (half - (CONV_W - 1), half), (0, 0))).reshape(ns, 2 * D_FF)
        xs, tails = _ffn(x1s, h2s, st, wup, wc, bc, wd, gffn, tm=ns, sample=True, nseq=Bd)
        zs3 = zs.reshape(Bd, SAMPLE_ROWS, Z_W)
        outs["ret_s"].append(rets)
        outs["gla_s"].append(glas)
        outs["k_s"].append(zs3[:, half:, Z_MK:Z_MK + W].reshape(Bd, T, MOBA_HEADS, MOBA_HD))
        outs["v_s"].append(zs3[:, half:, Z_MV:Z_MV + W].reshape(Bd, T, MOBA_HEADS, MOBA_HD))
        outs["conv_s"].append(tails.reshape(Bd, SAMPLE_ROWS, 2 * D_FF)[:, SAMPLE_ROWS - (CONV_W - 1):, :])

    yp = xp.reshape(B, S, D)
    ys = xs.reshape(Bd, SAMPLE_ROWS, D)[:, half:, :]
    return (yp, ys,
            jnp.stack(outs["ret_p"], axis=0), jnp.stack(outs["ret_s"], axis=0),
            jnp.stack(outs["gla_p"], axis=0), jnp.stack(outs["gla_s"], axis=0),
            jnp.stack(outs["k_p"], axis=2), jnp.stack(outs["v_p"], axis=2),
            jnp.stack(outs["k_s"], axis=2), jnp.stack(outs["v_s"], axis=2),
            jnp.stack(outs["conv_p"], axis=0), jnp.stack(outs["conv_s"], axis=0))
```

```python
import functools

import numpy as np
import jax
import jax.numpy as jnp
from jax import lax
from jax.experimental import pallas as pl
from jax.experimental.pallas import tpu as pltpu

F32 = jnp.float32
BF16 = jnp.bfloat16

D_MODEL = 1024
BRANCH_W = 512
N_BRANCH = 3
RET_HEADS = 4
RET_DK = 64
RET_DV = 128
GLA_HEADS = 4
GLA_DK = 64
GLA_DV = 128
GLA_RANK = 16
GLA_TAU = 16.0
MOBA_HEADS = 8
MOBA_HD = 64
MOBA_BLOCK = 256
MOBA_TOPK = 3
ROPE_THETA = 10000.0
CHUNK = 64
SUBCHUNK = 16
SCAN_TILE = 512
D_FF = 2816
CONV_W = 3
EPS = 1e-6
PAGE_SIZE = 128

LANES = 128
SUBLANES = 8
VMEM_LIMIT = 56 * 1024 * 1024

Z_GL = 0
Z_RQ = 3072
Z_RK = 3328
Z_RV = 3584
Z_RG = 4096
Z_GQ = 4608
Z_GK = 4864
Z_GV = 5120
Z_GG = 5632
Z_MQ = 6144
Z_MK = 6656
Z_MV = 7168
Z_W = 7680
SEG_W = 512
SEG_RQK = Z_RQ // SEG_W
SEG_GQK = Z_GQ // SEG_W
SEG_MQ = Z_MQ // SEG_W
SEG_MK = Z_MK // SEG_W

SAMPLE_ROWS = 8
NEG_BIG = -1e30
MASK_BIG = 2.0 ** 100
LOG2E = 1.4426950408889634
MOBA_GROUP = 4

NT = (((1,), (1,)), ((), ()))
TN = (((0,), (0,)), ((), ()))


def _dot(a, b, dims=None, precision=None):
    if dims is None:
        return jnp.dot(a, b, preferred_element_type=F32, precision=precision)
    return lax.dot_general(a, b, dims, preferred_element_type=F32, precision=precision)


def _rms(x, g):
    return x * lax.rsqrt(jnp.mean(x * x, axis=-1, keepdims=True) + EPS) * g


def _silu(x):
    return x / (1.0 + jnp.exp(-x))


def _sigmoid(x):
    return 1.0 / (1.0 + jnp.exp(-x))


def _cparams(sem):
    return pltpu.CompilerParams(dimension_semantics=sem, vmem_limit_bytes=VMEM_LIMIT)


def _inproj_kernel(x_ref, g_ref, w_ref, wga_ref, w2_ref, ba_ref, cos_ref, sin_ref,
                   z_ref, la_ref, h_scr, *, sample):
    j = pl.program_id(1)

    @pl.when(j == 0)
    def _():
        hb = _rms(x_ref[...], g_ref[...]).astype(BF16)
        h_scr[...] = hb
        ga = _dot(hb, wga_ref[...])
        u = _dot(ga.astype(BF16), w2_ref[...]) + ba_ref[...]
        la = (jnp.minimum(u, 0.0) - jnp.log(1.0 + jnp.exp(-jnp.abs(u)))) * (1.0 / GLA_TAU)
        if sample:
            row = lax.broadcasted_iota(jnp.int32, (la.shape[0], 1), 0)
            la = jnp.where((row % SAMPLE_ROWS) >= SAMPLE_ROWS // 2, la, 0.0)
        la_ref[...] = la

    acc = _dot(h_scr[...], w_ref[...])
    lane = lax.broadcasted_iota(jnp.int32, (1, LANES), 1)
    first = (lane % MOBA_HD) < (MOBA_HD // 2)

    def rope(a):
        sw = jnp.where(first, pltpu.roll(a, LANES - MOBA_HD // 2, 1), pltpu.roll(a, MOBA_HD // 2, 1))
        return a * cos_ref[...] + sw * sin_ref[...]

    def store_rope(scales):
        for c, s in enumerate(scales):
            r = rope(acc[:, c * LANES:(c + 1) * LANES])
            z_ref[:, c * LANES:(c + 1) * LANES] = r if s == 1.0 else r * s

    ks = RET_DK ** -0.5

    @pl.when(j == SEG_RQK)
    def _():
        store_rope((1.0, 1.0, ks, ks))

    @pl.when(j == SEG_GQK)
    def _():
        z_ref[:, :SEG_W // 2] = acc[:, :SEG_W // 2]
        z_ref[:, SEG_W // 2:] = acc[:, SEG_W // 2:] * (GLA_DK ** -0.5)

    @pl.when(j == SEG_MQ)
    def _():
        store_rope((MOBA_HD ** -0.5,) * 4)

    @pl.when(j == SEG_MK)
    def _():
        store_rope((1.0,) * 4)

    plain = (j != SEG_RQK) & (j != SEG_GQK) & (j != SEG_MQ) & (j != SEG_MK)

    @pl.when(plain)
    def _():
        z_ref[...] = acc


def _inproj(x, g, w_main, w_ga, w2, ba, cos, sin, *, tm, sample):
    n = x.shape[0]
    nseg = Z_W // SEG_W
    ntab = cos.shape[0] // tm
    return pl.pallas_call(
        functools.partial(_inproj_kernel, sample=sample),
        grid=(n // tm, nseg),
        in_specs=[
            pl.BlockSpec((tm, D_MODEL), lambda i, j: (i, 0)),
            pl.BlockSpec((1, D_MODEL), lambda i, j: (0, 0)),
            pl.BlockSpec((D_MODEL, SEG_W), lambda i, j: (0, j)),
            pl.BlockSpec((D_MODEL, LANES), lambda i, j: (0, 0)),
            pl.BlockSpec((LANES, GLA_HEADS * GLA_DK), lambda i, j: (0, 0)),
            pl.BlockSpec((1, GLA_HEADS * GLA_DK), lambda i, j: (0, 0)),
            pl.BlockSpec((tm, LANES), lambda i, j: (i % ntab, 0)),
            pl.BlockSpec((tm, LANES), lambda i, j: (i % ntab, 0)),
        ],
        out_specs=[
            pl.BlockSpec((tm, SEG_W), lambda i, j: (i, j)),
            pl.BlockSpec((tm, GLA_HEADS * GLA_DK), lambda i, j: (i, 0)),
        ],
        out_shape=[jax.ShapeDtypeStruct((n, Z_W), F32),
                   jax.ShapeDtypeStruct((n, GLA_HEADS * GLA_DK), F32)],
        scratch_shapes=[pltpu.VMEM((tm, D_MODEL), BF16)],
        compiler_params=_cparams(("parallel", "arbitrary")),
        name="inproj",
    )(x, g, w_main, w_ga, w2, ba, cos, sin)


BNT = (((2,), (2,)), ((0,), (0,)))
BNN = (((2,), (1,)), ((0,), (0,)))


def _decay_tables(c, valid, cg):
    log_g = np.log1p(-np.exp2(-5.0 - np.arange(RET_HEADS, dtype=np.float64)))
    b = log_g[:, None] * np.cumsum(np.asarray(valid, np.float64))[None, :]
    diff = b[:, :, None] - b[:, None, :]
    causal = np.arange(c)[:, None] >= np.arange(c)[None, :]
    dmat = np.where(causal[None], np.exp(np.where(causal[None], diff, 0.0)), 0.0)
    qdec = np.broadcast_to(np.exp(b)[:, :, None], (RET_HEADS, c, LANES))
    kdec = np.broadcast_to(np.exp(b[:, -1:] - b)[:, :, None], (RET_HEADS, c, LANES))
    cdec = tuple(float(v) for v in np.exp(b[:, -1]))
    same = (np.arange(c)[:, None] // cg) == (np.arange(c)[None, :] // cg)
    tri = (causal & same).astype(np.float32)
    return (jnp.asarray(dmat, F32), jnp.asarray(qdec, F32), jnp.asarray(kdec, F32),
            jnp.asarray(tri, F32), cdec)


def _scan_kernel(*refs, T, C, SC, cdec, zero_init):
    if zero_init:
        (zr_ref, zg_ref, la_ref, dmat_ref, qdec_ref, kdec_ref, tri_ref, gr_ref, gg_ref,
         o_ref, sr_ref, sg_ref, Sr, SgT) = refs
    else:
        (zr_ref, zg_ref, la_ref, dmat_ref, qdec_ref, kdec_ref, tri_ref, gr_ref, gg_ref,
         s0r_ref, s0g_ref, o_ref, sr_ref, sg_ref, Sr, SgT) = refs
    t = pl.program_id(1)
    nc, nsub, nsc = T // C, C // SC, T // SC
    HK = RET_HEADS * RET_DK
    lane = lax.broadcasted_iota(jnp.int32, (1, LANES), 1)
    in_e = [lane < RET_DK, lane >= RET_DK]
    rowp = lax.broadcasted_iota(jnp.int32, (LANES, 1), 0)

    @pl.when(t == 0)
    def _():
        if zero_init:
            Sr[...] = jnp.zeros_like(Sr)
            SgT[...] = jnp.zeros_like(SgT)
        else:
            for p in range(RET_HEADS // 2):
                for e in range(2):
                    Sr[p, e * RET_DK:(e + 1) * RET_DK, :] = s0r_ref[0, 2 * p + e]
                SgT[p] = jnp.concatenate([s0g_ref[0, 2 * p].T, s0g_ref[0, 2 * p + 1].T], axis=1)


    for p in range(RET_HEADS // 2):
        qp = zr_ref[:, p * LANES:(p + 1) * LANES]
        kp = zr_ref[:, HK + p * LANES:HK + (p + 1) * LANES]
        kpb = kp.astype(BF16)
        s_old = Sr[p]
        s_new = s_old * jnp.where(rowp < RET_DK, cdec[2 * p], cdec[2 * p + 1])
        for e in range(2):
            h = 2 * p + e
            qh = jnp.where(in_e[e], qp, 0.0)
            vh = zr_ref[:, 2 * HK + h * RET_DV:2 * HK + (h + 1) * RET_DV].astype(BF16)
            att = _dot(qh.astype(BF16), kpb, NT) * dmat_ref[h]
            o = _dot(att.astype(BF16), vh) + _dot((qh * qdec_ref[h]).astype(BF16), s_old.astype(BF16))
            s_new = s_new + _dot((jnp.where(in_e[e], kp, 0.0) * kdec_ref[h]).T.astype(BF16), vh)
            cen = o - jnp.mean(o, axis=-1, keepdims=True)
            y = cen * lax.rsqrt(jnp.mean(cen * cen, axis=-1, keepdims=True) + EPS)
            rg = zr_ref[:, 2 * HK + BRANCH_W + h * RET_DV:2 * HK + BRANCH_W + (h + 1) * RET_DV]
            y = y * gr_ref[:, h * RET_DV:(h + 1) * RET_DV] * _silu(rg)
            o_ref[:, h * RET_DV:(h + 1) * RET_DV] = y.astype(o_ref.dtype)
        Sr[p] = s_new

    q = zg_ref[:, 0:HK]
    k = zg_ref[:, HK:2 * HK]
    b = _dot(tri_ref[...], la_ref[...], precision=lax.Precision.HIGHEST)
    q3, k3, b3 = (a.reshape(nc, C, HK) for a in (q, k, b))
    blast3 = b3[:, C - 1:C, :]
    qe3 = q3 * jnp.exp(b3)
    kd3 = k3 * jnp.exp(blast3 - b3)
    eblast3 = jnp.exp(blast3)
    q4, k4, b4 = (a.reshape(nsc, SC, HK) for a in (q, k, b))
    rsub4 = lax.broadcasted_iota(jnp.int32, (1, SC, 1), 1)
    rowT = lax.broadcasted_iota(jnp.int32, (T, 1), 0)
    sub0 = (rowT % C) - (rowT % SC)
    lane_c = lax.broadcasted_iota(jnp.int32, (1, C), 1)
    rc3 = lax.broadcasted_iota(jnp.int32, (1, C, 1), 1)
    in_e3 = [m.reshape(1, 1, LANES) for m in in_e]

    att = [jnp.zeros((T, C), F32) for _ in range(GLA_HEADS)]
    for jj in range(SC):
        w = q4 * k4[:, jj:jj + 1, :] * jnp.exp(jnp.where(rsub4 >= jj, b4 - b4[:, jj:jj + 1, :], NEG_BIG))
        hit = lane_c == sub0 + jj
        for h in range(GLA_HEADS):
            wp = w[:, :, (h // 2) * LANES:(h // 2 + 1) * LANES]
            col = jnp.sum(jnp.where(in_e3[h % 2], wp, 0.0), axis=-1, keepdims=True).reshape(T, 1)
            att[h] = jnp.where(hit, col, att[h])
    att3 = [a.reshape(nc, C, C) for a in att]
    for i in range(1, nsub):
        bi = b3[:, i * SC - 1:i * SC, :]
        in_i = (rc3 >= i * SC) & (rc3 < (i + 1) * SC)
        qi = q3 * jnp.exp(jnp.where(in_i, b3 - bi, NEG_BIG))
        ki = (k3 * jnp.exp(jnp.where(rc3 < i * SC, bi - b3, NEG_BIG))).astype(BF16)
        for h in range(GLA_HEADS):
            ps = slice((h // 2) * LANES, (h // 2 + 1) * LANES)
            qih = jnp.where(in_e3[h % 2], qi[:, :, ps], 0.0).astype(BF16)
            att3[h] = att3[h] + lax.dot_general(qih, ki[:, :, ps], BNT, preferred_element_type=F32)
    for p in range(GLA_HEADS // 2):
        ps = slice(p * LANES, (p + 1) * LANES)
        o_intra, qeb, ut3 = [], [], None
        for e in range(2):
            h = 2 * p + e
            v3 = zg_ref[:, 2 * HK + h * GLA_DV:2 * HK + (h + 1) * GLA_DV].reshape(nc, C, GLA_DV)
            o_intra.append(lax.dot_general(att3[h].astype(BF16), v3.astype(BF16), BNN, preferred_element_type=F32))
            qeb.append(jnp.where(in_e3[e], qe3[:, :, ps], 0.0).astype(BF16))
            u = lax.dot_general(jnp.swapaxes(v3, 1, 2).astype(BF16),
                                jnp.where(in_e3[e], kd3[:, :, ps], 0.0).astype(BF16), BNN,
                                preferred_element_type=F32)
            ut3 = u if ut3 is None else ut3 + u
        st = SgT[p]
        outs = [[], []]
        for c in range(nc):
            stb = st.astype(BF16)
            for e in range(2):
                outs[e].append(o_intra[e][c] + _dot(qeb[e][c], stb, NT))
            st = st * eblast3[c][:, ps] + ut3[c]
        SgT[p] = st
        for e in range(2):
            h = 2 * p + e
            o = outs[e][0] if nc == 1 else jnp.concatenate(outs[e], axis=0)
            gg = zg_ref[:, 2 * HK + BRANCH_W + h * GLA_DV:2 * HK + BRANCH_W + (h + 1) * GLA_DV]
            y = _rms(o, gg_ref[:, h * GLA_DV:(h + 1) * GLA_DV]) * _silu(gg)
            o_ref[:, BRANCH_W + h * GLA_DV:BRANCH_W + (h + 1) * GLA_DV] = y.astype(o_ref.dtype)

    @pl.when(t == pl.num_programs(1) - 1)
    def _():
        for p in range(RET_HEADS // 2):
            for e in range(2):
                sr_ref[0, 2 * p + e] = Sr[p, e * RET_DK:(e + 1) * RET_DK, :]
                sg_ref[0, 2 * p + e] = SgT[p][:, e * GLA_DK:(e + 1) * GLA_DK].T


def _scan(z, la, g_ret, g_gla, tables, s0, *, nseq, seq_rows, T, C, SC, out_dtype):
    dmat, qdec, kdec, tri, cdec = tables
    nT = seq_rows // T
    zero_init = s0 is None
    zw = 2 * RET_HEADS * RET_DK + 2 * BRANCH_W
    const3 = lambda s, t: (0, 0, 0)
    const2 = lambda s, t: (0, 0)
    in_specs = [
        pl.BlockSpec((T, zw), lambda s, t: (s * nT + t, Z_RQ // zw)),
        pl.BlockSpec((T, zw), lambda s, t: (s * nT + t, Z_GQ // zw)),
        pl.BlockSpec((T, GLA_HEADS * GLA_DK), lambda s, t: (s * nT + t, 0)),
        pl.BlockSpec(dmat.shape, const3),
        pl.BlockSpec(qdec.shape, const3),
        pl.BlockSpec(kdec.shape, const3),
        pl.BlockSpec(tri.shape, const2),
        pl.BlockSpec((1, BRANCH_W), const2),
        pl.BlockSpec((1, BRANCH_W), const2),
    ]
    args = [z, z, la, dmat, qdec, kdec, tri, g_ret, g_gla]
    st_spec = pl.BlockSpec((1, RET_HEADS, RET_DK, RET_DV), lambda s, t: (s, 0, 0, 0))
    if not zero_init:
        in_specs += [st_spec, st_spec]
        args += [s0[0], s0[1]]
    st_shape = jax.ShapeDtypeStruct((nseq, RET_HEADS, RET_DK, RET_DV), F32)
    return pl.pallas_call(
        functools.partial(_scan_kernel, T=T, C=C, SC=SC, cdec=cdec, zero_init=zero_init),
        grid=(nseq, nT),
        in_specs=in_specs,
        out_specs=[pl.BlockSpec((T, 2 * BRANCH_W), lambda s, t: (s * nT + t, 0)), st_spec, st_spec],
        out_shape=[jax.ShapeDtypeStruct((nseq * seq_rows, 2 * BRANCH_W), out_dtype), st_shape, st_shape],
        scratch_shapes=[pltpu.VMEM((RET_HEADS // 2, LANES, RET_DV), F32),
                        pltpu.VMEM((GLA_HEADS // 2, GLA_DV, LANES), F32)],
        compiler_params=_cparams(("parallel", "arbitrary")),
        name="scan",
    )(*args)


def _top3_select(g, lane):
    sel = jnp.zeros(g.shape, jnp.bool_)
    for _ in range(MOBA_TOPK):
        m = jnp.max(g, axis=-1, keepdims=True)
        idx = jnp.min(jnp.where(g == m, lane, 4 * LANES), axis=-1, keepdims=True)
        pick = (lane == idx) & (m > -jnp.inf)
        sel = sel | pick
        g = jnp.where(pick, -jnp.inf, g)
    return sel


def _moba_prompt_kernel(q_ref, k_ref, v_ref, o_ref, km_scr, ka_scr, vb_scr, *, nblk):
    t = pl.program_id(2)
    BLK = MOBA_BLOCK
    G = MOBA_GROUP
    lane = lax.broadcasted_iota(jnp.int32, (1, LANES), 1)
    in_e = [(lane >= e * MOBA_HD) & (lane < (e + 1) * MOBA_HD) for e in range(2)]
    off = [MOBA_HD, 0]

    @pl.when(t == 0)
    def _():
        km_scr[...] = jnp.zeros_like(km_scr)
        for blk in range(nblk):
            m = jnp.mean(k_ref[blk * BLK:(blk + 1) * BLK, :], axis=0, keepdims=True)
            km_scr[0, MOBA_HD + blk:MOBA_HD + blk + 1, :] = m
            km_scr[1, blk:blk + 1, :] = m

        def prep(blk, c):
            r0 = pl.multiple_of(blk * BLK, BLK)
            kj = k_ref[pl.ds(r0, BLK), :]
            for e in range(2):
                ka_scr[e, pl.ds(r0, BLK), :] = jnp.where(
                    in_e[e], kj, jnp.where(lane == off[e] + blk, 1.0, 0.0)).astype(BF16)
            vb_scr[pl.ds(r0, BLK), :] = v_ref[pl.ds(r0, BLK), :].astype(BF16)
            return c
        lax.fori_loop(0, nblk, prep, 0)

    q = q_ref[...]
    q2 = q * LOG2E
    t0 = pl.multiple_of(t * BLK, BLK)
    vt = vb_scr[pl.ds(t0, BLK), :]
    rowi = lax.broadcasted_iota(jnp.int32, (BLK, 1), 0)
    coli = lax.broadcasted_iota(jnp.int32, (1, BLK), 1)
    qaug, init = [], []
    for e in range(2):
        qm = jnp.where(in_e[e], q, 0.0)
        gate = _dot(qm, km_scr[e], NT, precision=lax.Precision.HIGHEST)
        blkid = lane - off[e]
        g = jnp.where((blkid >= 0) & (blkid < t), gate, -jnp.inf)
        sel = _top3_select(g, lane)
        in_blk = (blkid >= 0) & (blkid < nblk)
        qaug.append(jnp.where(in_e[e], q2, jnp.where(in_blk & jnp.logical_not(sel), -MASK_BIG, 0.0)).astype(BF16))
        s = _dot(jnp.where(in_e[e], q2, 0.0).astype(BF16), ka_scr[e, pl.ds(t0, BLK), :], NT)
        s = jnp.where(coli <= rowi, s, NEG_BIG)
        m0 = jnp.max(s, axis=-1, keepdims=True)
        p = jnp.exp2(s - m0)
        init += [m0, jnp.sum(p, axis=-1, keepdims=True), _dot(p.astype(BF16), vt)]

    def body(g, carry):
        r0 = pl.multiple_of(g * (G * BLK), G * BLK)
        vg = vb_scr[pl.ds(r0, G * BLK), :]
        out = []
        for e in range(2):
            m, l, acc = carry[3 * e:3 * e + 3]
            s = _dot(qaug[e], ka_scr[e, pl.ds(r0, G * BLK), :], NT)
            mn = jnp.maximum(m, jnp.max(s, axis=-1, keepdims=True))
            a = jnp.exp2(m - mn)
            p = jnp.exp2(s - mn)
            out += [mn, a * l + jnp.sum(p, axis=-1, keepdims=True), a * acc + _dot(p.astype(BF16), vg)]
        return tuple(out)

    res = lax.fori_loop(0, (t + G - 1) // G, body, tuple(init))
    o_ref[...] = jnp.where(lane < MOBA_HD, res[2] / res[1], res[5] / res[4]).astype(o_ref.dtype)


def _moba_prompt(z, *, nseq, seq_rows):
    nblk = seq_rows // MOBA_BLOCK
    npair = MOBA_HEADS // 2
    assert nblk % MOBA_GROUP == 0 and nblk <= MOBA_HD
    return pl.pallas_call(
        functools.partial(_moba_prompt_kernel, nblk=nblk),
        grid=(nseq, npair, nblk),
        in_specs=[
            pl.BlockSpec((MOBA_BLOCK, LANES), lambda b, hp, t: (b * nblk + t, Z_MQ // LANES + hp)),
            pl.BlockSpec((seq_rows, LANES), lambda b, hp, t: (b, Z_MK // LANES + hp)),
            pl.BlockSpec((seq_rows, LANES), lambda b, hp, t: (b, Z_MV // LANES + hp)),
        ],
        out_specs=pl.BlockSpec((MOBA_BLOCK, LANES), lambda b, hp, t: (b * nblk + t, hp)),
        out_shape=jax.ShapeDtypeStruct((nseq * seq_rows, BRANCH_W), BF16),
        scratch_shapes=[pltpu.VMEM((2, LANES, LANES), F32),
                        pltpu.VMEM((2, seq_rows, LANES), BF16),
                        pltpu.VMEM((seq_rows, LANES), BF16)],
        compiler_params=_cparams(("parallel", "parallel", "arbitrary")),
        name="moba_prompt",
    )(z, z, z)


PAGES_PER_STEP = 16
PAGES_PER_BLOCK = MOBA_BLOCK // PAGE_SIZE


def _kmean_kernel(pt_ref, *refs):
    page_refs = refs[:PAGES_PER_STEP]
    km_ref = refs[PAGES_PER_STEP]
    for blk in range(PAGES_PER_STEP // PAGES_PER_BLOCK):
        tot = page_refs[blk * PAGES_PER_BLOCK][...]
        for p in range(1, PAGES_PER_BLOCK):
            tot = tot + page_refs[blk * PAGES_PER_BLOCK + p][...]
        km_ref[0, 0, :, :, :, blk:blk + 1] = jnp.sum(tot, axis=-1, keepdims=True) * (1.0 / MOBA_BLOCK)


def _kmean(cache_kt, pt_flat, *, nseq, npages):
    depth = cache_kt.shape[1]
    nfull = npages // PAGES_PER_BLOCK
    bps = PAGES_PER_STEP // PAGES_PER_BLOCK
    nstep = npages // PAGES_PER_STEP

    def page_spec(i):
        return pl.BlockSpec((None, depth, MOBA_HEADS, MOBA_HD, PAGE_SIZE),
                            lambda b, g, pt, i=i: (pt[b * npages + g * PAGES_PER_STEP + i], 0, 0, 0, 0))

    grid_spec = pltpu.PrefetchScalarGridSpec(
        num_scalar_prefetch=1,
        grid=(nseq, nstep),
        in_specs=[page_spec(i) for i in range(PAGES_PER_STEP)],
        out_specs=pl.BlockSpec((1, 1, depth, MOBA_HEADS, MOBA_HD, bps), lambda b, g, pt: (b, g, 0, 0, 0, 0)),
    )
    km = pl.pallas_call(
        _kmean_kernel,
        grid_spec=grid_spec,
        out_shape=jax.ShapeDtypeStruct((nseq, nstep, depth, MOBA_HEADS, MOBA_HD, bps), F32),
        compiler_params=_cparams(("parallel", "arbitrary")),
        name="kmean",
    )(pt_flat, *([cache_kt] * PAGES_PER_STEP))
    return km.transpose(0, 2, 3, 4, 1, 5).reshape(nseq, depth, MOBA_HEADS, MOBA_HD, nfull)


def _moba_topk_kernel(q_ref, km_ref, idx_ref):
    nfull = km_ref.shape[2]
    lane = lax.broadcasted_iota(jnp.int32, (1, LANES), 1)
    lane_b = lax.broadcasted_iota(jnp.int32, (1, nfull), 1)
    out = jnp.zeros((SAMPLE_ROWS, LANES), jnp.int32)
    for h in range(MOBA_HEADS):
        gsc = _dot(q_ref[h], km_ref[h], precision=lax.Precision.HIGHEST)
        for r in range(MOBA_TOPK):
            m = jnp.max(gsc, axis=-1, keepdims=True)
            idx = jnp.min(jnp.where(gsc == m, lane_b, nfull), axis=-1, keepdims=True)
            out = jnp.where(lane == h * MOBA_TOPK + r, idx, out)
            gsc = jnp.where(lane_b == idx, -jnp.inf, gsc)
    idx_ref[0] = out


def _moba_topk(qh, kmean, *, layer, nseq):
    nfull = kmean.shape[4]
    return pl.pallas_call(
        _moba_topk_kernel,
        grid=(nseq,),
        in_specs=[
            pl.BlockSpec((MOBA_HEADS, SAMPLE_ROWS, MOBA_HD), lambda b: (0, b, 0)),
            pl.BlockSpec((None, None, MOBA_HEADS, MOBA_HD, nfull), lambda b: (b, layer, 0, 0, 0)),
        ],
        out_specs=pl.BlockSpec((1, SAMPLE_ROWS, LANES), lambda b: (b, 0, 0)),
        out_shape=jax.ShapeDtypeStruct((nseq, SAMPLE_ROWS, LANES), jnp.int32),
        compiler_params=_cparams(("parallel",)),
        name="moba_topk",
    )(qh, kmean)


def _moba_attend_kernel(idx_ref, pt_ref, q_ref, kn_ref, vn_ref, ck_hbm, cv_hbm, o_ref, kbuf, vbuf, sem,
                        *, layer, npages, ntok):
    b = pl.program_id(0)
    h = pl.program_id(1)
    nh = pl.num_programs(1)
    step = b * nh + h
    nstep = pl.num_programs(0) * nh
    slot = step % 2

    def copies(bb, hh, sl):
        out = []
        for t in range(ntok):
            for r in range(MOBA_TOPK):
                blk = idx_ref[((bb * ntok + t) * MOBA_HEADS + hh) * MOBA_TOPK + r]
                for p in range(PAGES_PER_BLOCK):
                    page = pt_ref[bb * npages + blk * PAGES_PER_BLOCK + p]
                    i = (t * MOBA_TOPK + r) * PAGES_PER_BLOCK + p
                    out.append(pltpu.make_async_copy(ck_hbm.at[page, layer, hh], kbuf.at[sl, i], sem.at[0, sl]))
                    out.append(pltpu.make_async_copy(cv_hbm.at[page, layer, hh], vbuf.at[sl, i], sem.at[1, sl]))
        return out

    @pl.when(step == 0)
    def _():
        for c in copies(b, h, slot):
            c.start()

    @pl.when(step + 1 < nstep)
    def _():
        nxt = step + 1
        for c in copies(nxt // nh, nxt % nh, 1 - slot):
            c.start()

    for c in copies(b, h, slot):
        c.wait()

    half = SAMPLE_ROWS // 2
    rowi = lax.broadcasted_iota(jnp.int32, (SAMPLE_ROWS, 1), 0)
    coln = lax.broadcasted_iota(jnp.int32, (1, SAMPLE_ROWS), 1)
    qb = q_ref[...].astype(BF16)
    vn = vn_ref[...].astype(BF16)
    s_new_all = _dot(qb, kn_ref[...].astype(BF16), NT)
    out = jnp.zeros((SAMPLE_ROWS, MOBA_HD), F32)
    for t in range(ntok):
        qrow = half + t
        s_new = jnp.where((coln >= half) & (coln <= qrow), s_new_all, NEG_BIG)
        ss, vs = [], []
        for i in range(t * MOBA_TOPK * PAGES_PER_BLOCK, (t + 1) * MOBA_TOPK * PAGES_PER_BLOCK):
            ss.append(_dot(qb, kbuf[slot, i].astype(BF16)))
            vs.append(vbuf[slot, i].astype(BF16))
        m = jnp.max(s_new, axis=-1, keepdims=True)
        for s in ss:
            m = jnp.maximum(m, jnp.max(s, axis=-1, keepdims=True))
        p_new = jnp.exp(s_new - m)
        l = jnp.sum(p_new, axis=-1, keepdims=True)
        o = _dot(p_new.astype(BF16), vn)
        for s, vb in zip(ss, vs):
            p = jnp.exp(s - m)
            l = l + jnp.sum(p, axis=-1, keepdims=True)
            o = o + _dot(p.astype(BF16), vb, NT)
        out = jnp.where(rowi == qrow, o / l, out)
    o_ref[...] = out


def _moba_attend(qh, knh, vnh, cache_k, cache_v, idx_flat, pt_flat, *, layer, nseq, npages, ntok):
    nbuf = ntok * MOBA_TOPK * PAGES_PER_BLOCK
    head_spec = pl.BlockSpec((None, SAMPLE_ROWS, MOBA_HD), lambda b, h, idx, pt: (h, b, 0))
    grid_spec = pltpu.PrefetchScalarGridSpec(
        num_scalar_prefetch=2,
        grid=(nseq, MOBA_HEADS),
        in_specs=[head_spec, head_spec, head_spec,
                  pl.BlockSpec(memory_space=pl.ANY), pl.BlockSpec(memory_space=pl.ANY)],
        out_specs=head_spec,
        scratch_shapes=[pltpu.VMEM((2, nbuf, MOBA_HD, PAGE_SIZE), F32),
                        pltpu.VMEM((2, nbuf, MOBA_HD, PAGE_SIZE), F32),
                        pltpu.SemaphoreType.DMA((2, 2))],
    )
    return pl.pallas_call(
        functools.partial(_moba_attend_kernel, layer=layer, npages=npages, ntok=ntok),
        grid_spec=grid_spec,
        out_shape=jax.ShapeDtypeStruct((MOBA_HEADS, nseq * SAMPLE_ROWS, MOBA_HD), F32),
        compiler_params=_cparams(("arbitrary", "arbitrary")),
        name="moba_attend",
    )(idx_flat, pt_flat, qh, knh, vnh, cache_k, cache_v)


def _post_kernel(x_ref, org_ref, om_ref, gl_ref, wb_ref, wo_ref, gpost_ref, gpre_ref, x1_ref, h2_ref):
    merged = None
    for n in range(N_BRANCH):
        if n < 2:
            br = org_ref[:, n * BRANCH_W:(n + 1) * BRANCH_W]
        else:
            br = om_ref[...]
        proj = _dot(br.astype(BF16), wb_ref[n])
        term = _sigmoid(gl_ref[:, n * D_MODEL:(n + 1) * D_MODEL]) * proj
        merged = term if merged is None else merged + term
    y = _dot(merged.astype(BF16), wo_ref[...])
    x1 = x_ref[...] + _rms(y, gpost_ref[...])
    x1_ref[...] = x1
    h2_ref[...] = _rms(x1, gpre_ref[...]).astype(BF16)


def _post(x, org, om, z, wb, wo, gpost, gpre, *, tm):
    n = x.shape[0]
    c2 = lambda i: (0, 0)
    return pl.pallas_call(
        _post_kernel,
        grid=(n // tm,),
        in_specs=[
            pl.BlockSpec((tm, D_MODEL), lambda i: (i, 0)),
            pl.BlockSpec((tm, 2 * BRANCH_W), lambda i: (i, 0)),
            pl.BlockSpec((tm, BRANCH_W), lambda i: (i, 0)),
            pl.BlockSpec((tm, N_BRANCH * D_MODEL), lambda i: (i, 0)),
            pl.BlockSpec((N_BRANCH, BRANCH_W, D_MODEL), lambda i: (0, 0, 0)),
            pl.BlockSpec((D_MODEL, D_MODEL), c2),
            pl.BlockSpec((1, D_MODEL), c2),
            pl.BlockSpec((1, D_MODEL), c2),
        ],
        out_specs=[pl.BlockSpec((tm, D_MODEL), lambda i: (i, 0)),
                   pl.BlockSpec((tm, D_MODEL), lambda i: (i, 0))],
        out_shape=[jax.ShapeDtypeStruct((n, D_MODEL), F32), jax.ShapeDtypeStruct((n, D_MODEL), BF16)],
        compiler_params=_cparams(("parallel",)),
        name="post",
    )(x, org, om, z, wb, wo, gpost, gpre)


FF_CHUNK = 256


def _ffn_kernel(*refs, sample, tiles_per_seq, tail_rows):
    if sample:
        (x1_ref, h2_ref, st_ref, wup_ref, wc_ref, bc_ref, wd_ref, g_ref, x2_ref, tail_ref, acc_scr) = refs
    else:
        (x1_ref, h2_ref, halo_ref, wup_ref, wc_ref, bc_ref, wd_ref, g_ref, x2_ref, tail_ref, acc_scr) = refs
    i = pl.program_id(0)
    tm = x1_ref.shape[0]
    h2 = h2_ref[...]
    row = lax.broadcasted_iota(jnp.int32, (tm, 1), 0)
    row8 = lax.broadcasted_iota(jnp.int32, (SUBLANES, 1), 0)
    if not sample:
        halo = jnp.where((i % tiles_per_seq) == 0, jnp.zeros_like(halo_ref[...]), halo_ref[...])

    def shifted(u, uh, sh):
        us = pltpu.roll(u, sh, 0)
        if uh is None:
            return us
        hs = pltpu.roll(uh, sh, 0)
        head = jnp.where(row8 < sh, hs, us[:SUBLANES])
        return jnp.concatenate([head, us[SUBLANES:]], axis=0)

    for c in range(D_FF // FF_CHUNK):
        halves = []
        for half in range(2):
            c0 = half * D_FF + c * FF_CHUNK
            w = wup_ref[:, c0:c0 + FF_CHUNK]
            u = _dot(h2, w)
            if sample:
                inj = ((row % SAMPLE_ROWS) >= SAMPLE_ROWS // 2 - (CONV_W - 1)) & ((row % SAMPLE_ROWS) < SAMPLE_ROWS // 2)
                u = jnp.where(inj, st_ref[:, c0:c0 + FF_CHUNK], u)
                uh = None
            else:
                uh = _dot(halo, w)
            tail_ref[0, :, c0:c0 + FF_CHUNK] = u[tm - tail_rows:]
            conv = bc_ref[:, c0:c0 + FF_CHUNK] + u * wc_ref[CONV_W - 1:CONV_W, c0:c0 + FF_CHUNK]
            for sh in range(1, CONV_W):
                conv = conv + shifted(u, uh, sh) * wc_ref[CONV_W - 1 - sh:CONV_W - sh, c0:c0 + FF_CHUNK]
            halves.append(conv)
        a, b = halves
        gelu = 0.5 * a * (1.0 + jnp.tanh(0.7978845608028654 * (a + 0.044715 * a * a * a)))
        part = _dot((gelu * b).astype(BF16), wd_ref[c * FF_CHUNK:(c + 1) * FF_CHUNK, :])
        if c == 0:
            acc_scr[...] = part
        else:
            acc_scr[...] += part
    x2 = x1_ref[...] + _rms(acc_scr[...], g_ref[...])
    if sample:
        x2 = jnp.where((row % SAMPLE_ROWS) >= SAMPLE_ROWS // 2, x2, 0.0)
    x2_ref[...] = x2


def _ffn(x1, h2, extra, wup, wc, bc, wd, g, *, tm, sample, nseq):
    n = x1.shape[0]
    ntile = n // tm
    tiles_per_seq = ntile // nseq if not sample else 1
    tail_rows = tm if sample else SUBLANES
    ntail = ntile if sample else nseq
    c2 = lambda i: (0, 0)
    if sample:
        extra_spec = pl.BlockSpec((tm, 2 * D_FF), lambda i: (i, 0))
        extra_arg = extra
    else:
        hb = tm // SUBLANES
        extra_spec = pl.BlockSpec((SUBLANES, D_MODEL), lambda i: (jnp.maximum(i * hb - 1, 0), 0))
        extra_arg = h2
    return pl.pallas_call(
        functools.partial(_ffn_kernel, sample=sample, tiles_per_seq=tiles_per_seq, tail_rows=tail_rows),
        grid=(ntile,),
        in_specs=[
            pl.BlockSpec((tm, D_MODEL), lambda i: (i, 0)),
            pl.BlockSpec((tm, D_MODEL), lambda i: (i, 0)),
            extra_spec,
            pl.BlockSpec((D_MODEL, 2 * D_FF), c2, pipeline_mode=pl.Buffered(1)),
            pl.BlockSpec((CONV_W, 2 * D_FF), c2),
            pl.BlockSpec((1, 2 * D_FF), c2),
            pl.BlockSpec((D_FF, D_MODEL), c2, pipeline_mode=pl.Buffered(1)),
            pl.BlockSpec((1, D_MODEL), c2),
        ],
        out_specs=[pl.BlockSpec((tm, D_MODEL), lambda i: (i, 0)),
                   pl.BlockSpec((1, tail_rows, 2 * D_FF), lambda i: (i // tiles_per_seq, 0, 0))],
        out_shape=[jax.ShapeDtypeStruct((n, D_MODEL), F32),
                   jax.ShapeDtypeStruct((ntail, tail_rows, 2 * D_FF), F32)],
        scratch_shapes=[pltpu.VMEM((tm, D_MODEL), F32)],
        compiler_params=_cparams(("arbitrary",)),
        name="ffn",
    )(x1, h2, extra_arg, wup, wc, bc, wd, g)


def _rope_tables(pos):
    half = MOBA_HD // 2
    freqs = jnp.power(ROPE_THETA, -jnp.arange(half, dtype=F32) / half)
    ang = pos.astype(F32)[:, None] * freqs[None, :]
    cos, sin = jnp.cos(ang), jnp.sin(ang)
    return (jnp.concatenate([cos, cos, cos, cos], axis=1),
            jnp.concatenate([-sin, sin, -sin, sin], axis=1))


def kernel(x_prompt, x_sample, cache_k, cache_v, page_table, state_ret, state_gla, state_conv, g_pre_mix, w_in, w_gla_a2, b_gla_a, g_ret_norm, g_gla_norm, w_branch, w_out, g_post_mix, g_pre_ffn, w_up, w_conv, b_conv, w_down, g_post_ffn):
    B, S, D = x_prompt.shape
    Bd, T, _ = x_sample.shape
    depth = w_in.shape[0]
    npages = page_table.shape[1]
    past_len = npages * PAGE_SIZE
    half = SAMPLE_ROWS // 2
    assert D == D_MODEL and T == half and S % 1024 == 0
    assert past_len % MOBA_BLOCK == 0 and npages % PAGES_PER_STEP == 0
    assert past_len // MOBA_BLOCK >= MOBA_TOPK
    n_pool = cache_k.shape[0]
    W = MOBA_HEADS * MOBA_HD

    xp = x_prompt.reshape(B * S, D)
    xs = jnp.pad(x_sample, ((0, 0), (half, 0), (0, 0))).reshape(Bd * SAMPLE_ROWS, D)
    ns = Bd * SAMPLE_ROWS

    cos_p, sin_p = _rope_tables(jnp.arange(S, dtype=jnp.int32))
    pos_s = past_len + jnp.tile(jnp.arange(SAMPLE_ROWS, dtype=jnp.int32) - half, Bd)
    cos_s, sin_s = _rope_tables(pos_s)

    tab_p = _decay_tables(SCAN_TILE, np.ones(SCAN_TILE), CHUNK)
    tab_s = _decay_tables(SAMPLE_ROWS, np.concatenate([np.zeros(half), np.ones(half)]), SAMPLE_ROWS)

    pt_flat = page_table.reshape(-1).astype(jnp.int32)
    cache_kt = cache_k.transpose(0, 2, 3, 4, 1)
    cache_vt = cache_v.transpose(0, 2, 3, 4, 1)
    kmean = _kmean(cache_kt, pt_flat, nseq=Bd, npages=npages)

    def per_head(a):
        return a.reshape(a.shape[0], MOBA_HEADS, MOBA_HD).transpose(1, 0, 2)

    outs = {k: [] for k in ("ret_p", "ret_s", "gla_p", "gla_s", "k_p", "v_p", "k_s", "v_s", "conv_p", "conv_s")}
    for l in range(depth):
        w = w_in[l]
        o_ga = 2 * RET_HEADS * RET_DK + 2 * BRANCH_W + 2 * GLA_HEADS * GLA_DK + 2 * BRANCH_W
        o_m = o_ga + GLA_RANK
        o_gl = o_m + 3 * BRANCH_W
        w_main = jnp.concatenate([w[:, o_gl:], w[:, :o_ga], w[:, o_m:o_gl]], axis=1).astype(BF16)
        w_ga = jnp.pad(w[:, o_ga:o_m], ((0, 0), (0, LANES - GLA_RANK))).astype(BF16)
        w2 = jnp.pad(w_gla_a2[l], ((0, LANES - GLA_RANK), (0, 0))).astype(BF16)
        ba = b_gla_a[l][None, :]
        g_pre = g_pre_mix[l][None, :]
        g_ret = g_ret_norm[l][None, :]
        g_gla = g_gla_norm[l][None, :]
        wb = w_branch[l].astype(BF16)
        wo = w_out[l].astype(BF16)
        gpost = g_post_mix[l][None, :]
        gpre2 = g_pre_ffn[l][None, :]
        wup = w_up[l].astype(BF16)
        wd = w_down[l].astype(BF16)
        wc = w_conv[l]
        bc = b_conv[l][None, :]
        gffn = g_post_ffn[l][None, :]

        zp, lap = _inproj(xp, g_pre, w_main, w_ga, w2, ba, cos_p, sin_p, tm=1024, sample=False)
        orgp, retp, glap = _scan(zp, lap, g_ret, g_gla, tab_p, None, nseq=B, seq_rows=S, T=SCAN_TILE,
                                 C=CHUNK, SC=SUBCHUNK, out_dtype=BF16)
        omp = _moba_prompt(zp, nseq=B, seq_rows=S)
        x1p, h2p = _post(xp, orgp, omp, zp, wb, wo, gpost, gpre2, tm=512)
        xp, tailp = _ffn(x1p, h2p, None, wup, wc, bc, wd, gffn, tm=512, sample=False, nseq=B)
        outs["ret_p"].append(retp)
        outs["gla_p"].append(glap)
        outs["k_p"].append(zp[:, Z_MK:Z_MK + W].reshape(B, S, MOBA_HEADS, MOBA_HD))
        outs["v_p"].append(zp[:, Z_MV:Z_MV + W].reshape(B, S, MOBA_HEADS, MOBA_HD))
        outs["conv_p"].append(tailp[:, SUBLANES - (CONV_W - 1):, :])

        zs, las = _inproj(xs, g_pre, w_main, w_ga, w2, ba, cos_s, sin_s, tm=ns, sample=True)
        orgs, rets, glas = _scan(zs, las, g_ret, g_gla, tab_s, (state_ret[l], state_gla[l]), nseq=Bd,
                                 seq_rows=SAMPLE_ROWS, T=SAMPLE_ROWS, C=SAMPLE_ROWS, SC=SAMPLE_ROWS,
                                 out_dtype=F32)
        qh = per_head(zs[:, Z_MQ:Z_MQ + W])
        idx = _moba_topk(qh, kmean, layer=l, nseq=Bd)
        idx_flat = idx[:, half:, :MOBA_HEADS * MOBA_TOPK].reshape(-1)
        oh = _moba_attend(qh, per_head(zs[:, Z_MK:Z_MK + W]), per_head(zs[:, Z_MV:Z_MV + W]), cache_kt, cache_vt,
                          idx_flat, pt_flat, layer=l, nseq=Bd, npages=npages, ntok=T)
        oms = oh.transpose(1, 0, 2).reshape(ns, W)
        x1s, h2s = _post(xs, orgs, oms, zs, wb, wo, gpost, gpre2, tm=ns)
        st = jnp.pad(state_conv[l], ((0, 0), (half - (CONV_W - 1), half), (0, 0))).reshape(ns, 2 * D_FF)
        xs, tails = _ffn(x1s, h2s, st, wup, wc, bc, wd, gffn, tm=ns, sample=True, nseq=Bd)
        zs3 = zs.reshape(Bd, SAMPLE_ROWS, Z_W)
        outs["ret_s"].append(rets)
        outs["gla_s"].append(glas)
        outs["k_s"].append(zs3[:, half:, Z_MK:Z_MK + W].reshape(Bd, T, MOBA_HEADS, MOBA_HD))
        outs["v_s"].append(zs3[:, half:, Z_MV:Z_MV + W].reshape(Bd, T, MOBA_HEADS, MOBA_HD))
        outs["conv_s"].append(tails.reshape(Bd, SAMPLE_ROWS, 2 * D_FF)[:, SAMPLE_ROWS - (CONV_W - 1):, :])

    yp = xp.reshape(B, S, D)
    ys = xs.reshape(Bd, SAMPLE_ROWS, D)[:, half:, :]
    return (yp, ys,
            jnp.stack(outs["ret_p"], axis=0), jnp.stack(outs["ret_s"], axis=0),
            jnp.stack(outs["gla_p"], axis=0), jnp.stack(outs["gla_s"], axis=0),
            jnp.stack(outs["k_p"], axis=2), jnp.stack(outs["v_p"], axis=2),
            jnp.stack(outs["k_s"], axis=2), jnp.stack(outs["v_s"], axis=2),
            jnp.stack(outs["conv_p"], axis=0), jnp.stack(outs["conv_s"], axis=0))
```

```python
import functools

import numpy as np
import jax
import jax.numpy as jnp
from jax import lax
from jax.experimental import pallas as pl
from jax.experimental.pallas import tpu as pltpu

F32 = jnp.float32
BF16 = jnp.bfloat16

D_MODEL = 1024
BRANCH_W = 512
N_BRANCH = 3
RET_HEADS = 4
RET_DK = 64
RET_DV = 128
GLA_HEADS = 4
GLA_DK = 64
GLA_DV = 128
GLA_RANK = 16
GLA_TAU = 16.0
MOBA_HEADS = 8
MOBA_HD = 64
MOBA_BLOCK = 256
MOBA_TOPK = 3
ROPE_THETA = 10000.0
CHUNK = 64
SUBCHUNK = 16
SCAN_TILE = 512
D_FF = 2816
CONV_W = 3
EPS = 1e-6
PAGE_SIZE = 128

LANES = 128
SUBLANES = 8
VMEM_LIMIT = 56 * 1024 * 1024

Z_GL = 0
Z_RQ = 3072
Z_RK = 3328
Z_RV = 3584
Z_RG = 4096
Z_GQ = 4608
Z_GK = 4864
Z_GV = 5120
Z_GG = 5632
Z_MQ = 6144
Z_MK = 6656
Z_MV = 7168
Z_W = 7680
SEG_W = 512
SEG_RQK = Z_RQ // SEG_W
SEG_GQK = Z_GQ // SEG_W
SEG_MQ = Z_MQ // SEG_W
SEG_MK = Z_MK // SEG_W

SAMPLE_ROWS = 8
NEG_BIG = -1e30
MASK_BIG = 2.0 ** 100
LOG2E = 1.4426950408889634
MOBA_GROUP = 8

NT = (((1,), (1,)), ((), ()))
TN = (((0,), (0,)), ((), ()))


def _dot(a, b, dims=None, precision=None):
    if dims is None:
        return jnp.dot(a, b, preferred_element_type=F32, precision=precision)
    return lax.dot_general(a, b, dims, preferred_element_type=F32, precision=precision)


def _rms(x, g):
    return x * lax.rsqrt(jnp.mean(x * x, axis=-1, keepdims=True) + EPS) * g


def _silu(x):
    return x / (1.0 + jnp.exp(-x))


def _sigmoid(x):
    return 1.0 / (1.0 + jnp.exp(-x))


def _cparams(sem):
    return pltpu.CompilerParams(dimension_semantics=sem, vmem_limit_bytes=VMEM_LIMIT)


def _inproj_kernel(x_ref, g_ref, w_ref, wga_ref, w2_ref, ba_ref, cos_ref, sin_ref,
                   z_ref, la_ref, h_scr, *, sample):
    j = pl.program_id(1)

    @pl.when(j == 0)
    def _():
        hb = _rms(x_ref[...], g_ref[...]).astype(BF16)
        h_scr[...] = hb
        ga = _dot(hb, wga_ref[...])
        u = _dot(ga.astype(BF16), w2_ref[...]) + ba_ref[...]
        la = (jnp.minimum(u, 0.0) - jnp.log(1.0 + jnp.exp(-jnp.abs(u)))) * (1.0 / GLA_TAU)
        if sample:
            row = lax.broadcasted_iota(jnp.int32, (la.shape[0], 1), 0)
            la = jnp.where((row % SAMPLE_ROWS) >= SAMPLE_ROWS // 2, la, 0.0)
        la_ref[...] = la

    acc = _dot(h_scr[...], w_ref[...])
    lane = lax.broadcasted_iota(jnp.int32, (1, LANES), 1)
    first = (lane % MOBA_HD) < (MOBA_HD // 2)

    def rope(a):
        sw = jnp.where(first, pltpu.roll(a, LANES - MOBA_HD // 2, 1), pltpu.roll(a, MOBA_HD // 2, 1))
        return a * cos_ref[...] + sw * sin_ref[...]

    def store_rope(scales):
        for c, s in enumerate(scales):
            r = rope(acc[:, c * LANES:(c + 1) * LANES])
            z_ref[:, c * LANES:(c + 1) * LANES] = r if s == 1.0 else r * s

    ks = RET_DK ** -0.5

    @pl.when(j == SEG_RQK)
    def _():
        store_rope((1.0, 1.0, ks, ks))

    @pl.when(j == SEG_GQK)
    def _():
        z_ref[:, :SEG_W // 2] = acc[:, :SEG_W // 2]
        z_ref[:, SEG_W // 2:] = acc[:, SEG_W // 2:] * (GLA_DK ** -0.5)

    @pl.when(j == SEG_MQ)
    def _():
        store_rope((MOBA_HD ** -0.5,) * 4)

    @pl.when(j == SEG_MK)
    def _():
        store_rope((1.0,) * 4)

    plain = (j != SEG_RQK) & (j != SEG_GQK) & (j != SEG_MQ) & (j != SEG_MK)

    @pl.when(plain)
    def _():
        z_ref[...] = acc


def _inproj(x, g, w_main, w_ga, w2, ba, cos, sin, *, tm, sample):
    n = x.shape[0]
    nseg = Z_W // SEG_W
    ntab = cos.shape[0] // tm
    return pl.pallas_call(
        functools.partial(_inproj_kernel, sample=sample),
        grid=(n // tm, nseg),
        in_specs=[
            pl.BlockSpec((tm, D_MODEL), lambda i, j: (i, 0)),
            pl.BlockSpec((1, D_MODEL), lambda i, j: (0, 0)),
            pl.BlockSpec((D_MODEL, SEG_W), lambda i, j: (0, j)),
            pl.BlockSpec((D_MODEL, LANES), lambda i, j: (0, 0)),
            pl.BlockSpec((LANES, GLA_HEADS * GLA_DK), lambda i, j: (0, 0)),
            pl.BlockSpec((1, GLA_HEADS * GLA_DK), lambda i, j: (0, 0)),
            pl.BlockSpec((tm, LANES), lambda i, j: (i % ntab, 0)),
            pl.BlockSpec((tm, LANES), lambda i, j: (i % ntab, 0)),
        ],
        out_specs=[
            pl.BlockSpec((tm, SEG_W), lambda i, j: (i, j)),
            pl.BlockSpec((tm, GLA_HEADS * GLA_DK), lambda i, j: (i, 0)),
        ],
        out_shape=[jax.ShapeDtypeStruct((n, Z_W), F32),
                   jax.ShapeDtypeStruct((n, GLA_HEADS * GLA_DK), F32)],
        scratch_shapes=[pltpu.VMEM((tm, D_MODEL), BF16)],
        compiler_params=_cparams(("parallel", "arbitrary")),
        name="inproj",
    )(x, g, w_main, w_ga, w2, ba, cos, sin)


BNT = (((2,), (2,)), ((0,), (0,)))
BNN = (((2,), (1,)), ((0,), (0,)))


def _decay_tables(c, valid, cg):
    log_g = np.log1p(-np.exp2(-5.0 - np.arange(RET_HEADS, dtype=np.float64)))
    b = log_g[:, None] * np.cumsum(np.asarray(valid, np.float64))[None, :]
    diff = b[:, :, None] - b[:, None, :]
    causal = np.arange(c)[:, None] >= np.arange(c)[None, :]
    dmat = np.where(causal[None], np.exp(np.where(causal[None], diff, 0.0)), 0.0)
    qdec = np.broadcast_to(np.exp(b)[:, :, None], (RET_HEADS, c, LANES))
    kdec = np.broadcast_to(np.exp(b[:, -1:] - b)[:, :, None], (RET_HEADS, c, LANES))
    cdec = tuple(float(v) for v in np.exp(b[:, -1]))
    same = (np.arange(c)[:, None] // cg) == (np.arange(c)[None, :] // cg)
    tri = (causal & same).astype(np.float32)
    return (jnp.asarray(dmat, F32), jnp.asarray(qdec, F32), jnp.asarray(kdec, F32),
            jnp.asarray(tri, F32), cdec)


def _scan_kernel(*refs, T, C, SC, cdec, zero_init):
    if zero_init:
        (zr_ref, zg_ref, la_ref, dmat_ref, qdec_ref, kdec_ref, tri_ref, gr_ref, gg_ref,
         o_ref, sr_ref, sg_ref, Sr, SgT) = refs
    else:
        (zr_ref, zg_ref, la_ref, dmat_ref, qdec_ref, kdec_ref, tri_ref, gr_ref, gg_ref,
         s0r_ref, s0g_ref, o_ref, sr_ref, sg_ref, Sr, SgT) = refs
    t = pl.program_id(1)
    nc, nsub, nsc = T // C, C // SC, T // SC
    HK = RET_HEADS * RET_DK
    lane = lax.broadcasted_iota(jnp.int32, (1, LANES), 1)
    in_e = [lane < RET_DK, lane >= RET_DK]
    rowp = lax.broadcasted_iota(jnp.int32, (LANES, 1), 0)

    @pl.when(t == 0)
    def _():
        if zero_init:
            Sr[...] = jnp.zeros_like(Sr)
            SgT[...] = jnp.zeros_like(SgT)
        else:
            for p in range(RET_HEADS // 2):
                for e in range(2):
                    Sr[p, e * RET_DK:(e + 1) * RET_DK, :] = s0r_ref[0, 2 * p + e]
                SgT[p] = jnp.concatenate([s0g_ref[0, 2 * p].T, s0g_ref[0, 2 * p + 1].T], axis=1)


    for p in range(RET_HEADS // 2):
        qp = zr_ref[:, p * LANES:(p + 1) * LANES]
        kp = zr_ref[:, HK + p * LANES:HK + (p + 1) * LANES]
        kpb = kp.astype(BF16)
        s_old = Sr[p]
        s_new = s_old * jnp.where(rowp < RET_DK, cdec[2 * p], cdec[2 * p + 1])
        for e in range(2):
            h = 2 * p + e
            qh = jnp.where(in_e[e], qp, 0.0)
            vh = zr_ref[:, 2 * HK + h * RET_DV:2 * HK + (h + 1) * RET_DV].astype(BF16)
            att = _dot(qh.astype(BF16), kpb, NT) * dmat_ref[h]
            o = _dot(att.astype(BF16), vh) + _dot((qh * qdec_ref[h]).astype(BF16), s_old.astype(BF16))
            s_new = s_new + _dot((jnp.where(in_e[e], kp, 0.0) * kdec_ref[h]).T.astype(BF16), vh)
            cen = o - jnp.mean(o, axis=-1, keepdims=True)
            y = cen * lax.rsqrt(jnp.mean(cen * cen, axis=-1, keepdims=True) + EPS)
            rg = zr_ref[:, 2 * HK + BRANCH_W + h * RET_DV:2 * HK + BRANCH_W + (h + 1) * RET_DV]
            y = y * gr_ref[:, h * RET_DV:(h + 1) * RET_DV] * _silu(rg)
            o_ref[:, h * RET_DV:(h + 1) * RET_DV] = y.astype(o_ref.dtype)
        Sr[p] = s_new

    q = zg_ref[:, 0:HK]
    k = zg_ref[:, HK:2 * HK]
    b = _dot(tri_ref[...], la_ref[...], precision=lax.Precision.HIGHEST)
    q3, k3, b3 = (a.reshape(nc, C, HK) for a in (q, k, b))
    blast3 = b3[:, C - 1:C, :]
    qe3 = q3 * jnp.exp(b3)
    kd3 = k3 * jnp.exp(blast3 - b3)
    eblast3 = jnp.exp(blast3)
    q4, k4, b4 = (a.reshape(nsc, SC, HK) for a in (q, k, b))
    rsub4 = lax.broadcasted_iota(jnp.int32, (1, SC, 1), 1)
    rowT = lax.broadcasted_iota(jnp.int32, (T, 1), 0)
    sub0 = (rowT % C) - (rowT % SC)
    lane_c = lax.broadcasted_iota(jnp.int32, (1, C), 1)
    rc3 = lax.broadcasted_iota(jnp.int32, (1, C, 1), 1)
    in_e3 = [m.reshape(1, 1, LANES) for m in in_e]

    att = [jnp.zeros((T, C), F32) for _ in range(GLA_HEADS)]
    for jj in range(SC):
        w = q4 * k4[:, jj:jj + 1, :] * jnp.exp(jnp.where(rsub4 >= jj, b4 - b4[:, jj:jj + 1, :], NEG_BIG))
        hit = lane_c == sub0 + jj
        for h in range(GLA_HEADS):
            wp = w[:, :, (h // 2) * LANES:(h // 2 + 1) * LANES]
            col = jnp.sum(jnp.where(in_e3[h % 2], wp, 0.0), axis=-1, keepdims=True).reshape(T, 1)
            att[h] = jnp.where(hit, col, att[h])
    att3 = [a.reshape(nc, C, C) for a in att]
    for i in range(1, nsub):
        bi = b3[:, i * SC - 1:i * SC, :]
        in_i = (rc3 >= i * SC) & (rc3 < (i + 1) * SC)
        qi = q3 * jnp.exp(jnp.where(in_i, b3 - bi, NEG_BIG))
        ki = (k3 * jnp.exp(jnp.where(rc3 < i * SC, bi - b3, NEG_BIG))).astype(BF16)
        for h in range(GLA_HEADS):
            ps = slice((h // 2) * LANES, (h // 2 + 1) * LANES)
            qih = jnp.where(in_e3[h % 2], qi[:, :, ps], 0.0).astype(BF16)
            att3[h] = att3[h] + lax.dot_general(qih, ki[:, :, ps], BNT, preferred_element_type=F32)
    for p in range(GLA_HEADS // 2):
        ps = slice(p * LANES, (p + 1) * LANES)
        o_intra, qeb, ut3 = [], [], None
        for e in range(2):
            h = 2 * p + e
            v3 = zg_ref[:, 2 * HK + h * GLA_DV:2 * HK + (h + 1) * GLA_DV].reshape(nc, C, GLA_DV)
            o_intra.append(lax.dot_general(att3[h].astype(BF16), v3.astype(BF16), BNN, preferred_element_type=F32))
            qeb.append(jnp.where(in_e3[e], qe3[:, :, ps], 0.0).astype(BF16))
            u = lax.dot_general(jnp.swapaxes(v3, 1, 2).astype(BF16),
                                jnp.where(in_e3[e], kd3[:, :, ps], 0.0).astype(BF16), BNN,
                                preferred_element_type=F32)
            ut3 = u if ut3 is None else ut3 + u
        st = SgT[p]
        outs = [[], []]
        for c in range(nc):
            stb = st.astype(BF16)
            for e in range(2):
                outs[e].append(o_intra[e][c] + _dot(qeb[e][c], stb, NT))
            st = st * eblast3[c][:, ps] + ut3[c]
        SgT[p] = st
        for e in range(2):
            h = 2 * p + e
            o = outs[e][0] if nc == 1 else jnp.concatenate(outs[e], axis=0)
            gg = zg_ref[:, 2 * HK + BRANCH_W + h * GLA_DV:2 * HK + BRANCH_W + (h + 1) * GLA_DV]
            y = _rms(o, gg_ref[:, h * GLA_DV:(h + 1) * GLA_DV]) * _silu(gg)
            o_ref[:, BRANCH_W + h * GLA_DV:BRANCH_W + (h + 1) * GLA_DV] = y.astype(o_ref.dtype)

    @pl.when(t == pl.num_programs(1) - 1)
    def _():
        for p in range(RET_HEADS // 2):
            for e in range(2):
                sr_ref[0, 2 * p + e] = Sr[p, e * RET_DK:(e + 1) * RET_DK, :]
                sg_ref[0, 2 * p + e] = SgT[p][:, e * GLA_DK:(e + 1) * GLA_DK].T


def _scan(z, la, g_ret, g_gla, tables, s0, *, nseq, seq_rows, T, C, SC, out_dtype):
    dmat, qdec, kdec, tri, cdec = tables
    nT = seq_rows // T
    zero_init = s0 is None
    zw = 2 * RET_HEADS * RET_DK + 2 * BRANCH_W
    const3 = lambda s, t: (0, 0, 0)
    const2 = lambda s, t: (0, 0)
    in_specs = [
        pl.BlockSpec((T, zw), lambda s, t: (s * nT + t, Z_RQ // zw)),
        pl.BlockSpec((T, zw), lambda s, t: (s * nT + t, Z_GQ // zw)),
        pl.BlockSpec((T, GLA_HEADS * GLA_DK), lambda s, t: (s * nT + t, 0)),
        pl.BlockSpec(dmat.shape, const3),
        pl.BlockSpec(qdec.shape, const3),
        pl.BlockSpec(kdec.shape, const3),
        pl.BlockSpec(tri.shape, const2),
        pl.BlockSpec((1, BRANCH_W), const2),
        pl.BlockSpec((1, BRANCH_W), const2),
    ]
    args = [z, z, la, dmat, qdec, kdec, tri, g_ret, g_gla]
    st_spec = pl.BlockSpec((1, RET_HEADS, RET_DK, RET_DV), lambda s, t: (s, 0, 0, 0))
    if not zero_init:
        in_specs += [st_spec, st_spec]
        args += [s0[0], s0[1]]
    st_shape = jax.ShapeDtypeStruct((nseq, RET_HEADS, RET_DK, RET_DV), F32)
    return pl.pallas_call(
        functools.partial(_scan_kernel, T=T, C=C, SC=SC, cdec=cdec, zero_init=zero_init),
        grid=(nseq, nT),
        in_specs=in_specs,
        out_specs=[pl.BlockSpec((T, 2 * BRANCH_W), lambda s, t: (s * nT + t, 0)), st_spec, st_spec],
        out_shape=[jax.ShapeDtypeStruct((nseq * seq_rows, 2 * BRANCH_W), out_dtype), st_shape, st_shape],
        scratch_shapes=[pltpu.VMEM((RET_HEADS // 2, LANES, RET_DV), F32),
                        pltpu.VMEM((GLA_HEADS // 2, GLA_DV, LANES), F32)],
        compiler_params=_cparams(("parallel", "arbitrary")),
        name="scan",
    )(*args)


def _moba_prompt_kernel(q_ref, k_ref, v_ref, o_ref, km_scr, ka_scr, vt_scr, *, nblk):
    t = pl.program_id(2)
    BLK = MOBA_BLOCK
    G = MOBA_GROUP
    lane = lax.broadcasted_iota(jnp.int32, (1, LANES), 1)
    in_e = [lane < MOBA_HD, lane >= MOBA_HD]
    off = [MOBA_HD, 0]

    @pl.when(t == 0)
    def _():
        for blk in range(nblk):
            km_scr[blk:blk + 1, :] = jnp.mean(k_ref[blk * BLK:(blk + 1) * BLK, :], axis=0, keepdims=True)

        def prep(blk, c):
            r0 = pl.multiple_of(blk * BLK, BLK)
            kj = k_ref[pl.ds(r0, BLK), :]
            for e in range(2):
                ka_scr[e, pl.ds(r0, BLK), :] = jnp.where(
                    in_e[e], kj, jnp.where(lane == off[e] + blk, 1.0, 0.0)).astype(BF16)
            vt_scr[:, pl.ds(r0, BLK)] = v_ref[pl.ds(r0, BLK), :].T.astype(BF16)
            return c
        lax.fori_loop(0, nblk, prep, 0)

    q = q_ref[...]
    q2t = (q * LOG2E).T
    t0 = pl.multiple_of(t * BLK, BLK)
    blkrow = lax.broadcasted_iota(jnp.int32, (nblk, 1), 0)
    rowk = lax.broadcasted_iota(jnp.int32, (BLK, 1), 0)
    laneq = lax.broadcasted_iota(jnp.int32, (1, BLK), 1)
    zpad = jnp.zeros((MOBA_HD - nblk, BLK), F32)
    zhead = jnp.zeros((MOBA_HD, BLK), F32)
    waug, init = [], []
    for e in range(2):
        g = _dot(km_scr[...], jnp.where(in_e[e], q, 0.0), NT, precision=lax.Precision.HIGHEST)
        g = jnp.where(blkrow < t, g, -jnp.inf)
        sel = jnp.zeros(g.shape, jnp.bool_)
        for _ in range(MOBA_TOPK):
            m = jnp.max(g, axis=0, keepdims=True)
            idx = jnp.min(jnp.where(g == m, blkrow, nblk), axis=0, keepdims=True)
            pick = (blkrow == idx) & (m > -jnp.inf)
            sel = sel | pick
            g = jnp.where(pick, -jnp.inf, g)
        pen = jnp.where(sel, 0.0, -MASK_BIG)
        qe = q2t[e * MOBA_HD:(e + 1) * MOBA_HD, :]
        if e == 0:
            waug.append(jnp.concatenate([qe, pen, zpad], axis=0).astype(BF16))
            wown = jnp.concatenate([qe, zhead], axis=0).astype(BF16)
        else:
            waug.append(jnp.concatenate([pen, zpad, qe], axis=0).astype(BF16))
            wown = jnp.concatenate([zhead, qe], axis=0).astype(BF16)
        s = _dot(ka_scr[e, pl.ds(t0, BLK), :], wown)
        s = jnp.where(rowk <= laneq, s, NEG_BIG)
        m0 = jnp.max(s, axis=0, keepdims=True)
        p = jnp.exp2(s - m0)
        init += [m0, jnp.sum(p, axis=0, keepdims=True),
                 _dot(vt_scr[e * MOBA_HD:(e + 1) * MOBA_HD, pl.ds(t0, BLK)], p.astype(BF16))]

    def body(gi, carry):
        r0 = pl.multiple_of(gi * (G * BLK), G * BLK)
        st = list(carry)
        ss = [[_dot(ka_scr[e, pl.ds(r0 + j * BLK, BLK), :], waug[e]) for j in range(G)] for e in range(2)]
        for j in range(G):
            for e in range(2):
                m, l, acc = st[3 * e:3 * e + 3]
                s = ss[e][j]
                mn = jnp.maximum(m, jnp.max(s, axis=0, keepdims=True))
                a = jnp.exp2(m - mn)
                p = jnp.exp2(s - mn)
                st[3 * e:3 * e + 3] = [mn, a * l + jnp.sum(p, axis=0, keepdims=True),
                                       a * acc + _dot(vt_scr[e * MOBA_HD:(e + 1) * MOBA_HD, pl.ds(r0 + j * BLK, BLK)],
                                                      p.astype(BF16))]
        return tuple(st)

    res = lax.fori_loop(0, (t + G - 1) // G, body, tuple(init))
    ot = jnp.concatenate([res[2] / res[1], res[5] / res[4]], axis=0)
    o_ref[...] = ot.T.astype(o_ref.dtype)


def _moba_prompt(z, *, nseq, seq_rows):
    nblk = seq_rows // MOBA_BLOCK
    npair = MOBA_HEADS // 2
    assert nblk % MOBA_GROUP == 0 and nblk <= MOBA_HD and nblk % SUBLANES == 0
    return pl.pallas_call(
        functools.partial(_moba_prompt_kernel, nblk=nblk),
        grid=(nseq, npair, nblk),
        in_specs=[
            pl.BlockSpec((MOBA_BLOCK, LANES), lambda b, hp, t: (b * nblk + t, Z_MQ // LANES + hp)),
            pl.BlockSpec((seq_rows, LANES), lambda b, hp, t: (b, Z_MK // LANES + hp)),
            pl.BlockSpec((seq_rows, LANES), lambda b, hp, t: (b, Z_MV // LANES + hp)),
        ],
        out_specs=pl.BlockSpec((MOBA_BLOCK, LANES), lambda b, hp, t: (b * nblk + t, hp)),
        out_shape=jax.ShapeDtypeStruct((nseq * seq_rows, BRANCH_W), BF16),
        scratch_shapes=[pltpu.VMEM((nblk, LANES), F32),
                        pltpu.VMEM((2, seq_rows, LANES), BF16),
                        pltpu.VMEM((LANES, seq_rows), BF16)],
        compiler_params=_cparams(("parallel", "parallel", "arbitrary")),
        name="moba_prompt",
    )(z, z, z)


PAGES_PER_STEP = 16
PAGES_PER_BLOCK = MOBA_BLOCK // PAGE_SIZE


def _kmean_kernel(pt_ref, *refs):
    page_refs = refs[:PAGES_PER_STEP]
    km_ref = refs[PAGES_PER_STEP]
    for blk in range(PAGES_PER_STEP // PAGES_PER_BLOCK):
        tot = page_refs[blk * PAGES_PER_BLOCK][...]
        for p in range(1, PAGES_PER_BLOCK):
            tot = tot + page_refs[blk * PAGES_PER_BLOCK + p][...]
        km_ref[0, 0, :, :, :, blk:blk + 1] = jnp.sum(tot, axis=-1, keepdims=True) * (1.0 / MOBA_BLOCK)


def _kmean(cache_kt, pt_flat, *, nseq, npages):
    depth = cache_kt.shape[1]
    nfull = npages // PAGES_PER_BLOCK
    bps = PAGES_PER_STEP // PAGES_PER_BLOCK
    nstep = npages // PAGES_PER_STEP

    def page_spec(i):
        return pl.BlockSpec((None, depth, MOBA_HEADS, MOBA_HD, PAGE_SIZE),
                            lambda b, g, pt, i=i: (pt[b * npages + g * PAGES_PER_STEP + i], 0, 0, 0, 0))

    grid_spec = pltpu.PrefetchScalarGridSpec(
        num_scalar_prefetch=1,
        grid=(nseq, nstep),
        in_specs=[page_spec(i) for i in range(PAGES_PER_STEP)],
        out_specs=pl.BlockSpec((1, 1, depth, MOBA_HEADS, MOBA_HD, bps), lambda b, g, pt: (b, g, 0, 0, 0, 0)),
    )
    km = pl.pallas_call(
        _kmean_kernel,
        grid_spec=grid_spec,
        out_shape=jax.ShapeDtypeStruct((nseq, nstep, depth, MOBA_HEADS, MOBA_HD, bps), F32),
        compiler_params=_cparams(("parallel", "arbitrary")),
        name="kmean",
    )(pt_flat, *([cache_kt] * PAGES_PER_STEP))
    return km.transpose(0, 2, 3, 4, 1, 5).reshape(nseq, depth, MOBA_HEADS, MOBA_HD, nfull)


def _moba_topk_kernel(q_ref, km_ref, idx_ref):
    nfull = km_ref.shape[2]
    lane = lax.broadcasted_iota(jnp.int32, (1, LANES), 1)
    lane_b = lax.broadcasted_iota(jnp.int32, (1, nfull), 1)
    out = jnp.zeros((SAMPLE_ROWS, LANES), jnp.int32)
    for h in range(MOBA_HEADS):
        gsc = _dot(q_ref[h], km_ref[h], precision=lax.Precision.HIGHEST)
        for r in range(MOBA_TOPK):
            m = jnp.max(gsc, axis=-1, keepdims=True)
            idx = jnp.min(jnp.where(gsc == m, lane_b, nfull), axis=-1, keepdims=True)
            out = jnp.where(lane == h * MOBA_TOPK + r, idx, out)
            gsc = jnp.where(lane_b == idx, -jnp.inf, gsc)
    idx_ref[0] = out


def _moba_topk(qh, kmean, *, layer, nseq):
    nfull = kmean.shape[4]
    return pl.pallas_call(
        _moba_topk_kernel,
        grid=(nseq,),
        in_specs=[
            pl.BlockSpec((MOBA_HEADS, SAMPLE_ROWS, MOBA_HD), lambda b: (0, b, 0)),
            pl.BlockSpec((None, None, MOBA_HEADS, MOBA_HD, nfull), lambda b: (b, layer, 0, 0, 0)),
        ],
        out_specs=pl.BlockSpec((1, SAMPLE_ROWS, LANES), lambda b: (b, 0, 0)),
        out_shape=jax.ShapeDtypeStruct((nseq, SAMPLE_ROWS, LANES), jnp.int32),
        compiler_params=_cparams(("parallel",)),
        name="moba_topk",
    )(qh, kmean)


def _moba_attend_kernel(idx_ref, pt_ref, q_ref, kn_ref, vn_ref, ck_hbm, cv_hbm, o_ref, kbuf, vbuf, sem,
                        *, layer, npages, ntok):
    b = pl.program_id(0)
    h = pl.program_id(1)
    nh = pl.num_programs(1)
    step = b * nh + h
    nstep = pl.num_programs(0) * nh
    slot = step % 2

    def copies(bb, hh, sl):
        out = []
        for t in range(ntok):
            for r in range(MOBA_TOPK):
                blk = idx_ref[((bb * ntok + t) * MOBA_HEADS + hh) * MOBA_TOPK + r]
                for p in range(PAGES_PER_BLOCK):
                    page = pt_ref[bb * npages + blk * PAGES_PER_BLOCK + p]
                    i = (t * MOBA_TOPK + r) * PAGES_PER_BLOCK + p
                    out.append(pltpu.make_async_copy(ck_hbm.at[page, layer, hh], kbuf.at[sl, i], sem.at[0, sl]))
                    out.append(pltpu.make_async_copy(cv_hbm.at[page, layer, hh], vbuf.at[sl, i], sem.at[1, sl]))
        return out

    @pl.when(step == 0)
    def _():
        for c in copies(b, h, slot):
            c.start()

    @pl.when(step + 1 < nstep)
    def _():
        nxt = step + 1
        for c in copies(nxt // nh, nxt % nh, 1 - slot):
            c.start()

    for c in copies(b, h, slot):
        c.wait()

    half = SAMPLE_ROWS // 2
    rowi = lax.broadcasted_iota(jnp.int32, (SAMPLE_ROWS, 1), 0)
    coln = lax.broadcasted_iota(jnp.int32, (1, SAMPLE_ROWS), 1)
    qb = q_ref[...].astype(BF16)
    vn = vn_ref[...].astype(BF16)
    s_new_all = _dot(qb, kn_ref[...].astype(BF16), NT)
    out = jnp.zeros((SAMPLE_ROWS, MOBA_HD), F32)
    for t in range(ntok):
        qrow = half + t
        s_new = jnp.where((coln >= half) & (coln <= qrow), s_new_all, NEG_BIG)
        ss, vs = [], []
        for i in range(t * MOBA_TOPK * PAGES_PER_BLOCK, (t + 1) * MOBA_TOPK * PAGES_PER_BLOCK):
            ss.append(_dot(qb, kbuf[slot, i].astype(BF16)))
            vs.append(vbuf[slot, i].astype(BF16))
        m = jnp.max(s_new, axis=-1, keepdims=True)
        for s in ss:
            m = jnp.maximum(m, jnp.max(s, axis=-1, keepdims=True))
        p_new = jnp.exp(s_new - m)
        l = jnp.sum(p_new, axis=-1, keepdims=True)
        o = _dot(p_new.astype(BF16), vn)
        for s, vb in zip(ss, vs):
            p = jnp.exp(s - m)
            l = l + jnp.sum(p, axis=-1, keepdims=True)
            o = o + _dot(p.astype(BF16), vb, NT)
        out = jnp.where(rowi == qrow, o / l, out)
    o_ref[...] = out


def _moba_attend(qh, knh, vnh, cache_k, cache_v, idx_flat, pt_flat, *, layer, nseq, npages, ntok):
    nbuf = ntok * MOBA_TOPK * PAGES_PER_BLOCK
    head_spec = pl.BlockSpec((None, SAMPLE_ROWS, MOBA_HD), lambda b, h, idx, pt: (h, b, 0))
    grid_spec = pltpu.PrefetchScalarGridSpec(
        num_scalar_prefetch=2,
        grid=(nseq, MOBA_HEADS),
        in_specs=[head_spec, head_spec, head_spec,
                  pl.BlockSpec(memory_space=pl.ANY), pl.BlockSpec(memory_space=pl.ANY)],
        out_specs=head_spec,
        scratch_shapes=[pltpu.VMEM((2, nbuf, MOBA_HD, PAGE_SIZE), F32),
                        pltpu.VMEM((2, nbuf, MOBA_HD, PAGE_SIZE), F32),
                        pltpu.SemaphoreType.DMA((2, 2))],
    )
    return pl.pallas_call(
        functools.partial(_moba_attend_kernel, layer=layer, npages=npages, ntok=ntok),
        grid_spec=grid_spec,
        out_shape=jax.ShapeDtypeStruct((MOBA_HEADS, nseq * SAMPLE_ROWS, MOBA_HD), F32),
        compiler_params=_cparams(("arbitrary", "arbitrary")),
        name="moba_attend",
    )(idx_flat, pt_flat, qh, knh, vnh, cache_k, cache_v)


def _post_kernel(x_ref, org_ref, om_ref, gl_ref, wb_ref, wo_ref, gpost_ref, gpre_ref, x1_ref, h2_ref):
    merged = None
    for n in range(N_BRANCH):
        if n < 2:
            br = org_ref[:, n * BRANCH_W:(n + 1) * BRANCH_W]
        else:
            br = om_ref[...]
        proj = _dot(br.astype(BF16), wb_ref[n])
        term = _sigmoid(gl_ref[:, n * D_MODEL:(n + 1) * D_MODEL]) * proj
        merged = term if merged is None else merged + term
    y = _dot(merged.astype(BF16), wo_ref[...])
    x1 = x_ref[...] + _rms(y, gpost_ref[...])
    x1_ref[...] = x1
    h2_ref[...] = _rms(x1, gpre_ref[...]).astype(BF16)


def _post(x, org, om, z, wb, wo, gpost, gpre, *, tm):
    n = x.shape[0]
    c2 = lambda i: (0, 0)
    return pl.pallas_call(
        _post_kernel,
        grid=(n // tm,),
        in_specs=[
            pl.BlockSpec((tm, D_MODEL), lambda i: (i, 0)),
            pl.BlockSpec((tm, 2 * BRANCH_W), lambda i: (i, 0)),
            pl.BlockSpec((tm, BRANCH_W), lambda i: (i, 0)),
            pl.BlockSpec((tm, N_BRANCH * D_MODEL), lambda i: (i, 0)),
            pl.BlockSpec((N_BRANCH, BRANCH_W, D_MODEL), lambda i: (0, 0, 0)),
            pl.BlockSpec((D_MODEL, D_MODEL), c2),
            pl.BlockSpec((1, D_MODEL), c2),
            pl.BlockSpec((1, D_MODEL), c2),
        ],
        out_specs=[pl.BlockSpec((tm, D_MODEL), lambda i: (i, 0)),
                   pl.BlockSpec((tm, D_MODEL), lambda i: (i, 0))],
        out_shape=[jax.ShapeDtypeStruct((n, D_MODEL), F32), jax.ShapeDtypeStruct((n, D_MODEL), BF16)],
        compiler_params=_cparams(("parallel",)),
        name="post",
    )(x, org, om, z, wb, wo, gpost, gpre)


FF_CHUNK = 256


def _ffn_kernel(*refs, sample, tiles_per_seq, tail_rows):
    if sample:
        (x1_ref, h2_ref, st_ref, wup_ref, wc_ref, bc_ref, wd_ref, g_ref, x2_ref, tail_ref, acc_scr) = refs
    else:
        (x1_ref, h2_ref, halo_ref, wup_ref, wc_ref, bc_ref, wd_ref, g_ref, x2_ref, tail_ref, acc_scr) = refs
    i = pl.program_id(0)
    tm = x1_ref.shape[0]
    h2 = h2_ref[...]
    row = lax.broadcasted_iota(jnp.int32, (tm, 1), 0)
    row8 = lax.broadcasted_iota(jnp.int32, (SUBLANES, 1), 0)
    if not sample:
        halo = jnp.where((i % tiles_per_seq) == 0, jnp.zeros_like(halo_ref[...]), halo_ref[...])

    def shifted(u, uh, sh):
        us = pltpu.roll(u, sh, 0)
        if uh is None:
            return us
        hs = pltpu.roll(uh, sh, 0)
        head = jnp.where(row8 < sh, hs, us[:SUBLANES])
        return jnp.concatenate([head, us[SUBLANES:]], axis=0)

    for c in range(D_FF // FF_CHUNK):
        halves = []
        for half in range(2):
            c0 = half * D_FF + c * FF_CHUNK
            w = wup_ref[:, c0:c0 + FF_CHUNK]
            u = _dot(h2, w)
            if sample:
                inj = ((row % SAMPLE_ROWS) >= SAMPLE_ROWS // 2 - (CONV_W - 1)) & ((row % SAMPLE_ROWS) < SAMPLE_ROWS // 2)
                u = jnp.where(inj, st_ref[:, c0:c0 + FF_CHUNK], u)
                uh = None
            else:
                uh = _dot(halo, w)
            tail_ref[0, :, c0:c0 + FF_CHUNK] = u[tm - tail_rows:]
            conv = bc_ref[:, c0:c0 + FF_CHUNK] + u * wc_ref[CONV_W - 1:CONV_W, c0:c0 + FF_CHUNK]
            for sh in range(1, CONV_W):
                conv = conv + shifted(u, uh, sh) * wc_ref[CONV_W - 1 - sh:CONV_W - sh, c0:c0 + FF_CHUNK]
            halves.append(conv)
        a, b = halves
        gelu = 0.5 * a * (1.0 + jnp.tanh(0.7978845608028654 * (a + 0.044715 * a * a * a)))
        part = _dot((gelu * b).astype(BF16), wd_ref[c * FF_CHUNK:(c + 1) * FF_CHUNK, :])
        if c == 0:
            acc_scr[...] = part
        else:
            acc_scr[...] += part
    x2 = x1_ref[...] + _rms(acc_scr[...], g_ref[...])
    if sample:
        x2 = jnp.where((row % SAMPLE_ROWS) >= SAMPLE_ROWS // 2, x2, 0.0)
    x2_ref[...] = x2


def _ffn(x1, h2, extra, wup, wc, bc, wd, g, *, tm, sample, nseq):
    n = x1.shape[0]
    ntile = n // tm
    tiles_per_seq = ntile // nseq if not sample else 1
    tail_rows = tm if sample else SUBLANES
    ntail = ntile if sample else nseq
    c2 = lambda i: (0, 0)
    if sample:
        extra_spec = pl.BlockSpec((tm, 2 * D_FF), lambda i: (i, 0))
        extra_arg = extra
    else:
        hb = tm // SUBLANES
        extra_spec = pl.BlockSpec((SUBLANES, D_MODEL), lambda i: (jnp.maximum(i * hb - 1, 0), 0))
        extra_arg = h2
    return pl.pallas_call(
        functools.partial(_ffn_kernel, sample=sample, tiles_per_seq=tiles_per_seq, tail_rows=tail_rows),
        grid=(ntile,),
        in_specs=[
            pl.BlockSpec((tm, D_MODEL), lambda i: (i, 0)),
            pl.BlockSpec((tm, D_MODEL), lambda i: (i, 0)),
            extra_spec,
            pl.BlockSpec((D_MODEL, 2 * D_FF), c2, pipeline_mode=pl.Buffered(1)),
            pl.BlockSpec((CONV_W, 2 * D_FF), c2),
            pl.BlockSpec((1, 2 * D_FF), c2),
            pl.BlockSpec((D_FF, D_MODEL), c2, pipeline_mode=pl.Buffered(1)),
            pl.BlockSpec((1, D_MODEL), c2),
        ],
        out_specs=[pl.BlockSpec((tm, D_MODEL), lambda i: (i, 0)),
                   pl.BlockSpec((1, tail_rows, 2 * D_FF), lambda i: (i // tiles_per_seq, 0, 0))],
        out_shape=[jax.ShapeDtypeStruct((n, D_MODEL), F32),
                   jax.ShapeDtypeStruct((ntail, tail_rows, 2 * D_FF), F32)],
        scratch_shapes=[pltpu.VMEM((tm, D_MODEL), F32)],
        compiler_params=_cparams(("arbitrary",)),
        name="ffn",
    )(x1, h2, extra_arg, wup, wc, bc, wd, g)


def _rope_tables(pos):
    half = MOBA_HD // 2
    freqs = jnp.power(ROPE_THETA, -jnp.arange(half, dtype=F32) / half)
    ang = pos.astype(F32)[:, None] * freqs[None, :]
    cos, sin = jnp.cos(ang), jnp.sin(ang)
    return (jnp.concatenate([cos, cos, cos, cos], axis=1),
            jnp.concatenate([-sin, sin, -sin, sin], axis=1))


def kernel(x_prompt, x_sample, cache_k, cache_v, page_table, state_ret, state_gla, state_conv, g_pre_mix, w_in, w_gla_a2, b_gla_a, g_ret_norm, g_gla_norm, w_branch, w_out, g_post_mix, g_pre_ffn, w_up, w_conv, b_conv, w_down, g_post_ffn):
    B, S, D = x_prompt.shape
    Bd, T, _ = x_sample.shape
    depth = w_in.shape[0]
    npages = page_table.shape[1]
    past_len = npages * PAGE_SIZE
    half = SAMPLE_ROWS // 2
    assert D == D_MODEL and T == half and S % 1024 == 0
    assert past_len % MOBA_BLOCK == 0 and npages % PAGES_PER_STEP == 0
    assert past_len // MOBA_BLOCK >= MOBA_TOPK
    n_pool = cache_k.shape[0]
    W = MOBA_HEADS * MOBA_HD

    xp = x_prompt.reshape(B * S, D)
    xs = jnp.pad(x_sample, ((0, 0), (half, 0), (0, 0))).reshape(Bd * SAMPLE_ROWS, D)
    ns = Bd * SAMPLE_ROWS

    cos_p, sin_p = _rope_tables(jnp.arange(S, dtype=jnp.int32))
    pos_s = past_len + jnp.tile(jnp.arange(SAMPLE_ROWS, dtype=jnp.int32) - half, Bd)
    cos_s, sin_s = _rope_tables(pos_s)

    tab_p = _decay_tables(SCAN_TILE, np.ones(SCAN_TILE), CHUNK)
    tab_s = _decay_tables(SAMPLE_ROWS, np.concatenate([np.zeros(half), np.ones(half)]), SAMPLE_ROWS)

    pt_flat = page_table.reshape(-1).astype(jnp.int32)
    cache_kt = cache_k.transpose(0, 2, 3, 4, 1)
    cache_vt = cache_v.transpose(0, 2, 3, 4, 1)
    kmean = _kmean(cache_kt, pt_flat, nseq=Bd, npages=npages)

    def per_head(a):
        return a.reshape(a.shape[0], MOBA_HEADS, MOBA_HD).transpose(1, 0, 2)

    outs = {k: [] for k in ("ret_p", "ret_s", "gla_p", "gla_s", "k_p", "v_p", "k_s", "v_s", "conv_p", "conv_s")}
    for l in range(depth):
        w = w_in[l]
        o_ga = 2 * RET_HEADS * RET_DK + 2 * BRANCH_W + 2 * GLA_HEADS * GLA_DK + 2 * BRANCH_W
        o_m = o_ga + GLA_RANK
        o_gl = o_m + 3 * BRANCH_W
        w_main = jnp.concatenate([w[:, o_gl:], w[:, :o_ga], w[:, o_m:o_gl]], axis=1).astype(BF16)
        w_ga = jnp.pad(w[:, o_ga:o_m], ((0, 0), (0, LANES - GLA_RANK))).astype(BF16)
        w2 = jnp.pad(w_gla_a2[l], ((0, LANES - GLA_RANK), (0, 0))).astype(BF16)
        ba = b_gla_a[l][None, :]
        g_pre = g_pre_mix[l][None, :]
        g_ret = g_ret_norm[l][None, :]
        g_gla = g_gla_norm[l][None, :]
        wb = w_branch[l].astype(BF16)
        wo = w_out[l].astype(BF16)
        gpost = g_post_mix[l][None, :]
        gpre2 = g_pre_ffn[l][None, :]
        wup = w_up[l].astype(BF16)
        wd = w_down[l].astype(BF16)
        wc = w_conv[l]
        bc = b_conv[l][None, :]
        gffn = g_post_ffn[l][None, :]

        zp, lap = _inproj(xp, g_pre, w_main, w_ga, w2, ba, cos_p, sin_p, tm=1024, sample=False)
        orgp, retp, glap = _scan(zp, lap, g_ret, g_gla, tab_p, None, nseq=B, seq_rows=S, T=SCAN_TILE,
                                 C=CHUNK, SC=SUBCHUNK, out_dtype=BF16)
        omp = _moba_prompt(zp, nseq=B, seq_rows=S)
        x1p, h2p = _post(xp, orgp, omp, zp, wb, wo, gpost, gpre2, tm=512)
        xp, tailp = _ffn(x1p, h2p, None, wup, wc, bc, wd, gffn, tm=512, sample=False, nseq=B)
        outs["ret_p"].append(retp)
        outs["gla_p"].append(glap)
        outs["k_p"].append(zp[:, Z_MK:Z_MK + W].reshape(B, S, MOBA_HEADS, MOBA_HD))
        outs["v_p"].append(zp[:, Z_MV:Z_MV + W].reshape(B, S, MOBA_HEADS, MOBA_HD))
        outs["conv_p"].append(tailp[:, SUBLANES - (CONV_W - 1):, :])

        zs, las = _inproj(xs, g_pre, w_main, w_ga, w2, ba, cos_s, sin_s, tm=ns, sample=True)
        orgs, rets, glas = _scan(zs, las, g_ret, g_gla, tab_s, (state_ret[l], state_gla[l]), nseq=Bd,
                                 seq_rows=SAMPLE_ROWS, T=SAMPLE_ROWS, C=SAMPLE_ROWS, SC=SAMPLE_ROWS,
                                 out_dtype=F32)
        qh = per_head(zs[:, Z_MQ:Z_MQ + W])
        idx = _moba_topk(qh, kmean, layer=l, nseq=Bd)
        idx_flat = idx[:, half:, :MOBA_HEADS * MOBA_TOPK].reshape(-1)
        oh = _moba_attend(qh, per_head(zs[:, Z_MK:Z_MK + W]), per_head(zs[:, Z_MV:Z_MV + W]), cache_kt, cache_vt,
                          idx_flat, pt_flat, layer=l, nseq=Bd, npages=npages, ntok=T)
        oms = oh.transpose(1, 0, 2).reshape(ns, W)
        x1s, h2s = _post(xs, orgs, oms, zs, wb, wo, gpost, gpre2, tm=ns)
        st = jnp.pad(state_conv[l], ((0, 0), (half - (CONV_W - 1), half), (0, 0))).reshape(ns, 2 * D_FF)
        xs, tails = _ffn(x1s, h2s, st, wup, wc, bc, wd, gffn, tm=ns, sample=True, nseq=Bd)
        zs3 = zs.reshape(Bd, SAMPLE_ROWS, Z_W)
        outs["ret_s"].append(rets)
        outs["gla_s"].append(glas)
        outs["k_s"].append(zs3[:, half:, Z_MK:Z_MK + W].reshape(Bd, T, MOBA_HEADS, MOBA_HD))
        outs["v_s"].append(zs3[:, half:, Z_MV:Z_MV + W].reshape(Bd, T, MOBA_HEADS, MOBA_HD))
        outs["conv_s"].append(tails.reshape(Bd, SAMPLE_ROWS, 2 * D_FF)[:, SAMPLE_ROWS - (CONV_W - 1):, :])

    yp = xp.reshape(B, S, D)
    ys = xs.reshape(Bd, SAMPLE_ROWS, D)[:, half:, :]
    return (yp, ys,
            jnp.stack(outs["ret_p"], axis=0), jnp.stack(outs["ret_s"], axis=0),
            jnp.stack(outs["gla_p"], axis=0), jnp.stack(outs["gla_s"], axis=0),
            jnp.stack(outs["k_p"], axis=2), jnp.stack(outs["v_p"], axis=2),
            jnp.stack(outs["k_s"], axis=2), jnp.stack(outs["v_s"], axis=2),
            jnp.stack(outs["conv_p"], axis=0), jnp.stack(outs["conv_s"], axis=0))
```

```python
import functools

import numpy as np
import jax
import jax.numpy as jnp
from jax import lax
from jax.experimental import pallas as pl
from jax.experimental.pallas import tpu as pltpu

F32 = jnp.float32
BF16 = jnp.bfloat16

D_MODEL = 1024
BRANCH_W = 512
N_BRANCH = 3
RET_HEADS = 4
RET_DK = 64
RET_DV = 128
GLA_HEADS = 4
GLA_DK = 64
GLA_DV = 128
GLA_RANK = 16
GLA_TAU = 16.0
MOBA_HEADS = 8
MOBA_HD = 64
MOBA_BLOCK = 256
MOBA_TOPK = 3
ROPE_THETA = 10000.0
CHUNK = 64
SUBCHUNK = 16
SCAN_TILE = 512
D_FF = 2816
CONV_W = 3
EPS = 1e-6
PAGE_SIZE = 128

LANES = 128
SUBLANES = 8
VMEM_LIMIT = 56 * 1024 * 1024

Z_GL = 0
Z_RQ = 3072
Z_RK = 3328
Z_RV = 3584
Z_RG = 4096
Z_GQ = 4608
Z_GK = 4864
Z_GV = 5120
Z_GG = 5632
Z_MQ = 6144
Z_MK = 6656
Z_MV = 7168
Z_W = 7680
SEG_W = 512
SEG_RQK = Z_RQ // SEG_W
SEG_GQK = Z_GQ // SEG_W
SEG_MQ = Z_MQ // SEG_W
SEG_MK = Z_MK // SEG_W

SAMPLE_ROWS = 8
NEG_BIG = -1e30
MASK_BIG = 2.0 ** 100
LOG2E = 1.4426950408889634
MOBA_GROUP = 8

NT = (((1,), (1,)), ((), ()))
TN = (((0,), (0,)), ((), ()))


def _dot(a, b, dims=None, precision=None):
    if dims is None:
        return jnp.dot(a, b, preferred_element_type=F32, precision=precision)
    return lax.dot_general(a, b, dims, preferred_element_type=F32, precision=precision)


def _rms(x, g):
    return x * lax.rsqrt(jnp.mean(x * x, axis=-1, keepdims=True) + EPS) * g


def _silu(x):
    return x / (1.0 + jnp.exp(-x))


def _sigmoid(x):
    return 1.0 / (1.0 + jnp.exp(-x))


def _cparams(sem):
    return pltpu.CompilerParams(dimension_semantics=sem, vmem_limit_bytes=VMEM_LIMIT)


def _inproj_kernel(x_ref, g_ref, w_ref, wga_ref, w2_ref, ba_ref, cos_ref, sin_ref,
                   z_ref, la_ref, h_scr, *, sample):
    j = pl.program_id(1)

    @pl.when(j == 0)
    def _():
        hb = _rms(x_ref[...], g_ref[...]).astype(BF16)
        h_scr[...] = hb
        ga = _dot(hb, wga_ref[...])
        u = _dot(ga.astype(BF16), w2_ref[...]) + ba_ref[...]
        la = (jnp.minimum(u, 0.0) - jnp.log(1.0 + jnp.exp(-jnp.abs(u)))) * (1.0 / GLA_TAU)
        if sample:
            row = lax.broadcasted_iota(jnp.int32, (la.shape[0], 1), 0)
            la = jnp.where((row % SAMPLE_ROWS) >= SAMPLE_ROWS // 2, la, 0.0)
        la_ref[...] = la

    acc = _dot(h_scr[...], w_ref[...])
    lane = lax.broadcasted_iota(jnp.int32, (1, LANES), 1)
    first = (lane % MOBA_HD) < (MOBA_HD // 2)

    def rope(a):
        sw = jnp.where(first, pltpu.roll(a, LANES - MOBA_HD // 2, 1), pltpu.roll(a, MOBA_HD // 2, 1))
        return a * cos_ref[...] + sw * sin_ref[...]

    def store_rope(scales):
        for c, s in enumerate(scales):
            r = rope(acc[:, c * LANES:(c + 1) * LANES])
            z_ref[:, c * LANES:(c + 1) * LANES] = r if s == 1.0 else r * s

    ks = RET_DK ** -0.5

    @pl.when(j == SEG_RQK)
    def _():
        store_rope((1.0, 1.0, ks, ks))

    @pl.when(j == SEG_GQK)
    def _():
        z_ref[:, :SEG_W // 2] = acc[:, :SEG_W // 2]
        z_ref[:, SEG_W // 2:] = acc[:, SEG_W // 2:] * (GLA_DK ** -0.5)

    @pl.when(j == SEG_MQ)
    def _():
        store_rope((MOBA_HD ** -0.5,) * 4)

    @pl.when(j == SEG_MK)
    def _():
        store_rope((1.0,) * 4)

    plain = (j != SEG_RQK) & (j != SEG_GQK) & (j != SEG_MQ) & (j != SEG_MK)

    @pl.when(plain)
    def _():
        z_ref[...] = acc


def _inproj(x, g, w_main, w_ga, w2, ba, cos, sin, *, tm, sample):
    n = x.shape[0]
    nseg = Z_W // SEG_W
    ntab = cos.shape[0] // tm
    return pl.pallas_call(
        functools.partial(_inproj_kernel, sample=sample),
        grid=(n // tm, nseg),
        in_specs=[
            pl.BlockSpec((tm, D_MODEL), lambda i, j: (i, 0)),
            pl.BlockSpec((1, D_MODEL), lambda i, j: (0, 0)),
            pl.BlockSpec((D_MODEL, SEG_W), lambda i, j: (0, j)),
            pl.BlockSpec((D_MODEL, LANES), lambda i, j: (0, 0)),
            pl.BlockSpec((LANES, GLA_HEADS * GLA_DK), lambda i, j: (0, 0)),
            pl.BlockSpec((1, GLA_HEADS * GLA_DK), lambda i, j: (0, 0)),
            pl.BlockSpec((tm, LANES), lambda i, j: (i % ntab, 0)),
            pl.BlockSpec((tm, LANES), lambda i, j: (i % ntab, 0)),
        ],
        out_specs=[
            pl.BlockSpec((tm, SEG_W), lambda i, j: (i, j)),
            pl.BlockSpec((tm, GLA_HEADS * GLA_DK), lambda i, j: (i, 0)),
        ],
        out_shape=[jax.ShapeDtypeStruct((n, Z_W), F32),
                   jax.ShapeDtypeStruct((n, GLA_HEADS * GLA_DK), F32)],
        scratch_shapes=[pltpu.VMEM((tm, D_MODEL), BF16)],
        compiler_params=_cparams(("parallel", "arbitrary")),
        name="inproj",
    )(x, g, w_main, w_ga, w2, ba, cos, sin)


BNT = (((2,), (2,)), ((0,), (0,)))
BNN = (((2,), (1,)), ((0,), (0,)))


def _decay_tables(c, valid, cg):
    log_g = np.log1p(-np.exp2(-5.0 - np.arange(RET_HEADS, dtype=np.float64)))
    b = log_g[:, None] * np.cumsum(np.asarray(valid, np.float64))[None, :]
    diff = b[:, :, None] - b[:, None, :]
    causal = np.arange(c)[:, None] >= np.arange(c)[None, :]
    dmat = np.where(causal[None], np.exp(np.where(causal[None], diff, 0.0)), 0.0)
    qdec = np.broadcast_to(np.exp(b)[:, :, None], (RET_HEADS, c, LANES))
    kdec = np.broadcast_to(np.exp(b[:, -1:] - b)[:, :, None], (RET_HEADS, c, LANES))
    cdec = tuple(float(v) for v in np.exp(b[:, -1]))
    same = (np.arange(c)[:, None] // cg) == (np.arange(c)[None, :] // cg)
    tri = (causal & same).astype(np.float32)
    return (jnp.asarray(dmat, F32), jnp.asarray(qdec, F32), jnp.asarray(kdec, F32),
            jnp.asarray(tri, F32), cdec)


def _scan_kernel(*refs, T, C, SC, cdec, zero_init):
    if zero_init:
        (zr_ref, zg_ref, la_ref, dmat_ref, qdec_ref, kdec_ref, tri_ref, gr_ref, gg_ref,
         o_ref, sr_ref, sg_ref, Sr, SgT) = refs
    else:
        (zr_ref, zg_ref, la_ref, dmat_ref, qdec_ref, kdec_ref, tri_ref, gr_ref, gg_ref,
         s0r_ref, s0g_ref, o_ref, sr_ref, sg_ref, Sr, SgT) = refs
    t = pl.program_id(1)
    nc, nsub, nsc = T // C, C // SC, T // SC
    HK = RET_HEADS * RET_DK
    lane = lax.broadcasted_iota(jnp.int32, (1, LANES), 1)
    in_e = [lane < RET_DK, lane >= RET_DK]
    rowp = lax.broadcasted_iota(jnp.int32, (LANES, 1), 0)

    @pl.when(t == 0)
    def _():
        if zero_init:
            Sr[...] = jnp.zeros_like(Sr)
            SgT[...] = jnp.zeros_like(SgT)
        else:
            for p in range(RET_HEADS // 2):
                for e in range(2):
                    Sr[p, e * RET_DK:(e + 1) * RET_DK, :] = s0r_ref[0, 2 * p + e]
                SgT[p] = jnp.concatenate([s0g_ref[0, 2 * p].T, s0g_ref[0, 2 * p + 1].T], axis=1)


    for p in range(RET_HEADS // 2):
        qp = zr_ref[:, p * LANES:(p + 1) * LANES]
        kp = zr_ref[:, HK + p * LANES:HK + (p + 1) * LANES]
        kpb = kp.astype(BF16)
        s_old = Sr[p]
        s_new = s_old * jnp.where(rowp < RET_DK, cdec[2 * p], cdec[2 * p + 1])
        for e in range(2):
            h = 2 * p + e
            qh = jnp.where(in_e[e], qp, 0.0)
            vh = zr_ref[:, 2 * HK + h * RET_DV:2 * HK + (h + 1) * RET_DV].astype(BF16)
            att = _dot(qh.astype(BF16), kpb, NT) * dmat_ref[h]
            o = _dot(att.astype(BF16), vh) + _dot((qh * qdec_ref[h]).astype(BF16), s_old.astype(BF16))
            s_new = s_new + _dot((jnp.where(in_e[e], kp, 0.0) * kdec_ref[h]).T.astype(BF16), vh)
            cen = o - jnp.mean(o, axis=-1, keepdims=True)
            y = cen * lax.rsqrt(jnp.mean(cen * cen, axis=-1, keepdims=True) + EPS)
            rg = zr_ref[:, 2 * HK + BRANCH_W + h * RET_DV:2 * HK + BRANCH_W + (h + 1) * RET_DV]
            y = y * gr_ref[:, h * RET_DV:(h + 1) * RET_DV] * _silu(rg)
            o_ref[:, h * RET_DV:(h + 1) * RET_DV] = y.astype(o_ref.dtype)
        Sr[p] = s_new

    q = zg_ref[:, 0:HK]
    k = zg_ref[:, HK:2 * HK]
    b = _dot(tri_ref[...], la_ref[...], precision=lax.Precision.HIGHEST)
    q3, k3, b3 = (a.reshape(nc, C, HK) for a in (q, k, b))
    blast3 = b3[:, C - 1:C, :]
    qe3 = q3 * jnp.exp(b3)
    kd3 = k3 * jnp.exp(blast3 - b3)
    eblast3 = jnp.exp(blast3)
    q4, k4, b4 = (a.reshape(nsc, SC, HK) for a in (q, k, b))
    rsub4 = lax.broadcasted_iota(jnp.int32, (1, SC, 1), 1)
    rowT = lax.broadcasted_iota(jnp.int32, (T, 1), 0)
    sub0 = (rowT % C) - (rowT % SC)
    lane_c = lax.broadcasted_iota(jnp.int32, (1, C), 1)
    rc3 = lax.broadcasted_iota(jnp.int32, (1, C, 1), 1)
    in_e3 = [m.reshape(1, 1, LANES) for m in in_e]

    att = [jnp.zeros((T, C), F32) for _ in range(GLA_HEADS)]
    for jj in range(SC):
        w = q4 * k4[:, jj:jj + 1, :] * jnp.exp(jnp.where(rsub4 >= jj, b4 - b4[:, jj:jj + 1, :], NEG_BIG))
        hit = lane_c == sub0 + jj
        for h in range(GLA_HEADS):
            wp = w[:, :, (h // 2) * LANES:(h // 2 + 1) * LANES]
            col = jnp.sum(jnp.where(in_e3[h % 2], wp, 0.0), axis=-1, keepdims=True).reshape(T, 1)
            att[h] = jnp.where(hit, col, att[h])
    att3 = [a.reshape(nc, C, C) for a in att]
    for i in range(1, nsub):
        bi = b3[:, i * SC - 1:i * SC, :]
        in_i = (rc3 >= i * SC) & (rc3 < (i + 1) * SC)
        qi = q3 * jnp.exp(jnp.where(in_i, b3 - bi, NEG_BIG))
        ki = (k3 * jnp.exp(jnp.where(rc3 < i * SC, bi - b3, NEG_BIG))).astype(BF16)
        for h in range(GLA_HEADS):
            ps = slice((h // 2) * LANES, (h // 2 + 1) * LANES)
            qih = jnp.where(in_e3[h % 2], qi[:, :, ps], 0.0).astype(BF16)
            att3[h] = att3[h] + lax.dot_general(qih, ki[:, :, ps], BNT, preferred_element_type=F32)
    for p in range(GLA_HEADS // 2):
        ps = slice(p * LANES, (p + 1) * LANES)
        o_intra, qeb, ut3 = [], [], None
        for e in range(2):
            h = 2 * p + e
            v3 = zg_ref[:, 2 * HK + h * GLA_DV:2 * HK + (h + 1) * GLA_DV].reshape(nc, C, GLA_DV)
            o_intra.append(lax.dot_general(att3[h].astype(BF16), v3.astype(BF16), BNN, preferred_element_type=F32))
            qeb.append(jnp.where(in_e3[e], qe3[:, :, ps], 0.0).astype(BF16))
            u = lax.dot_general(jnp.swapaxes(v3, 1, 2).astype(BF16),
                                jnp.where(in_e3[e], kd3[:, :, ps], 0.0).astype(BF16), BNN,
                                preferred_element_type=F32)
            ut3 = u if ut3 is None else ut3 + u
        st = SgT[p]
        outs = [[], []]
        for c in range(nc):
            stb = st.astype(BF16)
            for e in range(2):
                outs[e].append(o_intra[e][c] + _dot(qeb[e][c], stb, NT))
            st = st * eblast3[c][:, ps] + ut3[c]
        SgT[p] = st
        for e in range(2):
            h = 2 * p + e
            o = outs[e][0] if nc == 1 else jnp.concatenate(outs[e], axis=0)
            gg = zg_ref[:, 2 * HK + BRANCH_W + h * GLA_DV:2 * HK + BRANCH_W + (h + 1) * GLA_DV]
            y = _rms(o, gg_ref[:, h * GLA_DV:(h + 1) * GLA_DV]) * _silu(gg)
            o_ref[:, BRANCH_W + h * GLA_DV:BRANCH_W + (h + 1) * GLA_DV] = y.astype(o_ref.dtype)

    @pl.when(t == pl.num_programs(1) - 1)
    def _():
        for p in range(RET_HEADS // 2):
            for e in range(2):
                sr_ref[0, 2 * p + e] = Sr[p, e * RET_DK:(e + 1) * RET_DK, :]
                sg_ref[0, 2 * p + e] = SgT[p][:, e * GLA_DK:(e + 1) * GLA_DK].T


def _scan(z, la, g_ret, g_gla, tables, s0, *, nseq, seq_rows, T, C, SC, out_dtype):
    dmat, qdec, kdec, tri, cdec = tables
    nT = seq_rows // T
    zero_init = s0 is None
    zw = 2 * RET_HEADS * RET_DK + 2 * BRANCH_W
    const3 = lambda s, t: (0, 0, 0)
    const2 = lambda s, t: (0, 0)
    in_specs = [
        pl.BlockSpec((T, zw), lambda s, t: (s * nT + t, Z_RQ // zw)),
        pl.BlockSpec((T, zw), lambda s, t: (s * nT + t, Z_GQ // zw)),
        pl.BlockSpec((T, GLA_HEADS * GLA_DK), lambda s, t: (s * nT + t, 0)),
        pl.BlockSpec(dmat.shape, const3),
        pl.BlockSpec(qdec.shape, const3),
        pl.BlockSpec(kdec.shape, const3),
        pl.BlockSpec(tri.shape, const2),
        pl.BlockSpec((1, BRANCH_W), const2),
        pl.BlockSpec((1, BRANCH_W), const2),
    ]
    args = [z, z, la, dmat, qdec, kdec, tri, g_ret, g_gla]
    st_spec = pl.BlockSpec((1, RET_HEADS, RET_DK, RET_DV), lambda s, t: (s, 0, 0, 0))
    if not zero_init:
        in_specs += [st_spec, st_spec]
        args += [s0[0], s0[1]]
    st_shape = jax.ShapeDtypeStruct((nseq, RET_HEADS, RET_DK, RET_DV), F32)
    return pl.pallas_call(
        functools.partial(_scan_kernel, T=T, C=C, SC=SC, cdec=cdec, zero_init=zero_init),
        grid=(nseq, nT),
        in_specs=in_specs,
        out_specs=[pl.BlockSpec((T, 2 * BRANCH_W), lambda s, t: (s * nT + t, 0)), st_spec, st_spec],
        out_shape=[jax.ShapeDtypeStruct((nseq * seq_rows, 2 * BRANCH_W), out_dtype), st_shape, st_shape],
        scratch_shapes=[pltpu.VMEM((RET_HEADS // 2, LANES, RET_DV), F32),
                        pltpu.VMEM((GLA_HEADS // 2, GLA_DV, LANES), F32)],
        compiler_params=_cparams(("parallel", "arbitrary")),
        name="scan",
    )(*args)


PAGES_PER_STEP = 16
PAGES_PER_BLOCK = MOBA_BLOCK // PAGE_SIZE
VT_ROWS = MOBA_HD + 16


def _kmean_step(page_refs, km_ref):
    for blk in range(PAGES_PER_STEP // PAGES_PER_BLOCK):
        tot = page_refs[blk * PAGES_PER_BLOCK][...]
        for p in range(1, PAGES_PER_BLOCK):
            tot = tot + page_refs[blk * PAGES_PER_BLOCK + p][...]
        km_ref[0, 0, :, :, :, blk:blk + 1] = jnp.sum(tot, axis=-1, keepdims=True) * (1.0 / MOBA_BLOCK)


def _moba_prompt_kernel(*refs, nblk, with_kmean):
    if with_kmean:
        q_ref, k_ref, v_ref = refs[1:4]
        o_ref, kmo_ref, km_scr, ka_scr, vt_scr = refs[4 + PAGES_PER_STEP:]
        _kmean_step(refs[4:4 + PAGES_PER_STEP], kmo_ref)
    else:
        q_ref, k_ref, v_ref, o_ref, km_scr, ka_scr, vt_scr = refs
    t = pl.program_id(2)
    BLK = MOBA_BLOCK
    G = MOBA_GROUP
    lane = lax.broadcasted_iota(jnp.int32, (1, LANES), 1)
    in_e = [lane < MOBA_HD, lane >= MOBA_HD]
    off = [MOBA_HD, 0]

    @pl.when(t == 0)
    def _():
        for blk in range(nblk):
            km_scr[blk:blk + 1, :] = jnp.mean(k_ref[blk * BLK:(blk + 1) * BLK, :], axis=0, keepdims=True)

        def prep(blk, c):
            r0 = pl.multiple_of(blk * BLK, BLK)
            kj = k_ref[pl.ds(r0, BLK), :]
            for e in range(2):
                ka_scr[e, pl.ds(r0, BLK), :] = jnp.where(
                    in_e[e], kj, jnp.where(lane == off[e] + blk, 1.0, 0.0)).astype(BF16)
            vt = v_ref[pl.ds(r0, BLK), :].T.astype(BF16)
            for e in range(2):
                vt_scr[e, 0:MOBA_HD, pl.ds(r0, BLK)] = vt[e * MOBA_HD:(e + 1) * MOBA_HD, :]
                vt_scr[e, MOBA_HD:VT_ROWS, pl.ds(r0, BLK)] = jnp.ones((VT_ROWS - MOBA_HD, BLK), BF16)
            return c
        lax.fori_loop(0, nblk, prep, 0)

    q = q_ref[...]
    q2t = (q * LOG2E).T
    t0 = pl.multiple_of(t * BLK, BLK)
    blkrow = lax.broadcasted_iota(jnp.int32, (nblk, 1), 0)
    rowk = lax.broadcasted_iota(jnp.int32, (BLK, 1), 0)
    laneq = lax.broadcasted_iota(jnp.int32, (1, BLK), 1)
    zpad = jnp.zeros((MOBA_HD - nblk, BLK), F32)
    zhead = jnp.zeros((MOBA_HD, BLK), F32)
    waug, init = [], []
    for e in range(2):
        g = _dot(km_scr[...], jnp.where(in_e[e], q, 0.0), NT, precision=lax.Precision.HIGHEST)
        g = jnp.where(blkrow < t, g, -jnp.inf)
        sel = jnp.zeros(g.shape, jnp.bool_)
        for _ in range(MOBA_TOPK):
            m = jnp.max(g, axis=0, keepdims=True)
            idx = jnp.min(jnp.where(g == m, blkrow, nblk), axis=0, keepdims=True)
            pick = (blkrow == idx) & (m > -jnp.inf)
            sel = sel | pick
            g = jnp.where(pick, -jnp.inf, g)
        pen = jnp.where(sel, 0.0, -MASK_BIG)
        qe = q2t[e * MOBA_HD:(e + 1) * MOBA_HD, :]
        if e == 0:
            waug.append(jnp.concatenate([qe, pen, zpad], axis=0).astype(BF16))
            wown = jnp.concatenate([qe, zhead], axis=0).astype(BF16)
        else:
            waug.append(jnp.concatenate([pen, zpad, qe], axis=0).astype(BF16))
            wown = jnp.concatenate([zhead, qe], axis=0).astype(BF16)
        s = _dot(ka_scr[e, pl.ds(t0, BLK), :], wown)
        s = jnp.where(rowk <= laneq, s, NEG_BIG)
        m0 = jnp.max(s, axis=0, keepdims=True)
        init += [m0, _dot(vt_scr[e, :, pl.ds(t0, BLK)], jnp.exp2(s - m0).astype(BF16))]

    def body(gi, carry):
        r0 = pl.multiple_of(gi * (G * BLK), G * BLK)
        st = list(carry)
        ss = [[_dot(ka_scr[e, pl.ds(r0 + j * BLK, BLK), :], waug[e]) for j in range(G)] for e in range(2)]
        for j in range(G):
            for e in range(2):
                m, acc = st[2 * e:2 * e + 2]
                s = ss[e][j]
                mn = jnp.maximum(m, jnp.max(s, axis=0, keepdims=True))
                p = jnp.exp2(s - mn).astype(BF16)
                st[2 * e:2 * e + 2] = [mn, jnp.exp2(m - mn) * acc + _dot(vt_scr[e, :, pl.ds(r0 + j * BLK, BLK)], p)]
        return tuple(st)

    res = lax.fori_loop(0, (t + G - 1) // G, body, tuple(init))
    outs = [res[2 * e + 1][0:MOBA_HD] / res[2 * e + 1][MOBA_HD:MOBA_HD + 1] for e in range(2)]
    o_ref[...] = jnp.concatenate(outs, axis=0).T.astype(o_ref.dtype)


def _moba_prompt(z, *, nseq, seq_rows, kmean_src=None):
    nblk = seq_rows // MOBA_BLOCK
    npair = MOBA_HEADS // 2
    assert nblk % MOBA_GROUP == 0 and nblk <= MOBA_HD and nblk % SUBLANES == 0
    with_kmean = kmean_src is not None
    in_specs = [
        pl.BlockSpec((MOBA_BLOCK, LANES), lambda b, hp, t, *_: (b * nblk + t, Z_MQ // LANES + hp)),
        pl.BlockSpec((seq_rows, LANES), lambda b, hp, t, *_: (b, Z_MK // LANES + hp)),
        pl.BlockSpec((seq_rows, LANES), lambda b, hp, t, *_: (b, Z_MV // LANES + hp)),
    ]
    out_specs = [pl.BlockSpec((MOBA_BLOCK, LANES), lambda b, hp, t, *_: (b * nblk + t, hp))]
    out_shape = [jax.ShapeDtypeStruct((nseq * seq_rows, BRANCH_W), BF16)]
    args = [z, z, z]
    if with_kmean:
        cache_kt, pt_flat, nseq_s, npages = kmean_src
        depth = cache_kt.shape[1]
        bps = PAGES_PER_STEP // PAGES_PER_BLOCK
        nstep = npages // PAGES_PER_STEP
        assert nseq * npair * nblk == nseq_s * nstep

        def page_spec(i):
            def imap(b, hp, t, pt):
                return (pt[((b * npair + hp) * nblk + t) * PAGES_PER_STEP + i], 0, 0, 0, 0)
            return pl.BlockSpec((None, depth, MOBA_HEADS, MOBA_HD, PAGE_SIZE), imap)

        def km_map(b, hp, t, pt):
            s = (b * npair + hp) * nblk + t
            return (s // nstep, s % nstep, 0, 0, 0, 0)

        in_specs += [page_spec(i) for i in range(PAGES_PER_STEP)]
        args = [pt_flat] + args + [cache_kt] * PAGES_PER_STEP
        out_specs.append(pl.BlockSpec((1, 1, depth, MOBA_HEADS, MOBA_HD, bps), km_map))
        out_shape.append(jax.ShapeDtypeStruct((nseq_s, nstep, depth, MOBA_HEADS, MOBA_HD, bps), F32))
    grid_spec = pltpu.PrefetchScalarGridSpec(
        num_scalar_prefetch=1 if with_kmean else 0,
        grid=(nseq, npair, nblk),
        in_specs=in_specs,
        out_specs=out_specs,
        scratch_shapes=[pltpu.VMEM((nblk, LANES), F32),
                        pltpu.VMEM((2, seq_rows, LANES), BF16),
                        pltpu.VMEM((2, VT_ROWS, seq_rows), BF16)],
    )
    res = pl.pallas_call(
        functools.partial(_moba_prompt_kernel, nblk=nblk, with_kmean=with_kmean),
        grid_spec=grid_spec,
        out_shape=out_shape,
        compiler_params=_cparams(("parallel", "parallel", "arbitrary")),
        name="moba_prompt",
    )(*args)
    if not with_kmean:
        return res[0], None
    km = res[1].transpose(0, 2, 3, 4, 1, 5).reshape(nseq_s, depth, MOBA_HEADS, MOBA_HD, nstep * bps)
    return res[0], km


def _moba_topk_kernel(q_ref, km_ref, idx_ref):
    nfull = km_ref.shape[2]
    lane = lax.broadcasted_iota(jnp.int32, (1, LANES), 1)
    lane_b = lax.broadcasted_iota(jnp.int32, (1, nfull), 1)
    out = jnp.zeros((SAMPLE_ROWS, LANES), jnp.int32)
    for h in range(MOBA_HEADS):
        gsc = _dot(q_ref[h], km_ref[h], precision=lax.Precision.HIGHEST)
        for r in range(MOBA_TOPK):
            m = jnp.max(gsc, axis=-1, keepdims=True)
            idx = jnp.min(jnp.where(gsc == m, lane_b, nfull), axis=-1, keepdims=True)
            out = jnp.where(lane == h * MOBA_TOPK + r, idx, out)
            gsc = jnp.where(lane_b == idx, -jnp.inf, gsc)
    idx_ref[0] = out


def _moba_topk(qh, kmean, *, layer, nseq):
    nfull = kmean.shape[4]
    return pl.pallas_call(
        _moba_topk_kernel,
        grid=(nseq,),
        in_specs=[
            pl.BlockSpec((MOBA_HEADS, SAMPLE_ROWS, MOBA_HD), lambda b: (0, b, 0)),
            pl.BlockSpec((None, None, MOBA_HEADS, MOBA_HD, nfull), lambda b: (b, layer, 0, 0, 0)),
        ],
        out_specs=pl.BlockSpec((1, SAMPLE_ROWS, LANES), lambda b: (b, 0, 0)),
        out_shape=jax.ShapeDtypeStruct((nseq, SAMPLE_ROWS, LANES), jnp.int32),
        compiler_params=_cparams(("parallel",)),
        name="moba_topk",
    )(qh, kmean)


def _moba_attend_kernel(idx_ref, pt_ref, q_ref, kn_ref, vn_ref, ck_hbm, cv_hbm, o_ref, kbuf, vbuf, sem,
                        *, layer, npages, ntok):
    b = pl.program_id(0)
    h = pl.program_id(1)
    nh = pl.num_programs(1)
    step = b * nh + h
    nstep = pl.num_programs(0) * nh
    slot = step % 2

    def copies(bb, hh, sl):
        out = []
        for t in range(ntok):
            for r in range(MOBA_TOPK):
                blk = idx_ref[((bb * ntok + t) * MOBA_HEADS + hh) * MOBA_TOPK + r]
                for p in range(PAGES_PER_BLOCK):
                    page = pt_ref[bb * npages + blk * PAGES_PER_BLOCK + p]
                    i = (t * MOBA_TOPK + r) * PAGES_PER_BLOCK + p
                    out.append(pltpu.make_async_copy(ck_hbm.at[page, layer, hh], kbuf.at[sl, i], sem.at[0, sl]))
                    out.append(pltpu.make_async_copy(cv_hbm.at[page, layer, hh], vbuf.at[sl, i], sem.at[1, sl]))
        return out

    @pl.when(step == 0)
    def _():
        for c in copies(b, h, slot):
            c.start()

    @pl.when(step + 1 < nstep)
    def _():
        nxt = step + 1
        for c in copies(nxt // nh, nxt % nh, 1 - slot):
            c.start()

    for c in copies(b, h, slot):
        c.wait()

    half = SAMPLE_ROWS // 2
    rowi = lax.broadcasted_iota(jnp.int32, (SAMPLE_ROWS, 1), 0)
    coln = lax.broadcasted_iota(jnp.int32, (1, SAMPLE_ROWS), 1)
    qb = q_ref[...].astype(BF16)
    vn = vn_ref[...].astype(BF16)
    s_new_all = _dot(qb, kn_ref[...].astype(BF16), NT)
    out = jnp.zeros((SAMPLE_ROWS, MOBA_HD), F32)
    for t in range(ntok):
        qrow = half + t
        s_new = jnp.where((coln >= half) & (coln <= qrow), s_new_all, NEG_BIG)
        ss, vs = [], []
        for i in range(t * MOBA_TOPK * PAGES_PER_BLOCK, (t + 1) * MOBA_TOPK * PAGES_PER_BLOCK):
            ss.append(_dot(qb, kbuf[slot, i].astype(BF16)))
            vs.append(vbuf[slot, i].astype(BF16))
        m = jnp.max(s_new, axis=-1, keepdims=True)
        for s in ss:
            m = jnp.maximum(m, jnp.max(s, axis=-1, keepdims=True))
        p_new = jnp.exp(s_new - m)
        l = jnp.sum(p_new, axis=-1, keepdims=True)
        o = _dot(p_new.astype(BF16), vn)
        for s, vb in zip(ss, vs):
            p = jnp.exp(s - m)
            l = l + jnp.sum(p, axis=-1, keepdims=True)
            o = o + _dot(p.astype(BF16), vb, NT)
        out = jnp.where(rowi == qrow, o / l, out)
    o_ref[...] = out


def _moba_attend(qh, knh, vnh, cache_k, cache_v, idx_flat, pt_flat, *, layer, nseq, npages, ntok):
    nbuf = ntok * MOBA_TOPK * PAGES_PER_BLOCK
    head_spec = pl.BlockSpec((None, SAMPLE_ROWS, MOBA_HD), lambda b, h, idx, pt: (h, b, 0))
    grid_spec = pltpu.PrefetchScalarGridSpec(
        num_scalar_prefetch=2,
        grid=(nseq, MOBA_HEADS),
        in_specs=[head_spec, head_spec, head_spec,
                  pl.BlockSpec(memory_space=pl.ANY), pl.BlockSpec(memory_space=pl.ANY)],
        out_specs=head_spec,
        scratch_shapes=[pltpu.VMEM((2, nbuf, MOBA_HD, PAGE_SIZE), F32),
                        pltpu.VMEM((2, nbuf, MOBA_HD, PAGE_SIZE), F32),
                        pltpu.SemaphoreType.DMA((2, 2))],
    )
    return pl.pallas_call(
        functools.partial(_moba_attend_kernel, layer=layer, npages=npages, ntok=ntok),
        grid_spec=grid_spec,
        out_shape=jax.ShapeDtypeStruct((MOBA_HEADS, nseq * SAMPLE_ROWS, MOBA_HD), F32),
        compiler_params=_cparams(("arbitrary", "arbitrary")),
        name="moba_attend",
    )(idx_flat, pt_flat, qh, knh, vnh, cache_k, cache_v)


def _post_kernel(x_ref, org_ref, om_ref, gl_ref, wb_ref, wo_ref, gpost_ref, gpre_ref, x1_ref, h2_ref):
    merged = None
    for n in range(N_BRANCH):
        if n < 2:
            br = org_ref[:, n * BRANCH_W:(n + 1) * BRANCH_W]
        else:
            br = om_ref[...]
        proj = _dot(br.astype(BF16), wb_ref[n])
        term = _sigmoid(gl_ref[:, n * D_MODEL:(n + 1) * D_MODEL]) * proj
        merged = term if merged is None else merged + term
    y = _dot(merged.astype(BF16), wo_ref[...])
    x1 = x_ref[...] + _rms(y, gpost_ref[...])
    x1_ref[...] = x1
    h2_ref[...] = _rms(x1, gpre_ref[...]).astype(BF16)


def _post(x, org, om, z, wb, wo, gpost, gpre, *, tm):
    n = x.shape[0]
    c2 = lambda i: (0, 0)
    return pl.pallas_call(
        _post_kernel,
        grid=(n // tm,),
        in_specs=[
            pl.BlockSpec((tm, D_MODEL), lambda i: (i, 0)),
            pl.BlockSpec((tm, 2 * BRANCH_W), lambda i: (i, 0)),
            pl.BlockSpec((tm, BRANCH_W), lambda i: (i, 0)),
            pl.BlockSpec((tm, N_BRANCH * D_MODEL), lambda i: (i, 0)),
            pl.BlockSpec((N_BRANCH, BRANCH_W, D_MODEL), lambda i: (0, 0, 0)),
            pl.BlockSpec((D_MODEL, D_MODEL), c2),
            pl.BlockSpec((1, D_MODEL), c2),
            pl.BlockSpec((1, D_MODEL), c2),
        ],
        out_specs=[pl.BlockSpec((tm, D_MODEL), lambda i: (i, 0)),
                   pl.BlockSpec((tm, D_MODEL), lambda i: (i, 0))],
        out_shape=[jax.ShapeDtypeStruct((n, D_MODEL), F32), jax.ShapeDtypeStruct((n, D_MODEL), BF16)],
        compiler_params=_cparams(("parallel",)),
        name="post",
    )(x, org, om, z, wb, wo, gpost, gpre)


FF_CHUNK = 256


def _ffn_kernel(*refs, sample, tiles_per_seq, tail_rows):
    if sample:
        (x1_ref, h2_ref, st_ref, wup_ref, wc_ref, bc_ref, wd_ref, g_ref, x2_ref, tail_ref, acc_scr) = refs
    else:
        (x1_ref, h2_ref, halo_ref, wup_ref, wc_ref, bc_ref, wd_ref, g_ref, x2_ref, tail_ref, acc_scr) = refs
    i = pl.program_id(0)
    tm = x1_ref.shape[0]
    h2 = h2_ref[...]
    row = lax.broadcasted_iota(jnp.int32, (tm, 1), 0)
    row8 = lax.broadcasted_iota(jnp.int32, (SUBLANES, 1), 0)
    if not sample:
        halo = jnp.where((i % tiles_per_seq) == 0, jnp.zeros_like(halo_ref[...]), halo_ref[...])

    def shifted(u, uh, sh):
        us = pltpu.roll(u, sh, 0)
        if uh is None:
            return us
        hs = pltpu.roll(uh, sh, 0)
        head = jnp.where(row8 < sh, hs, us[:SUBLANES])
        return jnp.concatenate([head, us[SUBLANES:]], axis=0)

    for c in range(D_FF // FF_CHUNK):
        halves = []
        for half in range(2):
            c0 = half * D_FF + c * FF_CHUNK
            w = wup_ref[:, c0:c0 + FF_CHUNK]
            u = _dot(h2, w)
            if sample:
                inj = ((row % SAMPLE_ROWS) >= SAMPLE_ROWS // 2 - (CONV_W - 1)) & ((row % SAMPLE_ROWS) < SAMPLE_ROWS // 2)
                u = jnp.where(inj, st_ref[:, c0:c0 + FF_CHUNK], u)
                uh = None
            else:
                uh = _dot(halo, w)
            tail_ref[0, :, c0:c0 + FF_CHUNK] = u[tm - tail_rows:]
            conv = bc_ref[:, c0:c0 + FF_CHUNK] + u * wc_ref[CONV_W - 1:CONV_W, c0:c0 + FF_CHUNK]
            for sh in range(1, CONV_W):
                conv = conv + shifted(u, uh, sh) * wc_ref[CONV_W - 1 - sh:CONV_W - sh, c0:c0 + FF_CHUNK]
            halves.append(conv)
        a, b = halves
        gelu = 0.5 * a * (1.0 + jnp.tanh(0.7978845608028654 * (a + 0.044715 * a * a * a)))
        part = _dot((gelu * b).astype(BF16), wd_ref[c * FF_CHUNK:(c + 1) * FF_CHUNK, :])
        if c == 0:
            acc_scr[...] = part
        else:
            acc_scr[...] += part
    x2 = x1_ref[...] + _rms(acc_scr[...], g_ref[...])
    if sample:
        x2 = jnp.where((row % SAMPLE_ROWS) >= SAMPLE_ROWS // 2, x2, 0.0)
    x2_ref[...] = x2


def _ffn(x1, h2, extra, wup, wc, bc, wd, g, *, tm, sample, nseq):
    n = x1.shape[0]
    ntile = n // tm
    tiles_per_seq = ntile // nseq if not sample else 1
    tail_rows = tm if sample else SUBLANES
    ntail = ntile if sample else nseq
    c2 = lambda i: (0, 0)
    if sample:
        extra_spec = pl.BlockSpec((tm, 2 * D_FF), lambda i: (i, 0))
        extra_arg = extra
    else:
        hb = tm // SUBLANES
        extra_spec = pl.BlockSpec((SUBLANES, D_MODEL), lambda i: (jnp.maximum(i * hb - 1, 0), 0))
        extra_arg = h2
    return pl.pallas_call(
        functools.partial(_ffn_kernel, sample=sample, tiles_per_seq=tiles_per_seq, tail_rows=tail_rows),
        grid=(ntile,),
        in_specs=[
            pl.BlockSpec((tm, D_MODEL), lambda i: (i, 0)),
            pl.BlockSpec((tm, D_MODEL), lambda i: (i, 0)),
            extra_spec,
            pl.BlockSpec((D_MODEL, 2 * D_FF), c2, pipeline_mode=pl.Buffered(1)),
            pl.BlockSpec((CONV_W, 2 * D_FF), c2),
            pl.BlockSpec((1, 2 * D_FF), c2),
            pl.BlockSpec((D_FF, D_MODEL), c2, pipeline_mode=pl.Buffered(1)),
            pl.BlockSpec((1, D_MODEL), c2),
        ],
        out_specs=[pl.BlockSpec((tm, D_MODEL), lambda i: (i, 0)),
                   pl.BlockSpec((1, tail_rows, 2 * D_FF), lambda i: (i // tiles_per_seq, 0, 0))],
        out_shape=[jax.ShapeDtypeStruct((n, D_MODEL), F32),
                   jax.ShapeDtypeStruct((ntail, tail_rows, 2 * D_FF), F32)],
        scratch_shapes=[pltpu.VMEM((tm, D_MODEL), F32)],
        compiler_params=_cparams(("arbitrary",)),
        name="ffn",
    )(x1, h2, extra_arg, wup, wc, bc, wd, g)


def _rope_tables(pos):
    half = MOBA_HD // 2
    freqs = jnp.power(ROPE_THETA, -jnp.arange(half, dtype=F32) / half)
    ang = pos.astype(F32)[:, None] * freqs[None, :]
    cos, sin = jnp.cos(ang), jnp.sin(ang)
    return (jnp.concatenate([cos, cos, cos, cos], axis=1),
            jnp.concatenate([-sin, sin, -sin, sin], axis=1))


def kernel(x_prompt, x_sample, cache_k, cache_v, page_table, state_ret, state_gla, state_conv, g_pre_mix, w_in, w_gla_a2, b_gla_a, g_ret_norm, g_gla_norm, w_branch, w_out, g_post_mix, g_pre_ffn, w_up, w_conv, b_conv, w_down, g_post_ffn):
    B, S, D = x_prompt.shape
    Bd, T, _ = x_sample.shape
    depth = w_in.shape[0]
    npages = page_table.shape[1]
    past_len = npages * PAGE_SIZE
    half = SAMPLE_ROWS // 2
    assert D == D_MODEL and T == half and S % 2048 == 0
    assert past_len % MOBA_BLOCK == 0 and npages % PAGES_PER_STEP == 0
    assert past_len // MOBA_BLOCK >= MOBA_TOPK
    n_pool = cache_k.shape[0]
    W = MOBA_HEADS * MOBA_HD

    xp = x_prompt.reshape(B * S, D)
    xs = jnp.pad(x_sample, ((0, 0), (half, 0), (0, 0))).reshape(Bd * SAMPLE_ROWS, D)
    ns = Bd * SAMPLE_ROWS

    cos_p, sin_p = _rope_tables(jnp.arange(S, dtype=jnp.int32))
    pos_s = past_len + jnp.tile(jnp.arange(SAMPLE_ROWS, dtype=jnp.int32) - half, Bd)
    cos_s, sin_s = _rope_tables(pos_s)

    tab_p = _decay_tables(SCAN_TILE, np.ones(SCAN_TILE), CHUNK)
    tab_s = _decay_tables(SAMPLE_ROWS, np.concatenate([np.zeros(half), np.ones(half)]), SAMPLE_ROWS)

    pt_flat = page_table.reshape(-1).astype(jnp.int32)
    cache_kt = cache_k.transpose(0, 2, 3, 4, 1)
    cache_vt = cache_v.transpose(0, 2, 3, 4, 1)
    kmean = None

    def per_head(a):
        return a.reshape(a.shape[0], MOBA_HEADS, MOBA_HD).transpose(1, 0, 2)

    outs = {k: [] for k in ("ret_p", "ret_s", "gla_p", "gla_s", "k_p", "v_p", "k_s", "v_s", "conv_p", "conv_s")}
    for l in range(depth):
        w = w_in[l]
        o_ga = 2 * RET_HEADS * RET_DK + 2 * BRANCH_W + 2 * GLA_HEADS * GLA_DK + 2 * BRANCH_W
        o_m = o_ga + GLA_RANK
        o_gl = o_m + 3 * BRANCH_W
        w_main = jnp.concatenate([w[:, o_gl:], w[:, :o_ga], w[:, o_m:o_gl]], axis=1).astype(BF16)
        w_ga = jnp.pad(w[:, o_ga:o_m], ((0, 0), (0, LANES - GLA_RANK))).astype(BF16)
        w2 = jnp.pad(w_gla_a2[l], ((0, LANES - GLA_RANK), (0, 0))).astype(BF16)
        ba = b_gla_a[l][None, :]
        g_pre = g_pre_mix[l][None, :]
        g_ret = g_ret_norm[l][None, :]
        g_gla = g_gla_norm[l][None, :]
        wb = w_branch[l].astype(BF16)
        wo = w_out[l].astype(BF16)
        gpost = g_post_mix[l][None, :]
        gpre2 = g_pre_ffn[l][None, :]
        wup = w_up[l].astype(BF16)
        wd = w_down[l].astype(BF16)
        wc = w_conv[l]
        bc = b_conv[l][None, :]
        gffn = g_post_ffn[l][None, :]

        zp, lap = _inproj(xp, g_pre, w_main, w_ga, w2, ba, cos_p, sin_p, tm=2048, sample=False)
        orgp, retp, glap = _scan(zp, lap, g_ret, g_gla, tab_p, None, nseq=B, seq_rows=S, T=SCAN_TILE,
                                 C=CHUNK, SC=SUBCHUNK, out_dtype=BF16)
        if kmean is None:
            omp, kmean = _moba_prompt(zp, nseq=B, seq_rows=S, kmean_src=(cache_kt, pt_flat, Bd, npages))
        else:
            omp, _ = _moba_prompt(zp, nseq=B, seq_rows=S)
        x1p, h2p = _post(xp, orgp, omp, zp, wb, wo, gpost, gpre2, tm=512)
        xp, tailp = _ffn(x1p, h2p, None, wup, wc, bc, wd, gffn, tm=512, sample=False, nseq=B)
        outs["ret_p"].append(retp)
        outs["gla_p"].append(glap)
        outs["k_p"].append(zp[:, Z_MK:Z_MK + W].reshape(B, S, MOBA_HEADS, MOBA_HD))
        outs["v_p"].append(zp[:, Z_MV:Z_MV + W].reshape(B, S, MOBA_HEADS, MOBA_HD))
        outs["conv_p"].append(tailp[:, SUBLANES - (CONV_W - 1):, :])

        zs, las = _inproj(xs, g_pre, w_main, w_ga, w2, ba, cos_s, sin_s, tm=ns, sample=True)
        orgs, rets, glas = _scan(zs, las, g_ret, g_gla, tab_s, (state_ret[l], state_gla[l]), nseq=Bd,
                                 seq_rows=SAMPLE_ROWS, T=SAMPLE_ROWS, C=SAMPLE_ROWS, SC=SAMPLE_ROWS,
                                 out_dtype=F32)
        qh = per_head(zs[:, Z_MQ:Z_MQ + W])
        idx = _moba_topk(qh, kmean, layer=l, nseq=Bd)
        idx_flat = idx[:, half:, :MOBA_HEADS * MOBA_TOPK].reshape(-1)
        oh = _moba_attend(qh, per_head(zs[:, Z_MK:Z_MK + W]), per_head(zs[:, Z_MV:Z_MV + W]), cache_kt, cache_vt,
                          idx_flat, pt_flat, layer=l, nseq=Bd, npages=npages, ntok=T)
        oms = oh.transpose(1, 0, 2).reshape(ns, W)
        x1s, h2s = _post(xs, orgs, oms, zs, wb, wo, gpost, gpre2, tm=ns)
        st = jnp.pad(state_conv[l], ((0, 0), (half - (CONV_W - 1), half), (0, 0))).reshape(ns, 2 * D_FF)
        xs, tails = _ffn(x1s, h2s, st, wup, wc, bc, wd, gffn, tm=ns, sample=True, nseq=Bd)
        zs3 = zs.reshape(Bd, SAMPLE_ROWS, Z_W)
        outs["ret_s"].append(rets)
        outs["gla_s"].append(glas)
        outs["k_s"].append(zs3[:, half:, Z_MK:Z_MK + W].reshape(Bd, T, MOBA_HEADS, MOBA_HD))
        outs["v_s"].append(zs3[:, half:, Z_MV:Z_MV + W].reshape(Bd, T, MOBA_HEADS, MOBA_HD))
        outs["conv_s"].append(tails.reshape(Bd, SAMPLE_ROWS, 2 * D_FF)[:, SAMPLE_ROWS - (CONV_W - 1):, :])

    yp = xp.reshape(B, S, D)
    ys = xs.reshape(Bd, SAMPLE_ROWS, D)[:, half:, :]
    return (yp, ys,
            jnp.stack(outs["ret_p"], axis=0), jnp.stack(outs["ret_s"], axis=0),
            jnp.stack(outs["gla_p"], axis=0), jnp.stack(outs["gla_s"], axis=0),
            jnp.stack(outs["k_p"], axis=2), jnp.stack(outs["v_p"], axis=2),
            jnp.stack(outs["k_s"], axis=2), jnp.stack(outs["v_s"], axis=2),
            jnp.stack(outs["conv_p"], axis=0), jnp.stack(outs["conv_s"], axis=0))
```

```python
import functools

import numpy as np
import jax
import jax.numpy as jnp
from jax import lax
from jax.experimental import pallas as pl
from jax.experimental.pallas import tpu as pltpu

F32 = jnp.float32
BF16 = jnp.bfloat16

D_MODEL = 1024
BRANCH_W = 512
N_BRANCH = 3
RET_HEADS = 4
RET_DK = 64
RET_DV = 128
GLA_HEADS = 4
GLA_DK = 64
GLA_DV = 128
GLA_RANK = 16
GLA_TAU = 16.0
MOBA_HEADS = 8
MOBA_HD = 64
MOBA_BLOCK = 256
MOBA_TOPK = 3
ROPE_THETA = 10000.0
CHUNK = 64
SUBCHUNK = 16
SCAN_TILE = 512
D_FF = 2816
CONV_W = 3
EPS = 1e-6
PAGE_SIZE = 128

LANES = 128
SUBLANES = 8
VMEM_LIMIT = 56 * 1024 * 1024

Z_GL = 0
Z_RQ = 3072
Z_RK = 3328
Z_RV = 3584
Z_RG = 4096
Z_GQ = 4608
Z_GK = 4864
Z_GV = 5120
Z_GG = 5632
Z_MQ = 6144
Z_MK = 6656
Z_MV = 7168
Z_W = 7680
SEG_W = 512
SEG_RQK = Z_RQ // SEG_W
SEG_GQK = Z_GQ // SEG_W
SEG_MQ = Z_MQ // SEG_W
SEG_MK = Z_MK // SEG_W

SAMPLE_ROWS = 8
NEG_BIG = -1e30
MASK_BIG = 2.0 ** 100
LOG2E = 1.4426950408889634
MOBA_GROUP = 8

NT = (((1,), (1,)), ((), ()))
TN = (((0,), (0,)), ((), ()))


def _dot(a, b, dims=None, precision=None):
    if dims is None:
        return jnp.dot(a, b, preferred_element_type=F32, precision=precision)
    return lax.dot_general(a, b, dims, preferred_element_type=F32, precision=precision)


def _rms(x, g):
    return x * lax.rsqrt(jnp.mean(x * x, axis=-1, keepdims=True) + EPS) * g


def _silu(x):
    return x / (1.0 + jnp.exp(-x))


def _sigmoid(x):
    return 1.0 / (1.0 + jnp.exp(-x))


def _cparams(sem):
    return pltpu.CompilerParams(dimension_semantics=sem, vmem_limit_bytes=VMEM_LIMIT)


def _inproj_kernel(x_ref, g_ref, w_ref, wga_ref, w2_ref, ba_ref, cos_ref, sin_ref,
                   z_ref, la_ref, h_scr, *, sample):
    j = pl.program_id(1)

    @pl.when(j == 0)
    def _():
        hb = _rms(x_ref[...], g_ref[...]).astype(BF16)
        h_scr[...] = hb
        ga = _dot(hb, wga_ref[...])
        u = _dot(ga.astype(BF16), w2_ref[...]) + ba_ref[...]
        la = (jnp.minimum(u, 0.0) - jnp.log(1.0 + jnp.exp(-jnp.abs(u)))) * (1.0 / GLA_TAU)
        if sample:
            row = lax.broadcasted_iota(jnp.int32, (la.shape[0], 1), 0)
            la = jnp.where((row % SAMPLE_ROWS) >= SAMPLE_ROWS // 2, la, 0.0)
        la_ref[...] = la

    acc = _dot(h_scr[...], w_ref[...])
    lane = lax.broadcasted_iota(jnp.int32, (1, LANES), 1)
    first = (lane % MOBA_HD) < (MOBA_HD // 2)

    def rope(a):
        sw = jnp.where(first, pltpu.roll(a, LANES - MOBA_HD // 2, 1), pltpu.roll(a, MOBA_HD // 2, 1))
        return a * cos_ref[...] + sw * sin_ref[...]

    def store_rope(scales):
        for c, s in enumerate(scales):
            r = rope(acc[:, c * LANES:(c + 1) * LANES])
            z_ref[:, c * LANES:(c + 1) * LANES] = r if s == 1.0 else r * s

    ks = RET_DK ** -0.5

    @pl.when(j == SEG_RQK)
    def _():
        store_rope((1.0, 1.0, ks, ks))

    @pl.when(j == SEG_GQK)
    def _():
        z_ref[:, :SEG_W // 2] = acc[:, :SEG_W // 2]
        z_ref[:, SEG_W // 2:] = acc[:, SEG_W // 2:] * (GLA_DK ** -0.5)

    @pl.when(j == SEG_MQ)
    def _():
        store_rope((MOBA_HD ** -0.5,) * 4)

    @pl.when(j == SEG_MK)
    def _():
        store_rope((1.0,) * 4)

    plain = (j != SEG_RQK) & (j != SEG_GQK) & (j != SEG_MQ) & (j != SEG_MK)

    @pl.when(plain)
    def _():
        z_ref[...] = acc


def _inproj(x, g, w_main, w_ga, w2, ba, cos, sin, *, tm, sample):
    n = x.shape[0]
    nseg = Z_W // SEG_W
    ntab = cos.shape[0] // tm
    return pl.pallas_call(
        functools.partial(_inproj_kernel, sample=sample),
        grid=(n // tm, nseg),
        in_specs=[
            pl.BlockSpec((tm, D_MODEL), lambda i, j: (i, 0)),
            pl.BlockSpec((1, D_MODEL), lambda i, j: (0, 0)),
            pl.BlockSpec((D_MODEL, SEG_W), lambda i, j: (0, j)),
            pl.BlockSpec((D_MODEL, LANES), lambda i, j: (0, 0)),
            pl.BlockSpec((LANES, GLA_HEADS * GLA_DK), lambda i, j: (0, 0)),
            pl.BlockSpec((1, GLA_HEADS * GLA_DK), lambda i, j: (0, 0)),
            pl.BlockSpec((tm, LANES), lambda i, j: (i % ntab, 0)),
            pl.BlockSpec((tm, LANES), lambda i, j: (i % ntab, 0)),
        ],
        out_specs=[
            pl.BlockSpec((tm, SEG_W), lambda i, j: (i, j)),
            pl.BlockSpec((tm, GLA_HEADS * GLA_DK), lambda i, j: (i, 0)),
        ],
        out_shape=[jax.ShapeDtypeStruct((n, Z_W), F32),
                   jax.ShapeDtypeStruct((n, GLA_HEADS * GLA_DK), F32)],
        scratch_shapes=[pltpu.VMEM((tm, D_MODEL), BF16)],
        compiler_params=_cparams(("parallel", "arbitrary")),
        name="inproj",
    )(x, g, w_main, w_ga, w2, ba, cos, sin)


BNT = (((2,), (2,)), ((0,), (0,)))
BNN = (((2,), (1,)), ((0,), (0,)))


def _decay_tables(c, valid, cg):
    log_g = np.log1p(-np.exp2(-5.0 - np.arange(RET_HEADS, dtype=np.float64)))
    b = log_g[:, None] * np.cumsum(np.asarray(valid, np.float64))[None, :]
    diff = b[:, :, None] - b[:, None, :]
    causal = np.arange(c)[:, None] >= np.arange(c)[None, :]
    dmat = np.where(causal[None], np.exp(np.where(causal[None], diff, 0.0)), 0.0)
    qdec = np.broadcast_to(np.exp(b)[:, :, None], (RET_HEADS, c, LANES))
    kdec = np.broadcast_to(np.exp(b[:, -1:] - b)[:, :, None], (RET_HEADS, c, LANES))
    cdec = tuple(float(v) for v in np.exp(b[:, -1]))
    same = (np.arange(c)[:, None] // cg) == (np.arange(c)[None, :] // cg)
    tri = (causal & same).astype(np.float32)
    return (jnp.asarray(dmat, F32), jnp.asarray(qdec, F32), jnp.asarray(kdec, F32),
            jnp.asarray(tri, F32), cdec)


def _scan_kernel(*refs, T, C, SC, cdec, zero_init):
    if zero_init:
        (zr_ref, zg_ref, la_ref, dmat_ref, qdec_ref, kdec_ref, tri_ref, gr_ref, gg_ref,
         o_ref, sr_ref, sg_ref, Sr, SgT) = refs
    else:
        (zr_ref, zg_ref, la_ref, dmat_ref, qdec_ref, kdec_ref, tri_ref, gr_ref, gg_ref,
         s0r_ref, s0g_ref, o_ref, sr_ref, sg_ref, Sr, SgT) = refs
    t = pl.program_id(1)
    nc, nsub, nsc = T // C, C // SC, T // SC
    HK = RET_HEADS * RET_DK
    lane = lax.broadcasted_iota(jnp.int32, (1, LANES), 1)
    in_e = [lane < RET_DK, lane >= RET_DK]
    rowp = lax.broadcasted_iota(jnp.int32, (LANES, 1), 0)

    @pl.when(t == 0)
    def _():
        if zero_init:
            Sr[...] = jnp.zeros_like(Sr)
            SgT[...] = jnp.zeros_like(SgT)
        else:
            for p in range(RET_HEADS // 2):
                for e in range(2):
                    Sr[p, e * RET_DK:(e + 1) * RET_DK, :] = s0r_ref[0, 2 * p + e]
                SgT[p] = jnp.concatenate([s0g_ref[0, 2 * p].T, s0g_ref[0, 2 * p + 1].T], axis=1)


    for p in range(RET_HEADS // 2):
        qp = zr_ref[:, p * LANES:(p + 1) * LANES]
        kp = zr_ref[:, HK + p * LANES:HK + (p + 1) * LANES]
        kpb = kp.astype(BF16)
        s_old = Sr[p]
        s_new = s_old * jnp.where(rowp < RET_DK, cdec[2 * p], cdec[2 * p + 1])
        for e in range(2):
            h = 2 * p + e
            qh = jnp.where(in_e[e], qp, 0.0)
            vh = zr_ref[:, 2 * HK + h * RET_DV:2 * HK + (h + 1) * RET_DV].astype(BF16)
            att = _dot(qh.astype(BF16), kpb, NT) * dmat_ref[h]
            o = _dot(att.astype(BF16), vh) + _dot((qh * qdec_ref[h]).astype(BF16), s_old.astype(BF16))
            s_new = s_new + _dot((jnp.where(in_e[e], kp, 0.0) * kdec_ref[h]).T.astype(BF16), vh)
            cen = o - jnp.mean(o, axis=-1, keepdims=True)
            y = cen * lax.rsqrt(jnp.mean(cen * cen, axis=-1, keepdims=True) + EPS)
            rg = zr_ref[:, 2 * HK + BRANCH_W + h * RET_DV:2 * HK + BRANCH_W + (h + 1) * RET_DV]
            y = y * gr_ref[:, h * RET_DV:(h + 1) * RET_DV] * _silu(rg)
            o_ref[:, h * RET_DV:(h + 1) * RET_DV] = y.astype(o_ref.dtype)
        Sr[p] = s_new

    q = zg_ref[:, 0:HK]
    k = zg_ref[:, HK:2 * HK]
    b = _dot(tri_ref[...], la_ref[...], precision=lax.Precision.HIGHEST)
    q3, k3, b3 = (a.reshape(nc, C, HK) for a in (q, k, b))
    blast3 = b3[:, C - 1:C, :]
    qe3 = q3 * jnp.exp(b3)
    kd3 = k3 * jnp.exp(blast3 - b3)
    eblast3 = jnp.exp(blast3)
    q4, k4, b4 = (a.reshape(nsc, SC, HK) for a in (q, k, b))
    rsub4 = lax.broadcasted_iota(jnp.int32, (1, SC, 1), 1)
    rowT = lax.broadcasted_iota(jnp.int32, (T, 1), 0)
    sub0 = (rowT % C) - (rowT % SC)
    lane_c = lax.broadcasted_iota(jnp.int32, (1, C), 1)
    rc3 = lax.broadcasted_iota(jnp.int32, (1, C, 1), 1)
    in_e3 = [m.reshape(1, 1, LANES) for m in in_e]

    att = [jnp.zeros((T, C), F32) for _ in range(GLA_HEADS)]
    for jj in range(SC):
        w = q4 * k4[:, jj:jj + 1, :] * jnp.exp(jnp.where(rsub4 >= jj, b4 - b4[:, jj:jj + 1, :], NEG_BIG))
        hit = lane_c == sub0 + jj
        for h in range(GLA_HEADS):
            wp = w[:, :, (h // 2) * LANES:(h // 2 + 1) * LANES]
            col = jnp.sum(jnp.where(in_e3[h % 2], wp, 0.0), axis=-1, keepdims=True).reshape(T, 1)
            att[h] = jnp.where(hit, col, att[h])
    att3 = [a.reshape(nc, C, C) for a in att]
    for i in range(1, nsub):
        bi = b3[:, i * SC - 1:i * SC, :]
        in_i = (rc3 >= i * SC) & (rc3 < (i + 1) * SC)
        qi = q3 * jnp.exp(jnp.where(in_i, b3 - bi, NEG_BIG))
        ki = (k3 * jnp.exp(jnp.where(rc3 < i * SC, bi - b3, NEG_BIG))).astype(BF16)
        for h in range(GLA_HEADS):
            ps = slice((h // 2) * LANES, (h // 2 + 1) * LANES)
            qih = jnp.where(in_e3[h % 2], qi[:, :, ps], 0.0).astype(BF16)
            att3[h] = att3[h] + lax.dot_general(qih, ki[:, :, ps], BNT, preferred_element_type=F32)
    for p in range(GLA_HEADS // 2):
        ps = slice(p * LANES, (p + 1) * LANES)
        o_intra, qeb, ut3 = [], [], None
        for e in range(2):
            h = 2 * p + e
            v3 = zg_ref[:, 2 * HK + h * GLA_DV:2 * HK + (h + 1) * GLA_DV].reshape(nc, C, GLA_DV)
            o_intra.append(lax.dot_general(att3[h].astype(BF16), v3.astype(BF16), BNN, preferred_element_type=F32))
            qeb.append(jnp.where(in_e3[e], qe3[:, :, ps], 0.0).astype(BF16))
            u = lax.dot_general(jnp.swapaxes(v3, 1, 2).astype(BF16),
                                jnp.where(in_e3[e], kd3[:, :, ps], 0.0).astype(BF16), BNN,
                                preferred_element_type=F32)
            ut3 = u if ut3 is None else ut3 + u
        st = SgT[p]
        outs = [[], []]
        for c in range(nc):
            stb = st.astype(BF16)
            for e in range(2):
                outs[e].append(o_intra[e][c] + _dot(qeb[e][c], stb, NT))
            st = st * eblast3[c][:, ps] + ut3[c]
        SgT[p] = st
        for e in range(2):
            h = 2 * p + e
            o = outs[e][0] if nc == 1 else jnp.concatenate(outs[e], axis=0)
            gg = zg_ref[:, 2 * HK + BRANCH_W + h * GLA_DV:2 * HK + BRANCH_W + (h + 1) * GLA_DV]
            y = _rms(o, gg_ref[:, h * GLA_DV:(h + 1) * GLA_DV]) * _silu(gg)
            o_ref[:, BRANCH_W + h * GLA_DV:BRANCH_W + (h + 1) * GLA_DV] = y.astype(o_ref.dtype)

    @pl.when(t == pl.num_programs(1) - 1)
    def _():
        for p in range(RET_HEADS // 2):
            for e in range(2):
                sr_ref[0, 2 * p + e] = Sr[p, e * RET_DK:(e + 1) * RET_DK, :]
                sg_ref[0, 2 * p + e] = SgT[p][:, e * GLA_DK:(e + 1) * GLA_DK].T


def _scan(z, la, g_ret, g_gla, tables, s0, *, nseq, seq_rows, T, C, SC, out_dtype):
    dmat, qdec, kdec, tri, cdec = tables
    nT = seq_rows // T
    zero_init = s0 is None
    zw = 2 * RET_HEADS * RET_DK + 2 * BRANCH_W
    const3 = lambda s, t: (0, 0, 0)
    const2 = lambda s, t: (0, 0)
    in_specs = [
        pl.BlockSpec((T, zw), lambda s, t: (s * nT + t, Z_RQ // zw)),
        pl.BlockSpec((T, zw), lambda s, t: (s * nT + t, Z_GQ // zw)),
        pl.BlockSpec((T, GLA_HEADS * GLA_DK), lambda s, t: (s * nT + t, 0)),
        pl.BlockSpec(dmat.shape, const3),
        pl.BlockSpec(qdec.shape, const3),
        pl.BlockSpec(kdec.shape, const3),
        pl.BlockSpec(tri.shape, const2),
        pl.BlockSpec((1, BRANCH_W), const2),
        pl.BlockSpec((1, BRANCH_W), const2),
    ]
    args = [z, z, la, dmat, qdec, kdec, tri, g_ret, g_gla]
    st_spec = pl.BlockSpec((1, RET_HEADS, RET_DK, RET_DV), lambda s, t: (s, 0, 0, 0))
    if not zero_init:
        in_specs += [st_spec, st_spec]
        args += [s0[0], s0[1]]
    st_shape = jax.ShapeDtypeStruct((nseq, RET_HEADS, RET_DK, RET_DV), F32)
    return pl.pallas_call(
        functools.partial(_scan_kernel, T=T, C=C, SC=SC, cdec=cdec, zero_init=zero_init),
        grid=(nseq, nT),
        in_specs=in_specs,
        out_specs=[pl.BlockSpec((T, 2 * BRANCH_W), lambda s, t: (s * nT + t, 0)), st_spec, st_spec],
        out_shape=[jax.ShapeDtypeStruct((nseq * seq_rows, 2 * BRANCH_W), out_dtype), st_shape, st_shape],
        scratch_shapes=[pltpu.VMEM((RET_HEADS // 2, LANES, RET_DV), F32),
                        pltpu.VMEM((GLA_HEADS // 2, GLA_DV, LANES), F32)],
        compiler_params=_cparams(("parallel", "arbitrary")),
        name="scan",
    )(*args)


PAGES_PER_STEP = 16
PAGES_PER_BLOCK = MOBA_BLOCK // PAGE_SIZE
VT_ROWS = MOBA_HD + 16


def _kmean_step(page_refs, km_ref):
    for blk in range(PAGES_PER_STEP // PAGES_PER_BLOCK):
        tot = page_refs[blk * PAGES_PER_BLOCK][...]
        for p in range(1, PAGES_PER_BLOCK):
            tot = tot + page_refs[blk * PAGES_PER_BLOCK + p][...]
        km_ref[0, 0, :, :, :, blk:blk + 1] = jnp.sum(tot, axis=-1, keepdims=True) * (1.0 / MOBA_BLOCK)


def _moba_prompt_kernel(*refs, nblk, with_kmean):
    if with_kmean:
        q_ref, k_ref, v_ref = refs[1:4]
        o_ref, kmo_ref, km_scr, ka_scr, vt_scr = refs[4 + PAGES_PER_STEP:]
        _kmean_step(refs[4:4 + PAGES_PER_STEP], kmo_ref)
    else:
        q_ref, k_ref, v_ref, o_ref, km_scr, ka_scr, vt_scr = refs
    t = pl.program_id(2)
    BLK = MOBA_BLOCK
    G = MOBA_GROUP
    lane = lax.broadcasted_iota(jnp.int32, (1, LANES), 1)
    in_e = [lane < MOBA_HD, lane >= MOBA_HD]
    off = [MOBA_HD, 0]

    @pl.when(t == 0)
    def _():
        for blk in range(nblk):
            km_scr[blk:blk + 1, :] = jnp.mean(k_ref[blk * BLK:(blk + 1) * BLK, :], axis=0, keepdims=True)

        def prep(blk, c):
            r0 = pl.multiple_of(blk * BLK, BLK)
            kj = k_ref[pl.ds(r0, BLK), :]
            for e in range(2):
                ka_scr[e, pl.ds(r0, BLK), :] = jnp.where(
                    in_e[e], kj, jnp.where(lane == off[e] + blk, 1.0, 0.0)).astype(BF16)
            vt = v_ref[pl.ds(r0, BLK), :].T.astype(BF16)
            for e in range(2):
                vt_scr[e, 0:MOBA_HD, pl.ds(r0, BLK)] = vt[e * MOBA_HD:(e + 1) * MOBA_HD, :]
                vt_scr[e, MOBA_HD:VT_ROWS, pl.ds(r0, BLK)] = jnp.ones((VT_ROWS - MOBA_HD, BLK), BF16)
            return c
        lax.fori_loop(0, nblk, prep, 0)

    q = q_ref[...]
    q2t = (q * LOG2E).T
    t0 = pl.multiple_of(t * BLK, BLK)
    blkrow = lax.broadcasted_iota(jnp.int32, (nblk, 1), 0)
    rowk = lax.broadcasted_iota(jnp.int32, (BLK, 1), 0)
    laneq = lax.broadcasted_iota(jnp.int32, (1, BLK), 1)
    zpad = jnp.zeros((MOBA_HD - nblk, BLK), F32)
    zhead = jnp.zeros((MOBA_HD, BLK), F32)
    waug, init = [], []
    for e in range(2):
        g = _dot(km_scr[...], jnp.where(in_e[e], q, 0.0), NT, precision=lax.Precision.HIGHEST)
        g = jnp.where(blkrow < t, g, -jnp.inf)
        sel = jnp.zeros(g.shape, jnp.bool_)
        for _ in range(MOBA_TOPK):
            m = jnp.max(g, axis=0, keepdims=True)
            idx = jnp.min(jnp.where(g == m, blkrow, nblk), axis=0, keepdims=True)
            pick = (blkrow == idx) & (m > -jnp.inf)
            sel = sel | pick
            g = jnp.where(pick, -jnp.inf, g)
        pen = jnp.where(sel, 0.0, -MASK_BIG)
        qe = q2t[e * MOBA_HD:(e + 1) * MOBA_HD, :]
        if e == 0:
            waug.append(jnp.concatenate([qe, pen, zpad], axis=0).astype(BF16))
            wown = jnp.concatenate([qe, zhead], axis=0).astype(BF16)
        else:
            waug.append(jnp.concatenate([pen, zpad, qe], axis=0).astype(BF16))
            wown = jnp.concatenate([zhead, qe], axis=0).astype(BF16)
        s = _dot(ka_scr[e, pl.ds(t0, BLK), :], wown)
        s = jnp.where(rowk <= laneq, s, NEG_BIG)
        m0 = jnp.max(s, axis=0, keepdims=True)
        init += [m0, _dot(vt_scr[e, :, pl.ds(t0, BLK)], jnp.exp2(s - m0).astype(BF16))]

    def body(gi, carry):
        r0 = pl.multiple_of(gi * (G * BLK), G * BLK)
        st = list(carry)
        ss = [[_dot(ka_scr[e, pl.ds(r0 + j * BLK, BLK), :], waug[e]) for j in range(G)] for e in range(2)]
        for j in range(G):
            for e in range(2):
                m, acc = st[2 * e:2 * e + 2]
                s = ss[e][j]
                mn = jnp.maximum(m, jnp.max(s, axis=0, keepdims=True))
                p = jnp.exp2(s - mn).astype(BF16)
                st[2 * e:2 * e + 2] = [mn, jnp.exp2(m - mn) * acc + _dot(vt_scr[e, :, pl.ds(r0 + j * BLK, BLK)], p)]
        return tuple(st)

    res = lax.fori_loop(0, (t + G - 1) // G, body, tuple(init))
    outs = [res[2 * e + 1][0:MOBA_HD] / res[2 * e + 1][MOBA_HD:MOBA_HD + 1] for e in range(2)]
    o_ref[...] = jnp.concatenate(outs, axis=0).T.astype(o_ref.dtype)


def _moba_prompt(z, *, nseq, seq_rows, kmean_src=None):
    nblk = seq_rows // MOBA_BLOCK
    npair = MOBA_HEADS // 2
    assert nblk % MOBA_GROUP == 0 and nblk <= MOBA_HD and nblk % SUBLANES == 0
    with_kmean = kmean_src is not None
    in_specs = [
        pl.BlockSpec((MOBA_BLOCK, LANES), lambda b, hp, t, *_: (b * nblk + t, Z_MQ // LANES + hp)),
        pl.BlockSpec((seq_rows, LANES), lambda b, hp, t, *_: (b, Z_MK // LANES + hp)),
        pl.BlockSpec((seq_rows, LANES), lambda b, hp, t, *_: (b, Z_MV // LANES + hp)),
    ]
    out_specs = [pl.BlockSpec((MOBA_BLOCK, LANES), lambda b, hp, t, *_: (b * nblk + t, hp))]
    out_shape = [jax.ShapeDtypeStruct((nseq * seq_rows, BRANCH_W), BF16)]
    args = [z, z, z]
    if with_kmean:
        cache_kt, pt_flat, nseq_s, npages = kmean_src
        depth = cache_kt.shape[1]
        bps = PAGES_PER_STEP // PAGES_PER_BLOCK
        nstep = npages // PAGES_PER_STEP
        assert nseq * npair * nblk == nseq_s * nstep

        def page_spec(i):
            def imap(b, hp, t, pt):
                return (pt[((b * npair + hp) * nblk + t) * PAGES_PER_STEP + i], 0, 0, 0, 0)
            return pl.BlockSpec((None, depth, MOBA_HEADS, MOBA_HD, PAGE_SIZE), imap)

        def km_map(b, hp, t, pt):
            s = (b * npair + hp) * nblk + t
            return (s // nstep, s % nstep, 0, 0, 0, 0)

        in_specs += [page_spec(i) for i in range(PAGES_PER_STEP)]
        args = [pt_flat] + args + [cache_kt] * PAGES_PER_STEP
        out_specs.append(pl.BlockSpec((1, 1, depth, MOBA_HEADS, MOBA_HD, bps), km_map))
        out_shape.append(jax.ShapeDtypeStruct((nseq_s, nstep, depth, MOBA_HEADS, MOBA_HD, bps), F32))
    grid_spec = pltpu.PrefetchScalarGridSpec(
        num_scalar_prefetch=1 if with_kmean else 0,
        grid=(nseq, npair, nblk),
        in_specs=in_specs,
        out_specs=out_specs,
        scratch_shapes=[pltpu.VMEM((nblk, LANES), F32),
                        pltpu.VMEM((2, seq_rows, LANES), BF16),
                        pltpu.VMEM((2, VT_ROWS, seq_rows), BF16)],
    )
    res = pl.pallas_call(
        functools.partial(_moba_prompt_kernel, nblk=nblk, with_kmean=with_kmean),
        grid_spec=grid_spec,
        out_shape=out_shape,
        compiler_params=_cparams(("parallel", "parallel", "arbitrary")),
        name="moba_prompt",
    )(*args)
    if not with_kmean:
        return res[0], None
    km = res[1].transpose(0, 2, 3, 4, 1, 5).reshape(nseq_s, depth, MOBA_HEADS, MOBA_HD, nstep * bps)
    return res[0], km


def _moba_topk_kernel(q_ref, km_ref, idx_ref):
    nfull = km_ref.shape[2]
    lane = lax.broadcasted_iota(jnp.int32, (1, LANES), 1)
    lane_b = lax.broadcasted_iota(jnp.int32, (1, nfull), 1)
    out = jnp.zeros((SAMPLE_ROWS, LANES), jnp.int32)
    for h in range(MOBA_HEADS):
        gsc = _dot(q_ref[h], km_ref[h], precision=lax.Precision.HIGHEST)
        for r in range(MOBA_TOPK):
            m = jnp.max(gsc, axis=-1, keepdims=True)
            idx = jnp.min(jnp.where(gsc == m, lane_b, nfull), axis=-1, keepdims=True)
            out = jnp.where(lane == h * MOBA_TOPK + r, idx, out)
            gsc = jnp.where(lane_b == idx, -jnp.inf, gsc)
    idx_ref[0] = out


def _moba_topk(qh, kmean, *, layer, nseq):
    nfull = kmean.shape[4]
    return pl.pallas_call(
        _moba_topk_kernel,
        grid=(nseq,),
        in_specs=[
            pl.BlockSpec((MOBA_HEADS, SAMPLE_ROWS, MOBA_HD), lambda b: (0, b, 0)),
            pl.BlockSpec((None, None, MOBA_HEADS, MOBA_HD, nfull), lambda b: (b, layer, 0, 0, 0)),
        ],
        out_specs=pl.BlockSpec((1, SAMPLE_ROWS, LANES), lambda b: (b, 0, 0)),
        out_shape=jax.ShapeDtypeStruct((nseq, SAMPLE_ROWS, LANES), jnp.int32),
        compiler_params=_cparams(("parallel",)),
        name="moba_topk",
    )(qh, kmean)


def _moba_attend_kernel(idx_ref, pt_ref, q_ref, kn_ref, vn_ref, ck_hbm, cv_hbm, o_ref, kbuf, vbuf, sem,
                        *, layer, npages, ntok):
    b = pl.program_id(0)
    h = pl.program_id(1)
    nh = pl.num_programs(1)
    step = b * nh + h
    nstep = pl.num_programs(0) * nh
    slot = step % 2

    def copies(bb, hh, sl):
        out = []
        for t in range(ntok):
            for r in range(MOBA_TOPK):
                blk = idx_ref[((bb * ntok + t) * MOBA_HEADS + hh) * MOBA_TOPK + r]
                for p in range(PAGES_PER_BLOCK):
                    page = pt_ref[bb * npages + blk * PAGES_PER_BLOCK + p]
                    i = (t * MOBA_TOPK + r) * PAGES_PER_BLOCK + p
                    out.append(pltpu.make_async_copy(ck_hbm.at[page, layer, hh], kbuf.at[sl, i], sem.at[0, sl]))
                    out.append(pltpu.make_async_copy(cv_hbm.at[page, layer, hh], vbuf.at[sl, i], sem.at[1, sl]))
        return out

    @pl.when(step == 0)
    def _():
        for c in copies(b, h, slot):
            c.start()

    @pl.when(step + 1 < nstep)
    def _():
        nxt = step + 1
        for c in copies(nxt // nh, nxt % nh, 1 - slot):
            c.start()

    for c in copies(b, h, slot):
        c.wait()

    half = SAMPLE_ROWS // 2
    rowi = lax.broadcasted_iota(jnp.int32, (SAMPLE_ROWS, 1), 0)
    coln = lax.broadcasted_iota(jnp.int32, (1, SAMPLE_ROWS), 1)
    qb = q_ref[...].astype(BF16)
    vn = vn_ref[...].astype(BF16)
    s_new_all = _dot(qb, kn_ref[...].astype(BF16), NT)
    out = jnp.zeros((SAMPLE_ROWS, MOBA_HD), F32)
    for t in range(ntok):
        qrow = half + t
        s_new = jnp.where((coln >= half) & (coln <= qrow), s_new_all, NEG_BIG)
        ss, vs = [], []
        for i in range(t * MOBA_TOPK * PAGES_PER_BLOCK, (t + 1) * MOBA_TOPK * PAGES_PER_BLOCK):
            ss.append(_dot(qb, kbuf[slot, i].astype(BF16)))
            vs.append(vbuf[slot, i].astype(BF16))
        m = jnp.max(s_new, axis=-1, keepdims=True)
        for s in ss:
            m = jnp.maximum(m, jnp.max(s, axis=-1, keepdims=True))
        p_new = jnp.exp(s_new - m)
        l = jnp.sum(p_new, axis=-1, keepdims=True)
        o = _dot(p_new.astype(BF16), vn)
        for s, vb in zip(ss, vs):
            p = jnp.exp(s - m)
            l = l + jnp.sum(p, axis=-1, keepdims=True)
            o = o + _dot(p.astype(BF16), vb, NT)
        out = jnp.where(rowi == qrow, o / l, out)
    o_ref[...] = out


def _moba_attend(qh, knh, vnh, cache_k, cache_v, idx_flat, pt_flat, *, layer, nseq, npages, ntok):
    nbuf = ntok * MOBA_TOPK * PAGES_PER_BLOCK
    head_spec = pl.BlockSpec((None, SAMPLE_ROWS, MOBA_HD), lambda b, h, idx, pt: (h, b, 0))
    grid_spec = pltpu.PrefetchScalarGridSpec(
        num_scalar_prefetch=2,
        grid=(nseq, MOBA_HEADS),
        in_specs=[head_spec, head_spec, head_spec,
                  pl.BlockSpec(memory_space=pl.ANY), pl.BlockSpec(memory_space=pl.ANY)],
        out_specs=head_spec,
        scratch_shapes=[pltpu.VMEM((2, nbuf, MOBA_HD, PAGE_SIZE), F32),
                        pltpu.VMEM((2, nbuf, MOBA_HD, PAGE_SIZE), F32),
                        pltpu.SemaphoreType.DMA((2, 2))],
    )
    return pl.pallas_call(
        functools.partial(_moba_attend_kernel, layer=layer, npages=npages, ntok=ntok),
        grid_spec=grid_spec,
        out_shape=jax.ShapeDtypeStruct((MOBA_HEADS, nseq * SAMPLE_ROWS, MOBA_HD), F32),
        compiler_params=_cparams(("arbitrary", "arbitrary")),
        name="moba_attend",
    )(idx_flat, pt_flat, qh, knh, vnh, cache_k, cache_v)


def _post_kernel(x_ref, org_ref, om_ref, gl_ref, wb_ref, wo_ref, gpost_ref, gpre_ref, x1_ref, h2_ref):
    merged = None
    for n in range(N_BRANCH):
        if n < 2:
            br = org_ref[:, n * BRANCH_W:(n + 1) * BRANCH_W]
        else:
            br = om_ref[...]
        proj = _dot(br.astype(BF16), wb_ref[n])
        term = _sigmoid(gl_ref[:, n * D_MODEL:(n + 1) * D_MODEL]) * proj
        merged = term if merged is None else merged + term
    y = _dot(merged.astype(BF16), wo_ref[...])
    x1 = x_ref[...] + _rms(y, gpost_ref[...])
    x1_ref[...] = x1
    h2_ref[...] = _rms(x1, gpre_ref[...]).astype(BF16)


def _post(x, org, om, z, wb, wo, gpost, gpre, *, tm):
    n = x.shape[0]
    c2 = lambda i: (0, 0)
    return pl.pallas_call(
        _post_kernel,
        grid=(n // tm,),
        in_specs=[
            pl.BlockSpec((tm, D_MODEL), lambda i: (i, 0)),
            pl.BlockSpec((tm, 2 * BRANCH_W), lambda i: (i, 0)),
            pl.BlockSpec((tm, BRANCH_W), lambda i: (i, 0)),
            pl.BlockSpec((tm, N_BRANCH * D_MODEL), lambda i: (i, 0)),
            pl.BlockSpec((N_BRANCH, BRANCH_W, D_MODEL), lambda i: (0, 0, 0)),
            pl.BlockSpec((D_MODEL, D_MODEL), c2),
            pl.BlockSpec((1, D_MODEL), c2),
            pl.BlockSpec((1, D_MODEL), c2),
        ],
        out_specs=[pl.BlockSpec((tm, D_MODEL), lambda i: (i, 0)),
                   pl.BlockSpec((tm, D_MODEL), lambda i: (i, 0))],
        out_shape=[jax.ShapeDtypeStruct((n, D_MODEL), F32), jax.ShapeDtypeStruct((n, D_MODEL), BF16)],
        compiler_params=_cparams(("parallel",)),
        name="post",
    )(x, org, om, z, wb, wo, gpost, gpre)


FF_CHUNK = 256


def _ffn_kernel(*refs, sample, tiles_per_seq, tail_rows):
    if sample:
        (x1_ref, h2_ref, st_ref, wup_ref, wc_ref, bc_ref, wd_ref, g_ref, x2_ref, tail_ref, act_scr) = refs
    else:
        (x1_ref, h2_ref, halo_ref, wup_ref, wc_ref, bc_ref, wd_ref, g_ref, x2_ref, tail_ref, act_scr) = refs
    i = pl.program_id(0)
    tm = x1_ref.shape[0]
    h2 = h2_ref[...]
    row = lax.broadcasted_iota(jnp.int32, (tm, 1), 0)
    row8 = lax.broadcasted_iota(jnp.int32, (SUBLANES, 1), 0)
    if not sample:
        halo = jnp.where((i % tiles_per_seq) == 0, jnp.zeros_like(halo_ref[...]), halo_ref[...])

    def shifted(u, uh, sh):
        us = pltpu.roll(u, sh, 0)
        if uh is None:
            return us
        hs = pltpu.roll(uh, sh, 0)
        head = jnp.where(row8 < sh, hs, us[:SUBLANES])
        return jnp.concatenate([head, us[SUBLANES:]], axis=0)

    for c in range(D_FF // FF_CHUNK):
        halves = []
        for half in range(2):
            c0 = half * D_FF + c * FF_CHUNK
            w = wup_ref[:, c0:c0 + FF_CHUNK]
            u = _dot(h2, w)
            if sample:
                inj = ((row % SAMPLE_ROWS) >= SAMPLE_ROWS // 2 - (CONV_W - 1)) & ((row % SAMPLE_ROWS) < SAMPLE_ROWS // 2)
                u = jnp.where(inj, st_ref[:, c0:c0 + FF_CHUNK], u)
                uh = None
            else:
                uh = _dot(halo, w)
            tail_ref[0, :, c0:c0 + FF_CHUNK] = u[tm - tail_rows:]
            conv = bc_ref[:, c0:c0 + FF_CHUNK] + u * wc_ref[CONV_W - 1:CONV_W, c0:c0 + FF_CHUNK]
            for sh in range(1, CONV_W):
                conv = conv + shifted(u, uh, sh) * wc_ref[CONV_W - 1 - sh:CONV_W - sh, c0:c0 + FF_CHUNK]
            halves.append(conv)
        a, b = halves
        gelu = 0.5 * a * (1.0 + jnp.tanh(0.7978845608028654 * (a + 0.044715 * a * a * a)))
        act_scr[:, c * FF_CHUNK:(c + 1) * FF_CHUNK] = (gelu * b).astype(BF16)
    x2 = x1_ref[...] + _rms(_dot(act_scr[...], wd_ref[...]), g_ref[...])
    if sample:
        x2 = jnp.where((row % SAMPLE_ROWS) >= SAMPLE_ROWS // 2, x2, 0.0)
    x2_ref[...] = x2


def _ffn(x1, h2, extra, wup, wc, bc, wd, g, *, tm, sample, nseq):
    n = x1.shape[0]
    ntile = n // tm
    tiles_per_seq = ntile // nseq if not sample else 1
    tail_rows = tm if sample else SUBLANES
    ntail = ntile if sample else nseq
    c2 = lambda i: (0, 0)
    if sample:
        extra_spec = pl.BlockSpec((tm, 2 * D_FF), lambda i: (i, 0))
        extra_arg = extra
    else:
        hb = tm // SUBLANES
        extra_spec = pl.BlockSpec((SUBLANES, D_MODEL), lambda i: (jnp.maximum(i * hb - 1, 0), 0))
        extra_arg = h2
    return pl.pallas_call(
        functools.partial(_ffn_kernel, sample=sample, tiles_per_seq=tiles_per_seq, tail_rows=tail_rows),
        grid=(ntile,),
        in_specs=[
            pl.BlockSpec((tm, D_MODEL), lambda i: (i, 0)),
            pl.BlockSpec((tm, D_MODEL), lambda i: (i, 0)),
            extra_spec,
            pl.BlockSpec((D_MODEL, 2 * D_FF), c2, pipeline_mode=pl.Buffered(1)),
            pl.BlockSpec((CONV_W, 2 * D_FF), c2),
            pl.BlockSpec((1, 2 * D_FF), c2),
            pl.BlockSpec((D_FF, D_MODEL), c2, pipeline_mode=pl.Buffered(1)),
            pl.BlockSpec((1, D_MODEL), c2),
        ],
        out_specs=[pl.BlockSpec((tm, D_MODEL), lambda i: (i, 0)),
                   pl.BlockSpec((1, tail_rows, 2 * D_FF), lambda i: (i // tiles_per_seq, 0, 0))],
        out_shape=[jax.ShapeDtypeStruct((n, D_MODEL), F32),
                   jax.ShapeDtypeStruct((ntail, tail_rows, 2 * D_FF), F32)],
        scratch_shapes=[pltpu.VMEM((tm, D_FF), BF16)],
        compiler_params=_cparams(("arbitrary",)),
        name="ffn",
    )(x1, h2, extra_arg, wup, wc, bc, wd, g)


def _kv_out_kernel(*refs, depth):
    z_refs, (kt_ref, vt_ref) = refs[:2 * depth], refs[2 * depth:]
    for l in range(depth):
        kt_ref[l] = z_refs[2 * l][...].T
        vt_ref[l] = z_refs[2 * l + 1][...].T


def _kv_out(zs, *, nseq, seq_rows, tm):
    depth = len(zs)
    W = MOBA_HEADS * MOBA_HD
    nT = seq_rows // tm
    in_specs, args = [], []
    for z in zs:
        in_specs += [pl.BlockSpec((tm, W), lambda b, i: (b * nT + i, Z_MK // W)),
                     pl.BlockSpec((tm, W), lambda b, i: (b * nT + i, Z_MV // W))]
        args += [z, z]
    out_spec = pl.BlockSpec((None, depth, W, tm), lambda b, i: (b, 0, 0, i))
    shape = jax.ShapeDtypeStruct((nseq, depth, W, seq_rows), F32)
    return pl.pallas_call(
        functools.partial(_kv_out_kernel, depth=depth),
        grid=(nseq, nT),
        in_specs=in_specs,
        out_specs=[out_spec, out_spec],
        out_shape=[shape, shape],
        compiler_params=_cparams(("parallel", "parallel")),
        name="kv_out",
    )(*args)


def _rope_tables(pos):
    half = MOBA_HD // 2
    freqs = jnp.power(ROPE_THETA, -jnp.arange(half, dtype=F32) / half)
    ang = pos.astype(F32)[:, None] * freqs[None, :]
    cos, sin = jnp.cos(ang), jnp.sin(ang)
    return (jnp.concatenate([cos, cos, cos, cos], axis=1),
            jnp.concatenate([-sin, sin, -sin, sin], axis=1))


def kernel(x_prompt, x_sample, cache_k, cache_v, page_table, state_ret, state_gla, state_conv, g_pre_mix, w_in, w_gla_a2, b_gla_a, g_ret_norm, g_gla_norm, w_branch, w_out, g_post_mix, g_pre_ffn, w_up, w_conv, b_conv, w_down, g_post_ffn):
    B, S, D = x_prompt.shape
    Bd, T, _ = x_sample.shape
    depth = w_in.shape[0]
    npages = page_table.shape[1]
    past_len = npages * PAGE_SIZE
    half = SAMPLE_ROWS // 2
    assert D == D_MODEL and T == half and S % 2048 == 0
    assert past_len % MOBA_BLOCK == 0 and npages % PAGES_PER_STEP == 0
    assert past_len // MOBA_BLOCK >= MOBA_TOPK
    n_pool = cache_k.shape[0]
    W = MOBA_HEADS * MOBA_HD

    xp = x_prompt.reshape(B * S, D)
    xs = jnp.pad(x_sample, ((0, 0), (half, 0), (0, 0))).reshape(Bd * SAMPLE_ROWS, D)
    ns = Bd * SAMPLE_ROWS

    cos_p, sin_p = _rope_tables(jnp.arange(S, dtype=jnp.int32))
    pos_s = past_len + jnp.tile(jnp.arange(SAMPLE_ROWS, dtype=jnp.int32) - half, Bd)
    cos_s, sin_s = _rope_tables(pos_s)

    tab_p = _decay_tables(SCAN_TILE, np.ones(SCAN_TILE), CHUNK)
    tab_s = _decay_tables(SAMPLE_ROWS, np.concatenate([np.zeros(half), np.ones(half)]), SAMPLE_ROWS)

    pt_flat = page_table.reshape(-1).astype(jnp.int32)
    cache_kt = cache_k.transpose(0, 2, 3, 4, 1)
    cache_vt = cache_v.transpose(0, 2, 3, 4, 1)
    kmean = None

    def per_head(a):
        return a.reshape(a.shape[0], MOBA_HEADS, MOBA_HD).transpose(1, 0, 2)

    outs = {k: [] for k in ("ret_p", "ret_s", "gla_p", "gla_s", "k_p", "k_s", "v_s", "conv_p", "conv_s")}
    for l in range(depth):
        w = w_in[l]
        o_ga = 2 * RET_HEADS * RET_DK + 2 * BRANCH_W + 2 * GLA_HEADS * GLA_DK + 2 * BRANCH_W
        o_m = o_ga + GLA_RANK
        o_gl = o_m + 3 * BRANCH_W
        w_main = jnp.concatenate([w[:, o_gl:], w[:, :o_ga], w[:, o_m:o_gl]], axis=1).astype(BF16)
        w_ga = jnp.pad(w[:, o_ga:o_m], ((0, 0), (0, LANES - GLA_RANK))).astype(BF16)
        w2 = jnp.pad(w_gla_a2[l], ((0, LANES - GLA_RANK), (0, 0))).astype(BF16)
        ba = b_gla_a[l][None, :]
        g_pre = g_pre_mix[l][None, :]
        g_ret = g_ret_norm[l][None, :]
        g_gla = g_gla_norm[l][None, :]
        wb = w_branch[l].astype(BF16)
        wo = w_out[l].astype(BF16)
        gpost = g_post_mix[l][None, :]
        gpre2 = g_pre_ffn[l][None, :]
        wup = w_up[l].astype(BF16)
        wd = w_down[l].astype(BF16)
        wc = w_conv[l]
        bc = b_conv[l][None, :]
        gffn = g_post_ffn[l][None, :]

        zp, lap = _inproj(xp, g_pre, w_main, w_ga, w2, ba, cos_p, sin_p, tm=2048, sample=False)
        orgp, retp, glap = _scan(zp, lap, g_ret, g_gla, tab_p, None, nseq=B, seq_rows=S, T=SCAN_TILE,
                                 C=CHUNK, SC=SUBCHUNK, out_dtype=BF16)
        if kmean is None:
            omp, kmean = _moba_prompt(zp, nseq=B, seq_rows=S, kmean_src=(cache_kt, pt_flat, Bd, npages))
        else:
            omp, _ = _moba_prompt(zp, nseq=B, seq_rows=S)
        x1p, h2p = _post(xp, orgp, omp, zp, wb, wo, gpost, gpre2, tm=512)
        xp, tailp = _ffn(x1p, h2p, None, wup, wc, bc, wd, gffn, tm=512, sample=False, nseq=B)
        outs["ret_p"].append(retp)
        outs["gla_p"].append(glap)
        outs["k_p"].append(zp)
        outs["conv_p"].append(tailp[:, SUBLANES - (CONV_W - 1):, :])

        zs, las = _inproj(xs, g_pre, w_main, w_ga, w2, ba, cos_s, sin_s, tm=ns, sample=True)
        orgs, rets, glas = _scan(zs, las, g_ret, g_gla, tab_s, (state_ret[l], state_gla[l]), nseq=Bd,
                                 seq_rows=SAMPLE_ROWS, T=SAMPLE_ROWS, C=SAMPLE_ROWS, SC=SAMPLE_ROWS,
                                 out_dtype=F32)
        qh = per_head(zs[:, Z_MQ:Z_MQ + W])
        idx = _moba_topk(qh, kmean, layer=l, nseq=Bd)
        idx_flat = idx[:, half:, :MOBA_HEADS * MOBA_TOPK].reshape(-1)
        oh = _moba_attend(qh, per_head(zs[:, Z_MK:Z_MK + W]), per_head(zs[:, Z_MV:Z_MV + W]), cache_kt, cache_vt,
                          idx_flat, pt_flat, layer=l, nseq=Bd, npages=npages, ntok=T)
        oms = oh.transpose(1, 0, 2).reshape(ns, W)
        x1s, h2s = _post(xs, orgs, oms, zs, wb, wo, gpost, gpre2, tm=ns)
        st = jnp.pad(state_conv[l], ((0, 0), (half - (CONV_W - 1), half), (0, 0))).reshape(ns, 2 * D_FF)
        xs, tails = _ffn(x1s, h2s, st, wup, wc, bc, wd, gffn, tm=ns, sample=True, nseq=Bd)
        zs3 = zs.reshape(Bd, SAMPLE_ROWS, Z_W)
        outs["ret_s"].append(rets)
        outs["gla_s"].append(glas)
        outs["k_s"].append(zs3[:, half:, Z_MK:Z_MK + W].reshape(Bd, T, MOBA_HEADS, MOBA_HD))
        outs["v_s"].append(zs3[:, half:, Z_MV:Z_MV + W].reshape(Bd, T, MOBA_HEADS, MOBA_HD))
        outs["conv_s"].append(tails.reshape(Bd, SAMPLE_ROWS, 2 * D_FF)[:, SAMPLE_ROWS - (CONV_W - 1):, :])

    yp = xp.reshape(B, S, D)
    ys = xs.reshape(Bd, SAMPLE_ROWS, D)[:, half:, :]
    kt, vt = _kv_out(outs["k_p"], nseq=B, seq_rows=S, tm=512)
    k_p = kt.reshape(B, depth, MOBA_HEADS, MOBA_HD, S).transpose(0, 4, 1, 2, 3)
    v_p = vt.reshape(B, depth, MOBA_HEADS, MOBA_HD, S).transpose(0, 4, 1, 2, 3)
    return (yp, ys,
            jnp.stack(outs["ret_p"], axis=0), jnp.stack(outs["ret_s"], axis=0),
            jnp.stack(outs["gla_p"], axis=0), jnp.stack(outs["gla_s"], axis=0),
            k_p, v_p,
            jnp.stack(outs["k_s"], axis=2), jnp.stack(outs["v_s"], axis=2),
            jnp.stack(outs["conv_p"], axis=0), jnp.stack(outs["conv_s"], axis=0))
```

```python
import functools

import numpy as np
import jax
import jax.numpy as jnp
from jax import lax
from jax.experimental import pallas as pl
from jax.experimental.pallas import tpu as pltpu

F32 = jnp.float32
BF16 = jnp.bfloat16

D_MODEL = 1024
BRANCH_W = 512
N_BRANCH = 3
RET_HEADS = 4
RET_DK = 64
RET_DV = 128
GLA_HEADS = 4
GLA_DK = 64
GLA_DV = 128
GLA_RANK = 16
GLA_TAU = 16.0
MOBA_HEADS = 8
MOBA_HD = 64
MOBA_BLOCK = 256
MOBA_TOPK = 3
ROPE_THETA = 10000.0
CHUNK = 64
SUBCHUNK = 16
SCAN_TILE = 512
D_FF = 2816
CONV_W = 3
EPS = 1e-6
PAGE_SIZE = 128

LANES = 128
SUBLANES = 8
VMEM_LIMIT = 56 * 1024 * 1024

Z_GL = 0
Z_RQ = 3072
Z_RK = 3328
Z_RV = 3584
Z_RG = 4096
Z_GQ = 4608
Z_GK = 4864
Z_GV = 5120
Z_GG = 5632
Z_MQ = 6144
Z_MK = 6656
Z_MV = 7168
Z_W = 7680
SEG_W = 512
SEG_RQK = Z_RQ // SEG_W
SEG_GQK = Z_GQ // SEG_W
SEG_MQ = Z_MQ // SEG_W
SEG_MK = Z_MK // SEG_W

SAMPLE_ROWS = 8
NEG_BIG = -1e30
MASK_BIG = 2.0 ** 100
LOG2E = 1.4426950408889634
MOBA_GROUP = 8

NT = (((1,), (1,)), ((), ()))
TN = (((0,), (0,)), ((), ()))


def _dot(a, b, dims=None, precision=None):
    if dims is None:
        return jnp.dot(a, b, preferred_element_type=F32, precision=precision)
    return lax.dot_general(a, b, dims, preferred_element_type=F32, precision=precision)


def _rms(x, g):
    return x * lax.rsqrt(jnp.mean(x * x, axis=-1, keepdims=True) + EPS) * g


def _silu(x):
    return x / (1.0 + jnp.exp(-x))


def _sigmoid(x):
    return 1.0 / (1.0 + jnp.exp(-x))


def _cparams(sem):
    return pltpu.CompilerParams(dimension_semantics=sem, vmem_limit_bytes=VMEM_LIMIT)


def _inproj_kernel(x_ref, g_ref, w_ref, wga_ref, w2_ref, ba_ref, cos_ref, sin_ref,
                   z_ref, la_ref, h_scr, *, sample):
    j = pl.program_id(1)

    @pl.when(j == 0)
    def _():
        hb = _rms(x_ref[...], g_ref[...]).astype(BF16)
        h_scr[...] = hb
        ga = _dot(hb, wga_ref[...])
        u = _dot(ga.astype(BF16), w2_ref[...]) + ba_ref[...]
        la = (jnp.minimum(u, 0.0) - jnp.log(1.0 + jnp.exp(-jnp.abs(u)))) * (1.0 / GLA_TAU)
        if sample:
            row = lax.broadcasted_iota(jnp.int32, (la.shape[0], 1), 0)
            la = jnp.where((row % SAMPLE_ROWS) >= SAMPLE_ROWS // 2, la, 0.0)
        la_ref[...] = la

    acc = _dot(h_scr[...], w_ref[...])
    lane = lax.broadcasted_iota(jnp.int32, (1, LANES), 1)
    first = (lane % MOBA_HD) < (MOBA_HD // 2)

    def rope(a):
        sw = jnp.where(first, pltpu.roll(a, LANES - MOBA_HD // 2, 1), pltpu.roll(a, MOBA_HD // 2, 1))
        return a * cos_ref[...] + sw * sin_ref[...]

    def store_rope(scales):
        for c, s in enumerate(scales):
            r = rope(acc[:, c * LANES:(c + 1) * LANES])
            z_ref[:, c * LANES:(c + 1) * LANES] = r if s == 1.0 else r * s

    ks = RET_DK ** -0.5

    @pl.when(j == SEG_RQK)
    def _():
        store_rope((1.0, 1.0, ks, ks))

    @pl.when(j == SEG_GQK)
    def _():
        z_ref[:, :SEG_W // 2] = acc[:, :SEG_W // 2]
        z_ref[:, SEG_W // 2:] = acc[:, SEG_W // 2:] * (GLA_DK ** -0.5)

    @pl.when(j == SEG_MQ)
    def _():
        store_rope((MOBA_HD ** -0.5,) * 4)

    @pl.when(j == SEG_MK)
    def _():
        store_rope((1.0,) * 4)

    plain = (j != SEG_RQK) & (j != SEG_GQK) & (j != SEG_MQ) & (j != SEG_MK)

    @pl.when(plain)
    def _():
        z_ref[...] = acc


def _inproj(x, g, w_main, w_ga, w2, ba, cos, sin, *, tm, sample):
    n = x.shape[0]
    nseg = Z_W // SEG_W
    ntab = cos.shape[0] // tm
    return pl.pallas_call(
        functools.partial(_inproj_kernel, sample=sample),
        grid=(n // tm, nseg),
        in_specs=[
            pl.BlockSpec((tm, D_MODEL), lambda i, j: (i, 0)),
            pl.BlockSpec((1, D_MODEL), lambda i, j: (0, 0)),
            pl.BlockSpec((D_MODEL, SEG_W), lambda i, j: (0, j)),
            pl.BlockSpec((D_MODEL, LANES), lambda i, j: (0, 0)),
            pl.BlockSpec((LANES, GLA_HEADS * GLA_DK), lambda i, j: (0, 0)),
            pl.BlockSpec((1, GLA_HEADS * GLA_DK), lambda i, j: (0, 0)),
            pl.BlockSpec((tm, LANES), lambda i, j: (i % ntab, 0)),
            pl.BlockSpec((tm, LANES), lambda i, j: (i % ntab, 0)),
        ],
        out_specs=[
            pl.BlockSpec((tm, SEG_W), lambda i, j: (i, j)),
            pl.BlockSpec((tm, GLA_HEADS * GLA_DK), lambda i, j: (i, 0)),
        ],
        out_shape=[jax.ShapeDtypeStruct((n, Z_W), F32),
                   jax.ShapeDtypeStruct((n, GLA_HEADS * GLA_DK), F32)],
        scratch_shapes=[pltpu.VMEM((tm, D_MODEL), BF16)],
        compiler_params=_cparams(("parallel", "arbitrary")),
        name="inproj",
    )(x, g, w_main, w_ga, w2, ba, cos, sin)


BNT = (((2,), (2,)), ((0,), (0,)))
BNN = (((2,), (1,)), ((0,), (0,)))


def _decay_tables(c, valid, cg):
    log_g = np.log1p(-np.exp2(-5.0 - np.arange(RET_HEADS, dtype=np.float64)))
    b = log_g[:, None] * np.cumsum(np.asarray(valid, np.float64))[None, :]
    diff = b[:, :, None] - b[:, None, :]
    causal = np.arange(c)[:, None] >= np.arange(c)[None, :]
    dmat = np.where(causal[None], np.exp(np.where(causal[None], diff, 0.0)), 0.0)
    qdec = np.broadcast_to(np.exp(b)[:, :, None], (RET_HEADS, c, LANES))
    kdec = np.broadcast_to(np.exp(b[:, -1:] - b)[:, :, None], (RET_HEADS, c, LANES))
    cdec = tuple(float(v) for v in np.exp(b[:, -1]))
    same = (np.arange(c)[:, None] // cg) == (np.arange(c)[None, :] // cg)
    tri = (causal & same).astype(np.float32)
    return (jnp.asarray(dmat, F32), jnp.asarray(qdec, F32), jnp.asarray(kdec, F32),
            jnp.asarray(tri, F32), cdec)


def _scan_kernel(*refs, T, C, SC, cdec, zero_init):
    if zero_init:
        (zr_ref, zg_ref, la_ref, dmat_ref, qdec_ref, kdec_ref, tri_ref, gr_ref, gg_ref,
         o_ref, sr_ref, sg_ref, Sr, SgT) = refs
    else:
        (zr_ref, zg_ref, la_ref, dmat_ref, qdec_ref, kdec_ref, tri_ref, gr_ref, gg_ref,
         s0r_ref, s0g_ref, o_ref, sr_ref, sg_ref, Sr, SgT) = refs
    t = pl.program_id(1)
    nc, nsub, nsc = T // C, C // SC, T // SC
    HK = RET_HEADS * RET_DK
    lane = lax.broadcasted_iota(jnp.int32, (1, LANES), 1)
    in_e = [lane < RET_DK, lane >= RET_DK]
    rowp = lax.broadcasted_iota(jnp.int32, (LANES, 1), 0)

    @pl.when(t == 0)
    def _():
        if zero_init:
            Sr[...] = jnp.zeros_like(Sr)
            SgT[...] = jnp.zeros_like(SgT)
        else:
            for p in range(RET_HEADS // 2):
                for e in range(2):
                    Sr[p, e * RET_DK:(e + 1) * RET_DK, :] = s0r_ref[0, 2 * p + e]
                SgT[p] = jnp.concatenate([s0g_ref[0, 2 * p].T, s0g_ref[0, 2 * p + 1].T], axis=1)


    for p in range(RET_HEADS // 2):
        qp = zr_ref[:, p * LANES:(p + 1) * LANES]
        kp = zr_ref[:, HK + p * LANES:HK + (p + 1) * LANES]
        kpb = kp.astype(BF16)
        s_old = Sr[p]
        s_new = s_old * jnp.where(rowp < RET_DK, cdec[2 * p], cdec[2 * p + 1])
        for e in range(2):
            h = 2 * p + e
            qh = jnp.where(in_e[e], qp, 0.0)
            vh = zr_ref[:, 2 * HK + h * RET_DV:2 * HK + (h + 1) * RET_DV].astype(BF16)
            att = _dot(qh.astype(BF16), kpb, NT) * dmat_ref[h]
            o = _dot(att.astype(BF16), vh) + _dot((qh * qdec_ref[h]).astype(BF16), s_old.astype(BF16))
            s_new = s_new + _dot((jnp.where(in_e[e], kp, 0.0) * kdec_ref[h]).T.astype(BF16), vh)
            cen = o - jnp.mean(o, axis=-1, keepdims=True)
            y = cen * lax.rsqrt(jnp.mean(cen * cen, axis=-1, keepdims=True) + EPS)
            rg = zr_ref[:, 2 * HK + BRANCH_W + h * RET_DV:2 * HK + BRANCH_W + (h + 1) * RET_DV]
            y = y * gr_ref[:, h * RET_DV:(h + 1) * RET_DV] * _silu(rg)
            o_ref[:, h * RET_DV:(h + 1) * RET_DV] = y.astype(o_ref.dtype)
        Sr[p] = s_new

    q = zg_ref[:, 0:HK]
    k = zg_ref[:, HK:2 * HK]
    b = _dot(tri_ref[...], la_ref[...], precision=lax.Precision.HIGHEST)
    q3, k3, b3 = (a.reshape(nc, C, HK) for a in (q, k, b))
    blast3 = b3[:, C - 1:C, :]
    qe3 = q3 * jnp.exp(b3)
    kd3 = k3 * jnp.exp(blast3 - b3)
    eblast3 = jnp.exp(blast3)
    q4, k4, b4 = (a.reshape(nsc, SC, HK) for a in (q, k, b))
    rsub4 = lax.broadcasted_iota(jnp.int32, (1, SC, 1), 1)
    rowT = lax.broadcasted_iota(jnp.int32, (T, 1), 0)
    sub0 = (rowT % C) - (rowT % SC)
    lane_c = lax.broadcasted_iota(jnp.int32, (1, C), 1)
    rc3 = lax.broadcasted_iota(jnp.int32, (1, C, 1), 1)
    in_e3 = [m.reshape(1, 1, LANES) for m in in_e]

    att = [jnp.zeros((T, C), F32) for _ in range(GLA_HEADS)]
    for jj in range(SC):
        w = q4 * k4[:, jj:jj + 1, :] * jnp.exp(jnp.where(rsub4 >= jj, b4 - b4[:, jj:jj + 1, :], NEG_BIG))
        hit = lane_c == sub0 + jj
        for h in range(GLA_HEADS):
            wp = w[:, :, (h // 2) * LANES:(h // 2 + 1) * LANES]
            col = jnp.sum(jnp.where(in_e3[h % 2], wp, 0.0), axis=-1, keepdims=True).reshape(T, 1)
            att[h] = jnp.where(hit, col, att[h])
    att3 = [a.reshape(nc, C, C) for a in att]
    for i in range(1, nsub):
        bi = b3[:, i * SC - 1:i * SC, :]
        in_i = (rc3 >= i * SC) & (rc3 < (i + 1) * SC)
        qi = q3 * jnp.exp(jnp.where(in_i, b3 - bi, NEG_BIG))
        ki = (k3 * jnp.exp(jnp.where(rc3 < i * SC, bi - b3, NEG_BIG))).astype(BF16)
        for h in range(GLA_HEADS):
            ps = slice((h // 2) * LANES, (h // 2 + 1) * LANES)
            qih = jnp.where(in_e3[h % 2], qi[:, :, ps], 0.0).astype(BF16)
            att3[h] = att3[h] + lax.dot_general(qih, ki[:, :, ps], BNT, preferred_element_type=F32)
    for p in range(GLA_HEADS // 2):
        ps = slice(p * LANES, (p + 1) * LANES)
        o_intra, qeb, ut3 = [], [], None
        for e in range(2):
            h = 2 * p + e
            v3 = zg_ref[:, 2 * HK + h * GLA_DV:2 * HK + (h + 1) * GLA_DV].reshape(nc, C, GLA_DV)
            o_intra.append(lax.dot_general(att3[h].astype(BF16), v3.astype(BF16), BNN, preferred_element_type=F32))
            qeb.append(jnp.where(in_e3[e], qe3[:, :, ps], 0.0).astype(BF16))
            u = lax.dot_general(jnp.swapaxes(v3, 1, 2).astype(BF16),
                                jnp.where(in_e3[e], kd3[:, :, ps], 0.0).astype(BF16), BNN,
                                preferred_element_type=F32)
            ut3 = u if ut3 is None else ut3 + u
        st = SgT[p]
        outs = [[], []]
        for c in range(nc):
            stb = st.astype(BF16)
            for e in range(2):
                outs[e].append(o_intra[e][c] + _dot(qeb[e][c], stb, NT))
            st = st * eblast3[c][:, ps] + ut3[c]
        SgT[p] = st
        for e in range(2):
            h = 2 * p + e
            o = outs[e][0] if nc == 1 else jnp.concatenate(outs[e], axis=0)
            gg = zg_ref[:, 2 * HK + BRANCH_W + h * GLA_DV:2 * HK + BRANCH_W + (h + 1) * GLA_DV]
            y = _rms(o, gg_ref[:, h * GLA_DV:(h + 1) * GLA_DV]) * _silu(gg)
            o_ref[:, BRANCH_W + h * GLA_DV:BRANCH_W + (h + 1) * GLA_DV] = y.astype(o_ref.dtype)

    @pl.when(t == pl.num_programs(1) - 1)
    def _():
        for p in range(RET_HEADS // 2):
            for e in range(2):
                sr_ref[0, 2 * p + e] = Sr[p, e * RET_DK:(e + 1) * RET_DK, :]
                sg_ref[0, 2 * p + e] = SgT[p][:, e * GLA_DK:(e + 1) * GLA_DK].T


def _scan(z, la, g_ret, g_gla, tables, s0, *, nseq, seq_rows, T, C, SC, out_dtype):
    dmat, qdec, kdec, tri, cdec = tables
    nT = seq_rows // T
    zero_init = s0 is None
    zw = 2 * RET_HEADS * RET_DK + 2 * BRANCH_W
    const3 = lambda s, t: (0, 0, 0)
    const2 = lambda s, t: (0, 0)
    in_specs = [
        pl.BlockSpec((T, zw), lambda s, t: (s * nT + t, Z_RQ // zw)),
        pl.BlockSpec((T, zw), lambda s, t: (s * nT + t, Z_GQ // zw)),
        pl.BlockSpec((T, GLA_HEADS * GLA_DK), lambda s, t: (s * nT + t, 0)),
        pl.BlockSpec(dmat.shape, const3),
        pl.BlockSpec(qdec.shape, const3),
        pl.BlockSpec(kdec.shape, const3),
        pl.BlockSpec(tri.shape, const2),
        pl.BlockSpec((1, BRANCH_W), const2),
        pl.BlockSpec((1, BRANCH_W), const2),
    ]
    args = [z, z, la, dmat, qdec, kdec, tri, g_ret, g_gla]
    st_spec = pl.BlockSpec((1, RET_HEADS, RET_DK, RET_DV), lambda s, t: (s, 0, 0, 0))
    if not zero_init:
        in_specs += [st_spec, st_spec]
        args += [s0[0], s0[1]]
    st_shape = jax.ShapeDtypeStruct((nseq, RET_HEADS, RET_DK, RET_DV), F32)
    return pl.pallas_call(
        functools.partial(_scan_kernel, T=T, C=C, SC=SC, cdec=cdec, zero_init=zero_init),
        grid=(nseq, nT),
        in_specs=in_specs,
        out_specs=[pl.BlockSpec((T, 2 * BRANCH_W), lambda s, t: (s * nT + t, 0)), st_spec, st_spec],
        out_shape=[jax.ShapeDtypeStruct((nseq * seq_rows, 2 * BRANCH_W), out_dtype), st_shape, st_shape],
        scratch_shapes=[pltpu.VMEM((RET_HEADS // 2, LANES, RET_DV), F32),
                        pltpu.VMEM((GLA_HEADS // 2, GLA_DV, LANES), F32)],
        compiler_params=_cparams(("parallel", "arbitrary")),
        name="scan",
    )(*args)


PAGES_PER_STEP = 16
PAGES_PER_BLOCK = MOBA_BLOCK // PAGE_SIZE
VT_ROWS = MOBA_HD + 16
ATTEND_HEADS = 2


def _kmean_step(page_refs, km_ref):
    for blk in range(PAGES_PER_STEP // PAGES_PER_BLOCK):
        tot = page_refs[blk * PAGES_PER_BLOCK][...]
        for p in range(1, PAGES_PER_BLOCK):
            tot = tot + page_refs[blk * PAGES_PER_BLOCK + p][...]
        km_ref[0, 0, :, :, :, blk:blk + 1] = jnp.sum(tot, axis=-1, keepdims=True) * (1.0 / MOBA_BLOCK)


def _moba_prompt_kernel(*refs, nblk, with_kmean):
    if with_kmean:
        q_ref, k_ref, v_ref = refs[1:4]
        o_ref, kmo_ref, km_scr, ka_scr, vt_scr = refs[4 + PAGES_PER_STEP:]
    else:
        q_ref, k_ref, v_ref, o_ref, km_scr, ka_scr, vt_scr = refs
    t = pl.program_id(2)
    BLK = MOBA_BLOCK
    G = MOBA_GROUP
    lane = lax.broadcasted_iota(jnp.int32, (1, LANES), 1)
    in_e = [lane < MOBA_HD, lane >= MOBA_HD]
    off = [MOBA_HD, 0]

    @pl.when(t == 0)
    def _():
        for blk in range(nblk):
            km_scr[blk:blk + 1, :] = jnp.mean(k_ref[blk * BLK:(blk + 1) * BLK, :], axis=0, keepdims=True)

        def prep(blk, c):
            r0 = pl.multiple_of(blk * BLK, BLK)
            kj = k_ref[pl.ds(r0, BLK), :]
            for e in range(2):
                ka_scr[e, pl.ds(r0, BLK), :] = jnp.where(
                    in_e[e], kj, jnp.where(lane == off[e] + blk, 1.0, 0.0)).astype(BF16)
            vt = v_ref[pl.ds(r0, BLK), :].T.astype(BF16)
            for e in range(2):
                vt_scr[e, 0:MOBA_HD, pl.ds(r0, BLK)] = vt[e * MOBA_HD:(e + 1) * MOBA_HD, :]
                vt_scr[e, MOBA_HD:VT_ROWS, pl.ds(r0, BLK)] = jnp.ones((VT_ROWS - MOBA_HD, BLK), BF16)
            return c
        lax.fori_loop(0, nblk, prep, 0)

    if with_kmean:
        _kmean_step(refs[4:4 + PAGES_PER_STEP], kmo_ref)
    q = q_ref[...]
    q2t = (q * LOG2E).T
    t0 = pl.multiple_of(t * BLK, BLK)
    blkrow = lax.broadcasted_iota(jnp.int32, (nblk, 1), 0)
    rowk = lax.broadcasted_iota(jnp.int32, (BLK, 1), 0)
    laneq = lax.broadcasted_iota(jnp.int32, (1, BLK), 1)
    zpad = jnp.zeros((MOBA_HD - nblk, BLK), F32)
    zhead = jnp.zeros((MOBA_HD, BLK), F32)
    waug, init = [], []
    for e in range(2):
        g = _dot(km_scr[...], jnp.where(in_e[e], q, 0.0), NT, precision=lax.Precision.HIGHEST)
        g = jnp.where(blkrow < t, g, -jnp.inf)
        sel = jnp.zeros(g.shape, jnp.bool_)
        for _ in range(MOBA_TOPK):
            m = jnp.max(g, axis=0, keepdims=True)
            idx = jnp.min(jnp.where(g == m, blkrow, nblk), axis=0, keepdims=True)
            pick = (blkrow == idx) & (m > -jnp.inf)
            sel = sel | pick
            g = jnp.where(pick, -jnp.inf, g)
        pen = jnp.where(sel, 0.0, -MASK_BIG)
        qe = q2t[e * MOBA_HD:(e + 1) * MOBA_HD, :]
        if e == 0:
            waug.append(jnp.concatenate([qe, pen, zpad], axis=0).astype(BF16))
            wown = jnp.concatenate([qe, zhead], axis=0).astype(BF16)
        else:
            waug.append(jnp.concatenate([pen, zpad, qe], axis=0).astype(BF16))
            wown = jnp.concatenate([zhead, qe], axis=0).astype(BF16)
        s = _dot(ka_scr[e, pl.ds(t0, BLK), :], wown)
        s = jnp.where(rowk <= laneq, s, NEG_BIG)
        m0 = jnp.max(s, axis=0, keepdims=True)
        init += [m0, _dot(vt_scr[e, :, pl.ds(t0, BLK)], jnp.exp2(s - m0).astype(BF16))]

    def body(gi, carry):
        r0 = pl.multiple_of(gi * (G * BLK), G * BLK)
        st = list(carry)
        ss = [[_dot(ka_scr[e, pl.ds(r0 + j * BLK, BLK), :], waug[e]) for j in range(G)] for e in range(2)]
        for j in range(G):
            for e in range(2):
                m, acc = st[2 * e:2 * e + 2]
                s = ss[e][j]
                mn = jnp.maximum(m, jnp.max(s, axis=0, keepdims=True))
                p = jnp.exp2(s - mn).astype(BF16)
                st[2 * e:2 * e + 2] = [mn, jnp.exp2(m - mn) * acc + _dot(vt_scr[e, :, pl.ds(r0 + j * BLK, BLK)], p)]
        return tuple(st)

    res = lax.fori_loop(0, (t + G - 1) // G, body, tuple(init))
    outs = [res[2 * e + 1][0:MOBA_HD] / res[2 * e + 1][MOBA_HD:MOBA_HD + 1] for e in range(2)]
    o_ref[...] = jnp.concatenate(outs, axis=0).T.astype(o_ref.dtype)


def _moba_prompt(z, *, nseq, seq_rows, kmean_src=None):
    nblk = seq_rows // MOBA_BLOCK
    npair = MOBA_HEADS // 2
    assert nblk % MOBA_GROUP == 0 and nblk <= MOBA_HD and nblk % SUBLANES == 0
    with_kmean = kmean_src is not None
    in_specs = [
        pl.BlockSpec((MOBA_BLOCK, LANES), lambda b, hp, t, *_: (b * nblk + t, Z_MQ // LANES + hp)),
        pl.BlockSpec((seq_rows, LANES), lambda b, hp, t, *_: (b, Z_MK // LANES + hp)),
        pl.BlockSpec((seq_rows, LANES), lambda b, hp, t, *_: (b, Z_MV // LANES + hp)),
    ]
    out_specs = [pl.BlockSpec((MOBA_BLOCK, LANES), lambda b, hp, t, *_: (b * nblk + t, hp))]
    out_shape = [jax.ShapeDtypeStruct((nseq * seq_rows, BRANCH_W), BF16)]
    args = [z, z, z]
    if with_kmean:
        cache_kt, pt_flat, nseq_s, npages = kmean_src
        depth = cache_kt.shape[1]
        bps = PAGES_PER_STEP // PAGES_PER_BLOCK
        nstep = npages // PAGES_PER_STEP
        assert nseq * npair * nblk == nseq_s * nstep

        def page_spec(i):
            def imap(b, hp, t, pt):
                return (pt[((b * npair + hp) * nblk + t) * PAGES_PER_STEP + i], 0, 0, 0, 0)
            return pl.BlockSpec((None, depth, MOBA_HEADS, MOBA_HD, PAGE_SIZE), imap)

        def km_map(b, hp, t, pt):
            s = (b * npair + hp) * nblk + t
            return (s // nstep, s % nstep, 0, 0, 0, 0)

        in_specs += [page_spec(i) for i in range(PAGES_PER_STEP)]
        args = [pt_flat] + args + [cache_kt] * PAGES_PER_STEP
        out_specs.append(pl.BlockSpec((1, 1, depth, MOBA_HEADS, MOBA_HD, bps), km_map))
        out_shape.append(jax.ShapeDtypeStruct((nseq_s, nstep, depth, MOBA_HEADS, MOBA_HD, bps), F32))
    grid_spec = pltpu.PrefetchScalarGridSpec(
        num_scalar_prefetch=1 if with_kmean else 0,
        grid=(nseq, npair, nblk),
        in_specs=in_specs,
        out_specs=out_specs,
        scratch_shapes=[pltpu.VMEM((nblk, LANES), F32),
                        pltpu.VMEM((2, seq_rows, LANES), BF16),
                        pltpu.VMEM((2, VT_ROWS, seq_rows), BF16)],
    )
    res = pl.pallas_call(
        functools.partial(_moba_prompt_kernel, nblk=nblk, with_kmean=with_kmean),
        grid_spec=grid_spec,
        out_shape=out_shape,
        compiler_params=_cparams(("parallel", "parallel", "arbitrary")),
        name="moba_prompt",
    )(*args)
    if not with_kmean:
        return res[0], None
    km = res[1].transpose(0, 2, 3, 4, 1, 5).reshape(nseq_s, depth, MOBA_HEADS, MOBA_HD, nstep * bps)
    return res[0], km


def _moba_topk_kernel(q_ref, km_ref, idx_ref):
    nfull = km_ref.shape[2]
    lane = lax.broadcasted_iota(jnp.int32, (1, LANES), 1)
    lane_b = lax.broadcasted_iota(jnp.int32, (1, nfull), 1)
    out = jnp.zeros((SAMPLE_ROWS, LANES), jnp.int32)
    for h in range(MOBA_HEADS):
        gsc = _dot(q_ref[h], km_ref[h], precision=lax.Precision.HIGHEST)
        for r in range(MOBA_TOPK):
            m = jnp.max(gsc, axis=-1, keepdims=True)
            idx = jnp.min(jnp.where(gsc == m, lane_b, nfull), axis=-1, keepdims=True)
            out = jnp.where(lane == h * MOBA_TOPK + r, idx, out)
            gsc = jnp.where(lane_b == idx, -jnp.inf, gsc)
    idx_ref[0] = out


def _moba_topk(qh, kmean, *, layer, nseq):
    nfull = kmean.shape[4]
    return pl.pallas_call(
        _moba_topk_kernel,
        grid=(nseq,),
        in_specs=[
            pl.BlockSpec((MOBA_HEADS, SAMPLE_ROWS, MOBA_HD), lambda b: (0, b, 0)),
            pl.BlockSpec((None, None, MOBA_HEADS, MOBA_HD, nfull), lambda b: (b, layer, 0, 0, 0)),
        ],
        out_specs=pl.BlockSpec((1, SAMPLE_ROWS, LANES), lambda b: (b, 0, 0)),
        out_shape=jax.ShapeDtypeStruct((nseq, SAMPLE_ROWS, LANES), jnp.int32),
        compiler_params=_cparams(("parallel",)),
        name="moba_topk",
    )(qh, kmean)


def _moba_attend_kernel(idx_ref, pt_ref, q_ref, kn_ref, vn_ref, ck_hbm, cv_hbm, o_ref, kbuf, vbuf, sem,
                        *, layer, npages, ntok):
    b = pl.program_id(0)
    hg = pl.program_id(1)
    ng = pl.num_programs(1)
    step = b * ng + hg
    nstep = pl.num_programs(0) * ng
    slot = step % 2
    per_head = ntok * MOBA_TOPK * PAGES_PER_BLOCK

    def copies(bb, gg, sl):
        out = []
        for e in range(ATTEND_HEADS):
            hh = gg * ATTEND_HEADS + e
            for t in range(ntok):
                for r in range(MOBA_TOPK):
                    blk = idx_ref[((bb * ntok + t) * MOBA_HEADS + hh) * MOBA_TOPK + r]
                    for p in range(PAGES_PER_BLOCK):
                        page = pt_ref[bb * npages + blk * PAGES_PER_BLOCK + p]
                        i = e * per_head + (t * MOBA_TOPK + r) * PAGES_PER_BLOCK + p
                        out.append(pltpu.make_async_copy(ck_hbm.at[page, layer, hh], kbuf.at[sl, i], sem.at[0, sl]))
                        out.append(pltpu.make_async_copy(cv_hbm.at[page, layer, hh], vbuf.at[sl, i], sem.at[1, sl]))
        return out

    @pl.when(step == 0)
    def _():
        for c in copies(b, hg, slot):
            c.start()

    @pl.when(step + 1 < nstep)
    def _():
        nxt = step + 1
        for c in copies(nxt // ng, nxt % ng, 1 - slot):
            c.start()

    for c in copies(b, hg, slot):
        c.wait()

    half = SAMPLE_ROWS // 2
    rowi = lax.broadcasted_iota(jnp.int32, (SAMPLE_ROWS, 1), 0)
    coln = lax.broadcasted_iota(jnp.int32, (1, SAMPLE_ROWS), 1)
    for e in range(ATTEND_HEADS):
        qb = q_ref[e].astype(BF16)
        vn = vn_ref[e].astype(BF16)
        s_new_all = _dot(qb, kn_ref[e].astype(BF16), NT)
        out = jnp.zeros((SAMPLE_ROWS, MOBA_HD), F32)
        for t in range(ntok):
            qrow = half + t
            s_new = jnp.where((coln >= half) & (coln <= qrow), s_new_all, NEG_BIG)
            ss, vs = [], []
            i0 = e * per_head + t * MOBA_TOPK * PAGES_PER_BLOCK
            for i in range(i0, i0 + MOBA_TOPK * PAGES_PER_BLOCK):
                ss.append(_dot(qb, kbuf[slot, i].astype(BF16)))
                vs.append(vbuf[slot, i].astype(BF16))
            m = jnp.max(s_new, axis=-1, keepdims=True)
            for s in ss:
                m = jnp.maximum(m, jnp.max(s, axis=-1, keepdims=True))
            p_new = jnp.exp(s_new - m)
            l = jnp.sum(p_new, axis=-1, keepdims=True)
            o = _dot(p_new.astype(BF16), vn)
            for s, vb in zip(ss, vs):
                p = jnp.exp(s - m)
                l = l + jnp.sum(p, axis=-1, keepdims=True)
                o = o + _dot(p.astype(BF16), vb, NT)
            out = jnp.where(rowi == qrow, o / l, out)
        o_ref[e] = out


def _moba_attend(qh, knh, vnh, cache_k, cache_v, idx_flat, pt_flat, *, layer, nseq, npages, ntok):
    nbuf = ATTEND_HEADS * ntok * MOBA_TOPK * PAGES_PER_BLOCK
    head_spec = pl.BlockSpec((ATTEND_HEADS, SAMPLE_ROWS, MOBA_HD), lambda b, g, idx, pt: (g, b, 0))
    grid_spec = pltpu.PrefetchScalarGridSpec(
        num_scalar_prefetch=2,
        grid=(nseq, MOBA_HEADS // ATTEND_HEADS),
        in_specs=[head_spec, head_spec, head_spec,
                  pl.BlockSpec(memory_space=pl.ANY), pl.BlockSpec(memory_space=pl.ANY)],
        out_specs=head_spec,
        scratch_shapes=[pltpu.VMEM((2, nbuf, MOBA_HD, PAGE_SIZE), F32),
                        pltpu.VMEM((2, nbuf, MOBA_HD, PAGE_SIZE), F32),
                        pltpu.SemaphoreType.DMA((2, 2))],
    )
    return pl.pallas_call(
        functools.partial(_moba_attend_kernel, layer=layer, npages=npages, ntok=ntok),
        grid_spec=grid_spec,
        out_shape=jax.ShapeDtypeStruct((MOBA_HEADS, nseq * SAMPLE_ROWS, MOBA_HD), F32),
        compiler_params=_cparams(("arbitrary", "arbitrary")),
        name="moba_attend",
    )(idx_flat, pt_flat, qh, knh, vnh, cache_k, cache_v)


def _post_kernel(x_ref, org_ref, om_ref, gl_ref, wb_ref, wo_ref, gpost_ref, gpre_ref, x1_ref, h2_ref):
    merged = None
    for n in range(N_BRANCH):
        if n < 2:
            br = org_ref[:, n * BRANCH_W:(n + 1) * BRANCH_W]
        else:
            br = om_ref[...]
        proj = _dot(br.astype(BF16), wb_ref[n])
        term = _sigmoid(gl_ref[:, n * D_MODEL:(n + 1) * D_MODEL]) * proj
        merged = term if merged is None else merged + term
    y = _dot(merged.astype(BF16), wo_ref[...])
    x1 = x_ref[...] + _rms(y, gpost_ref[...])
    x1_ref[...] = x1
    h2_ref[...] = _rms(x1, gpre_ref[...]).astype(BF16)


def _post(x, org, om, z, wb, wo, gpost, gpre, *, tm):
    n = x.shape[0]
    c2 = lambda i: (0, 0)
    return pl.pallas_call(
        _post_kernel,
        grid=(n // tm,),
        in_specs=[
            pl.BlockSpec((tm, D_MODEL), lambda i: (i, 0)),
            pl.BlockSpec((tm, 2 * BRANCH_W), lambda i: (i, 0)),
            pl.BlockSpec((tm, BRANCH_W), lambda i: (i, 0)),
            pl.BlockSpec((tm, N_BRANCH * D_MODEL), lambda i: (i, 0)),
            pl.BlockSpec((N_BRANCH, BRANCH_W, D_MODEL), lambda i: (0, 0, 0)),
            pl.BlockSpec((D_MODEL, D_MODEL), c2),
            pl.BlockSpec((1, D_MODEL), c2),
            pl.BlockSpec((1, D_MODEL), c2),
        ],
        out_specs=[pl.BlockSpec((tm, D_MODEL), lambda i: (i, 0)),
                   pl.BlockSpec((tm, D_MODEL), lambda i: (i, 0))],
        out_shape=[jax.ShapeDtypeStruct((n, D_MODEL), F32), jax.ShapeDtypeStruct((n, D_MODEL), BF16)],
        compiler_params=_cparams(("parallel",)),
        name="post",
    )(x, org, om, z, wb, wo, gpost, gpre)


FF_CHUNK = 256


def _ffn_kernel(*refs, sample, tiles_per_seq, tail_rows):
    if sample:
        (x1_ref, h2_ref, st_ref, wup_ref, wc_ref, bc_ref, wd_ref, g_ref, x2_ref, tail_ref, act_scr) = refs
    else:
        (x1_ref, h2_ref, halo_ref, wup_ref, wc_ref, bc_ref, wd_ref, g_ref, x2_ref, tail_ref, act_scr) = refs
    i = pl.program_id(0)
    tm = x1_ref.shape[0]
    h2 = h2_ref[...]
    row = lax.broadcasted_iota(jnp.int32, (tm, 1), 0)
    row8 = lax.broadcasted_iota(jnp.int32, (SUBLANES, 1), 0)
    if not sample:
        halo = jnp.where((i % tiles_per_seq) == 0, jnp.zeros_like(halo_ref[...]), halo_ref[...])

    def shifted(u, uh, sh):
        us = pltpu.roll(u, sh, 0)
        if uh is None:
            return us
        hs = pltpu.roll(uh, sh, 0)
        head = jnp.where(row8 < sh, hs, us[:SUBLANES])
        return jnp.concatenate([head, us[SUBLANES:]], axis=0)

    for c in range(D_FF // FF_CHUNK):
        halves = []
        for half in range(2):
            c0 = half * D_FF + c * FF_CHUNK
            w = wup_ref[:, c0:c0 + FF_CHUNK]
            u = _dot(h2, w)
            if sample:
                inj = ((row % SAMPLE_ROWS) >= SAMPLE_ROWS // 2 - (CONV_W - 1)) & ((row % SAMPLE_ROWS) < SAMPLE_ROWS // 2)
                u = jnp.where(inj, st_ref[:, c0:c0 + FF_CHUNK], u)
                uh = None
            else:
                uh = _dot(halo, w)
            tail_ref[0, :, c0:c0 + FF_CHUNK] = u[tm - tail_rows:]
            conv = bc_ref[:, c0:c0 + FF_CHUNK] + u * wc_ref[CONV_W - 1:CONV_W, c0:c0 + FF_CHUNK]
            for sh in range(1, CONV_W):
                conv = conv + shifted(u, uh, sh) * wc_ref[CONV_W - 1 - sh:CONV_W - sh, c0:c0 + FF_CHUNK]
            halves.append(conv)
        a, b = halves
        gelu = 0.5 * a * (1.0 + jnp.tanh(0.7978845608028654 * (a + 0.044715 * a * a * a)))
        act_scr[:, c * FF_CHUNK:(c + 1) * FF_CHUNK] = (gelu * b).astype(BF16)
    x2 = x1_ref[...] + _rms(_dot(act_scr[...], wd_ref[...]), g_ref[...])
    if sample:
        x2 = jnp.where((row % SAMPLE_ROWS) >= SAMPLE_ROWS // 2, x2, 0.0)
    x2_ref[...] = x2


def _ffn(x1, h2, extra, wup, wc, bc, wd, g, *, tm, sample, nseq):
    n = x1.shape[0]
    ntile = n // tm
    tiles_per_seq = ntile // nseq if not sample else 1
    tail_rows = tm if sample else SUBLANES
    ntail = ntile if sample else nseq
    c2 = lambda i: (0, 0)
    if sample:
        extra_spec = pl.BlockSpec((tm, 2 * D_FF), lambda i: (i, 0))
        extra_arg = extra
    else:
        hb = tm // SUBLANES
        extra_spec = pl.BlockSpec((SUBLANES, D_MODEL), lambda i: (jnp.maximum(i * hb - 1, 0), 0))
        extra_arg = h2
    return pl.pallas_call(
        functools.partial(_ffn_kernel, sample=sample, tiles_per_seq=tiles_per_seq, tail_rows=tail_rows),
        grid=(ntile,),
        in_specs=[
            pl.BlockSpec((tm, D_MODEL), lambda i: (i, 0)),
            pl.BlockSpec((tm, D_MODEL), lambda i: (i, 0)),
            extra_spec,
            pl.BlockSpec((D_MODEL, 2 * D_FF), c2, pipeline_mode=pl.Buffered(1)),
            pl.BlockSpec((CONV_W, 2 * D_FF), c2),
            pl.BlockSpec((1, 2 * D_FF), c2),
            pl.BlockSpec((D_FF, D_MODEL), c2, pipeline_mode=pl.Buffered(1)),
            pl.BlockSpec((1, D_MODEL), c2),
        ],
        out_specs=[pl.BlockSpec((tm, D_MODEL), lambda i: (i, 0)),
                   pl.BlockSpec((1, tail_rows, 2 * D_FF), lambda i: (i // tiles_per_seq, 0, 0))],
        out_shape=[jax.ShapeDtypeStruct((n, D_MODEL), F32),
                   jax.ShapeDtypeStruct((ntail, tail_rows, 2 * D_FF), F32)],
        scratch_shapes=[pltpu.VMEM((tm, D_FF), BF16)],
        compiler_params=_cparams(("arbitrary",)),
        name="ffn",
    )(x1, h2, extra_arg, wup, wc, bc, wd, g)


def _kv_out_kernel(*refs, depth):
    z_refs, (kt_ref, vt_ref) = refs[:2 * depth], refs[2 * depth:]
    for l in range(depth):
        kt_ref[l] = z_refs[2 * l][...].T
        vt_ref[l] = z_refs[2 * l + 1][...].T


def _kv_out(zs, *, nseq, seq_rows, tm):
    depth = len(zs)
    W = MOBA_HEADS * MOBA_HD
    nT = seq_rows // tm
    in_specs, args = [], []
    for z in zs:
        in_specs += [pl.BlockSpec((tm, W), lambda b, i: (b * nT + i, Z_MK // W)),
                     pl.BlockSpec((tm, W), lambda b, i: (b * nT + i, Z_MV // W))]
        args += [z, z]
    out_spec = pl.BlockSpec((None, depth, W, tm), lambda b, i: (b, 0, 0, i))
    shape = jax.ShapeDtypeStruct((nseq, depth, W, seq_rows), F32)
    return pl.pallas_call(
        functools.partial(_kv_out_kernel, depth=depth),
        grid=(nseq, nT),
        in_specs=in_specs,
        out_specs=[out_spec, out_spec],
        out_shape=[shape, shape],
        compiler_params=_cparams(("parallel", "parallel")),
        name="kv_out",
    )(*args)


def _rope_tables(pos):
    half = MOBA_HD // 2
    freqs = jnp.power(ROPE_THETA, -jnp.arange(half, dtype=F32) / half)
    ang = pos.astype(F32)[:, None] * freqs[None, :]
    cos, sin = jnp.cos(ang), jnp.sin(ang)
    return (jnp.concatenate([cos, cos, cos, cos], axis=1),
            jnp.concatenate([-sin, sin, -sin, sin], axis=1))


def kernel(x_prompt, x_sample, cache_k, cache_v, page_table, state_ret, state_gla, state_conv, g_pre_mix, w_in, w_gla_a2, b_gla_a, g_ret_norm, g_gla_norm, w_branch, w_out, g_post_mix, g_pre_ffn, w_up, w_conv, b_conv, w_down, g_post_ffn):
    B, S, D = x_prompt.shape
    Bd, T, _ = x_sample.shape
    depth = w_in.shape[0]
    npages = page_table.shape[1]
    past_len = npages * PAGE_SIZE
    half = SAMPLE_ROWS // 2
    assert D == D_MODEL and T == half and S % 2048 == 0
    assert past_len % MOBA_BLOCK == 0 and npages % PAGES_PER_STEP == 0
    assert past_len // MOBA_BLOCK >= MOBA_TOPK
    n_pool = cache_k.shape[0]
    W = MOBA_HEADS * MOBA_HD

    xp = x_prompt.reshape(B * S, D)
    xs = jnp.pad(x_sample, ((0, 0), (half, 0), (0, 0))).reshape(Bd * SAMPLE_ROWS, D)
    ns = Bd * SAMPLE_ROWS

    cos_p, sin_p = _rope_tables(jnp.arange(S, dtype=jnp.int32))
    pos_s = past_len + jnp.tile(jnp.arange(SAMPLE_ROWS, dtype=jnp.int32) - half, Bd)
    cos_s, sin_s = _rope_tables(pos_s)

    tab_p = _decay_tables(SCAN_TILE, np.ones(SCAN_TILE), CHUNK)
    tab_s = _decay_tables(SAMPLE_ROWS, np.concatenate([np.zeros(half), np.ones(half)]), SAMPLE_ROWS)

    pt_flat = page_table.reshape(-1).astype(jnp.int32)
    cache_kt = cache_k.transpose(0, 2, 3, 4, 1)
    cache_vt = cache_v.transpose(0, 2, 3, 4, 1)
    kmean = None

    def per_head(a):
        return a.reshape(a.shape[0], MOBA_HEADS, MOBA_HD).transpose(1, 0, 2)

    outs = {k: [] for k in ("ret_p", "ret_s", "gla_p", "gla_s", "k_p", "k_s", "v_s", "conv_p", "conv_s")}
    for l in range(depth):
        w = w_in[l]
        o_ga = 2 * RET_HEADS * RET_DK + 2 * BRANCH_W + 2 * GLA_HEADS * GLA_DK + 2 * BRANCH_W
        o_m = o_ga + GLA_RANK
        o_gl = o_m + 3 * BRANCH_W
        w_main = jnp.concatenate([w[:, o_gl:], w[:, :o_ga], w[:, o_m:o_gl]], axis=1).astype(BF16)
        w_ga = jnp.pad(w[:, o_ga:o_m], ((0, 0), (0, LANES - GLA_RANK))).astype(BF16)
        w2 = jnp.pad(w_gla_a2[l], ((0, LANES - GLA_RANK), (0, 0))).astype(BF16)
        ba = b_gla_a[l][None, :]
        g_pre = g_pre_mix[l][None, :]
        g_ret = g_ret_norm[l][None, :]
        g_gla = g_gla_norm[l][None, :]
        wb = w_branch[l].astype(BF16)
        wo = w_out[l].astype(BF16)
        gpost = g_post_mix[l][None, :]
        gpre2 = g_pre_ffn[l][None, :]
        wup = w_up[l].astype(BF16)
        wd = w_down[l].astype(BF16)
        wc = w_conv[l]
        bc = b_conv[l][None, :]
        gffn = g_post_ffn[l][None, :]

        zp, lap = _inproj(xp, g_pre, w_main, w_ga, w2, ba, cos_p, sin_p, tm=2048, sample=False)
        orgp, retp, glap = _scan(zp, lap, g_ret, g_gla, tab_p, None, nseq=B, seq_rows=S, T=SCAN_TILE,
                                 C=CHUNK, SC=SUBCHUNK, out_dtype=BF16)
        if kmean is None:
            omp, kmean = _moba_prompt(zp, nseq=B, seq_rows=S, kmean_src=(cache_kt, pt_flat, Bd, npages))
        else:
            omp, _ = _moba_prompt(zp, nseq=B, seq_rows=S)
        x1p, h2p = _post(xp, orgp, omp, zp, wb, wo, gpost, gpre2, tm=512)
        xp, tailp = _ffn(x1p, h2p, None, wup, wc, bc, wd, gffn, tm=1024, sample=False, nseq=B)
        outs["ret_p"].append(retp)
        outs["gla_p"].append(glap)
        outs["k_p"].append(zp)
        outs["conv_p"].append(tailp[:, SUBLANES - (CONV_W - 1):, :])

        zs, las = _inproj(xs, g_pre, w_main, w_ga, w2, ba, cos_s, sin_s, tm=ns, sample=True)
        orgs, rets, glas = _scan(zs, las, g_ret, g_gla, tab_s, (state_ret[l], state_gla[l]), nseq=Bd,
                                 seq_rows=SAMPLE_ROWS, T=SAMPLE_ROWS, C=SAMPLE_ROWS, SC=SAMPLE_ROWS,
                                 out_dtype=F32)
        qh = per_head(zs[:, Z_MQ:Z_MQ + W])
        idx = _moba_topk(qh, kmean, layer=l, nseq=Bd)
        idx_flat = idx[:, half:, :MOBA_HEADS * MOBA_TOPK].reshape(-1)
        oh = _moba_attend(qh, per_head(zs[:, Z_MK:Z_MK + W]), per_head(zs[:, Z_MV:Z_MV + W]), cache_kt, cache_vt,
                          idx_flat, pt_flat, layer=l, nseq=Bd, npages=npages, ntok=T)
        oms = oh.transpose(1, 0, 2).reshape(ns, W)
        x1s, h2s = _post(xs, orgs, oms, zs, wb, wo, gpost, gpre2, tm=ns)
        st = jnp.pad(state_conv[l], ((0, 0), (half - (CONV_W - 1), half), (0, 0))).reshape(ns, 2 * D_FF)
        xs, tails = _ffn(x1s, h2s, st, wup, wc, bc, wd, gffn, tm=ns, sample=True, nseq=Bd)
        zs3 = zs.reshape(Bd, SAMPLE_ROWS, Z_W)
        outs["ret_s"].append(rets)
        outs["gla_s"].append(glas)
        outs["k_s"].append(zs3[:, half:, Z_MK:Z_MK + W].reshape(Bd, T, MOBA_HEADS, MOBA_HD))
        outs["v_s"].append(zs3[:, half:, Z_MV:Z_MV + W].reshape(Bd, T, MOBA_HEADS, MOBA_HD))
        outs["conv_s"].append(tails.reshape(Bd, SAMPLE_ROWS, 2 * D_FF)[:, SAMPLE_ROWS - (CONV_W - 1):, :])

    yp = xp.reshape(B, S, D)
    ys = xs.reshape(Bd, SAMPLE_ROWS, D)[:, half:, :]
    kt, vt = _kv_out(outs["k_p"], nseq=B, seq_rows=S, tm=512)
    k_p = kt.reshape(B, depth, MOBA_HEADS, MOBA_HD, S).transpose(0, 4, 1, 2, 3)
    v_p = vt.reshape(B, depth, MOBA_HEADS, MOBA_HD, S).transpose(0, 4, 1, 2, 3)
    return (yp, ys,
            jnp.stack(outs["ret_p"], axis=0), jnp.stack(outs["ret_s"], axis=0),
            jnp.stack(outs["gla_p"], axis=0), jnp.stack(outs["gla_s"], axis=0),
            k_p, v_p,
            jnp.stack(outs["k_s"], axis=2), jnp.stack(outs["v_s"], axis=2),
            jnp.stack(outs["conv_p"], axis=0), jnp.stack(outs["conv_s"], axis=0))
```

```python
import functools

import numpy as np
import jax
import jax.numpy as jnp
from jax import lax
from jax.experimental import pallas as pl
from jax.experimental.pallas import tpu as pltpu

F32 = jnp.float32
BF16 = jnp.bfloat16

D_MODEL = 1024
BRANCH_W = 512
N_BRANCH = 3
RET_HEADS = 4
RET_DK = 64
RET_DV = 128
GLA_HEADS = 4
GLA_DK = 64
GLA_DV = 128
GLA_RANK = 16
GLA_TAU = 16.0
MOBA_HEADS = 8
MOBA_HD = 64
MOBA_BLOCK = 256
MOBA_TOPK = 3
ROPE_THETA = 10000.0
CHUNK = 64
SUBCHUNK = 16
SCAN_TILE = 512
D_FF = 2816
CONV_W = 3
EPS = 1e-6
PAGE_SIZE = 128

LANES = 128
SUBLANES = 8
VMEM_LIMIT = 56 * 1024 * 1024

Z_GL = 0
Z_RQ = 3072
Z_RK = 3328
Z_RV = 3584
Z_RG = 4096
Z_GQ = 4608
Z_GK = 4864
Z_GV = 5120
Z_GG = 5632
Z_MQ = 6144
Z_MK = 6656
Z_MV = 7168
Z_W = 7680
SEG_W = 512
SEG_RQK = Z_RQ // SEG_W
SEG_GQK = Z_GQ // SEG_W
SEG_MQ = Z_MQ // SEG_W
SEG_MK = Z_MK // SEG_W

SAMPLE_ROWS = 8
NEG_BIG = -1e30
MASK_BIG = 2.0 ** 100
LOG2E = 1.4426950408889634
MOBA_GROUP = 8

NT = (((1,), (1,)), ((), ()))
TN = (((0,), (0,)), ((), ()))


def _dot(a, b, dims=None, precision=None):
    if dims is None:
        return jnp.dot(a, b, preferred_element_type=F32, precision=precision)
    return lax.dot_general(a, b, dims, preferred_element_type=F32, precision=precision)


def _rms(x, g):
    return x * lax.rsqrt(jnp.mean(x * x, axis=-1, keepdims=True) + EPS) * g


def _silu(x):
    return x / (1.0 + jnp.exp(-x))


def _sigmoid(x):
    return 1.0 / (1.0 + jnp.exp(-x))


def _cparams(sem):
    return pltpu.CompilerParams(dimension_semantics=sem, vmem_limit_bytes=VMEM_LIMIT)


def _inproj_kernel(x_ref, g_ref, w_ref, wga_ref, w2_ref, ba_ref, cos_ref, sin_ref,
                   z_ref, la_ref, h_scr, *, sample):
    j = pl.program_id(1)

    @pl.when(j == 0)
    def _():
        hb = _rms(x_ref[...], g_ref[...]).astype(BF16)
        h_scr[...] = hb
        ga = _dot(hb, wga_ref[...])
        u = _dot(ga.astype(BF16), w2_ref[...]) + ba_ref[...]
        la = (jnp.minimum(u, 0.0) - jnp.log(1.0 + jnp.exp(-jnp.abs(u)))) * (1.0 / GLA_TAU)
        if sample:
            row = lax.broadcasted_iota(jnp.int32, (la.shape[0], 1), 0)
            la = jnp.where((row % SAMPLE_ROWS) >= SAMPLE_ROWS // 2, la, 0.0)
        la_ref[...] = la

    acc = _dot(h_scr[...], w_ref[...])
    lane = lax.broadcasted_iota(jnp.int32, (1, LANES), 1)
    first = (lane % MOBA_HD) < (MOBA_HD // 2)

    def rope(a):
        sw = jnp.where(first, pltpu.roll(a, LANES - MOBA_HD // 2, 1), pltpu.roll(a, MOBA_HD // 2, 1))
        return a * cos_ref[...] + sw * sin_ref[...]

    def store_rope(scales):
        for c, s in enumerate(scales):
            r = rope(acc[:, c * LANES:(c + 1) * LANES])
            z_ref[:, c * LANES:(c + 1) * LANES] = r if s == 1.0 else r * s

    ks = RET_DK ** -0.5

    @pl.when(j == SEG_RQK)
    def _():
        store_rope((1.0, 1.0, ks, ks))

    @pl.when(j == SEG_GQK)
    def _():
        z_ref[:, :SEG_W // 2] = acc[:, :SEG_W // 2]
        z_ref[:, SEG_W // 2:] = acc[:, SEG_W // 2:] * (GLA_DK ** -0.5)

    @pl.when(j == SEG_MQ)
    def _():
        store_rope((MOBA_HD ** -0.5,) * 4)

    @pl.when(j == SEG_MK)
    def _():
        store_rope((1.0,) * 4)

    plain = (j != SEG_RQK) & (j != SEG_GQK) & (j != SEG_MQ) & (j != SEG_MK)

    @pl.when(plain)
    def _():
        z_ref[...] = acc


def _inproj(x, g, w_main, w_ga, w2, ba, cos, sin, *, tm, sample):
    n = x.shape[0]
    nseg = Z_W // SEG_W
    ntab = cos.shape[0] // tm
    return pl.pallas_call(
        functools.partial(_inproj_kernel, sample=sample),
        grid=(n // tm, nseg),
        in_specs=[
            pl.BlockSpec((tm, D_MODEL), lambda i, j: (i, 0)),
            pl.BlockSpec((1, D_MODEL), lambda i, j: (0, 0)),
            pl.BlockSpec((D_MODEL, SEG_W), lambda i, j: (0, j)),
            pl.BlockSpec((D_MODEL, LANES), lambda i, j: (0, 0)),
            pl.BlockSpec((LANES, GLA_HEADS * GLA_DK), lambda i, j: (0, 0)),
            pl.BlockSpec((1, GLA_HEADS * GLA_DK), lambda i, j: (0, 0)),
            pl.BlockSpec((tm, LANES), lambda i, j: (i % ntab, 0)),
            pl.BlockSpec((tm, LANES), lambda i, j: (i % ntab, 0)),
        ],
        out_specs=[
            pl.BlockSpec((tm, SEG_W), lambda i, j: (i, j)),
            pl.BlockSpec((tm, GLA_HEADS * GLA_DK), lambda i, j: (i, 0)),
        ],
        out_shape=[jax.ShapeDtypeStruct((n, Z_W), F32),
                   jax.ShapeDtypeStruct((n, GLA_HEADS * GLA_DK), F32)],
        scratch_shapes=[pltpu.VMEM((tm, D_MODEL), BF16)],
        compiler_params=_cparams(("parallel", "arbitrary")),
        name="inproj",
    )(x, g, w_main, w_ga, w2, ba, cos, sin)


BNT = (((2,), (2,)), ((0,), (0,)))
BNN = (((2,), (1,)), ((0,), (0,)))


def _decay_tables(c, valid, cg):
    log_g = np.log1p(-np.exp2(-5.0 - np.arange(RET_HEADS, dtype=np.float64)))
    b = log_g[:, None] * np.cumsum(np.asarray(valid, np.float64))[None, :]
    diff = b[:, :, None] - b[:, None, :]
    causal = np.arange(c)[:, None] >= np.arange(c)[None, :]
    dmat = np.where(causal[None], np.exp(np.where(causal[None], diff, 0.0)), 0.0)
    qdec = np.broadcast_to(np.exp(b)[:, :, None], (RET_HEADS, c, LANES))
    kdec = np.broadcast_to(np.exp(b[:, -1:] - b)[:, :, None], (RET_HEADS, c, LANES))
    cdec = tuple(float(v) for v in np.exp(b[:, -1]))
    same = (np.arange(c)[:, None] // cg) == (np.arange(c)[None, :] // cg)
    tri = (causal & same).astype(np.float32)
    return (jnp.asarray(dmat, F32), jnp.asarray(qdec, F32), jnp.asarray(kdec, F32),
            jnp.asarray(tri, F32), cdec)


def _scan_kernel(*refs, T, C, SC, cdec, zero_init):
    if zero_init:
        (zr_ref, zg_ref, la_ref, dmat_ref, qdec_ref, kdec_ref, tri_ref, gr_ref, gg_ref,
         o_ref, sr_ref, sg_ref, Sr, SgT) = refs
    else:
        (zr_ref, zg_ref, la_ref, dmat_ref, qdec_ref, kdec_ref, tri_ref, gr_ref, gg_ref,
         s0r_ref, s0g_ref, o_ref, sr_ref, sg_ref, Sr, SgT) = refs
    t = pl.program_id(1)
    nc, nsub, nsc = T // C, C // SC, T // SC
    HK = RET_HEADS * RET_DK
    lane = lax.broadcasted_iota(jnp.int32, (1, LANES), 1)
    in_e = [lane < RET_DK, lane >= RET_DK]
    rowp = lax.broadcasted_iota(jnp.int32, (LANES, 1), 0)

    @pl.when(t == 0)
    def _():
        if zero_init:
            Sr[...] = jnp.zeros_like(Sr)
            SgT[...] = jnp.zeros_like(SgT)
        else:
            for p in range(RET_HEADS // 2):
                for e in range(2):
                    Sr[p, e * RET_DK:(e + 1) * RET_DK, :] = s0r_ref[0, 2 * p + e]
                SgT[p] = jnp.concatenate([s0g_ref[0, 2 * p].T, s0g_ref[0, 2 * p + 1].T], axis=1)


    for p in range(RET_HEADS // 2):
        qp = zr_ref[:, p * LANES:(p + 1) * LANES]
        kp = zr_ref[:, HK + p * LANES:HK + (p + 1) * LANES]
        kpb = kp.astype(BF16)
        s_old = Sr[p]
        s_new = s_old * jnp.where(rowp < RET_DK, cdec[2 * p], cdec[2 * p + 1])
        for e in range(2):
            h = 2 * p + e
            qh = jnp.where(in_e[e], qp, 0.0)
            vh = zr_ref[:, 2 * HK + h * RET_DV:2 * HK + (h + 1) * RET_DV].astype(BF16)
            att = _dot(qh.astype(BF16), kpb, NT) * dmat_ref[h]
            o = _dot(att.astype(BF16), vh) + _dot((qh * qdec_ref[h]).astype(BF16), s_old.astype(BF16))
            s_new = s_new + _dot((jnp.where(in_e[e], kp, 0.0) * kdec_ref[h]).T.astype(BF16), vh)
            cen = o - jnp.mean(o, axis=-1, keepdims=True)
            y = cen * lax.rsqrt(jnp.mean(cen * cen, axis=-1, keepdims=True) + EPS)
            rg = zr_ref[:, 2 * HK + BRANCH_W + h * RET_DV:2 * HK + BRANCH_W + (h + 1) * RET_DV]
            y = y * gr_ref[:, h * RET_DV:(h + 1) * RET_DV] * _silu(rg)
            o_ref[:, h * RET_DV:(h + 1) * RET_DV] = y.astype(o_ref.dtype)
        Sr[p] = s_new

    q = zg_ref[:, 0:HK]
    k = zg_ref[:, HK:2 * HK]
    b = _dot(tri_ref[...], la_ref[...], precision=lax.Precision.HIGHEST)
    q3, k3, b3 = (a.reshape(nc, C, HK) for a in (q, k, b))
    blast3 = b3[:, C - 1:C, :]
    qe3 = q3 * jnp.exp(b3)
    kd3 = k3 * jnp.exp(blast3 - b3)
    eblast3 = jnp.exp(blast3)
    q4, k4, b4 = (a.reshape(nsc, SC, HK) for a in (q, k, b))
    rsub4 = lax.broadcasted_iota(jnp.int32, (1, SC, 1), 1)
    rowT = lax.broadcasted_iota(jnp.int32, (T, 1), 0)
    sub0 = (rowT % C) - (rowT % SC)
    lane_c = lax.broadcasted_iota(jnp.int32, (1, C), 1)
    rc3 = lax.broadcasted_iota(jnp.int32, (1, C, 1), 1)
    in_e3 = [m.reshape(1, 1, LANES) for m in in_e]

    att = [jnp.zeros((T, C), F32) for _ in range(GLA_HEADS)]
    for jj in range(SC):
        w = q4 * k4[:, jj:jj + 1, :] * jnp.exp(jnp.where(rsub4 >= jj, b4 - b4[:, jj:jj + 1, :], NEG_BIG))
        hit = lane_c == sub0 + jj
        for h in range(GLA_HEADS):
            wp = w[:, :, (h // 2) * LANES:(h // 2 + 1) * LANES]
            col = jnp.sum(jnp.where(in_e3[h % 2], wp, 0.0), axis=-1, keepdims=True).reshape(T, 1)
            att[h] = jnp.where(hit, col, att[h])
    att3 = [a.reshape(nc, C, C) for a in att]
    for i in range(1, nsub):
        bi = b3[:, i * SC - 1:i * SC, :]
        in_i = (rc3 >= i * SC) & (rc3 < (i + 1) * SC)
        qi = q3 * jnp.exp(jnp.where(in_i, b3 - bi, NEG_BIG))
        ki = (k3 * jnp.exp(jnp.where(rc3 < i * SC, bi - b3, NEG_BIG))).astype(BF16)
        for h in range(GLA_HEADS):
            ps = slice((h // 2) * LANES, (h // 2 + 1) * LANES)
            qih = jnp.where(in_e3[h % 2], qi[:, :, ps], 0.0).astype(BF16)
            att3[h] = att3[h] + lax.dot_general(qih, ki[:, :, ps], BNT, preferred_element_type=F32)
    for p in range(GLA_HEADS // 2):
        ps = slice(p * LANES, (p + 1) * LANES)
        o_intra, qeb, ut3 = [], [], None
        for e in range(2):
            h = 2 * p + e
            v3 = zg_ref[:, 2 * HK + h * GLA_DV:2 * HK + (h + 1) * GLA_DV].reshape(nc, C, GLA_DV)
            o_intra.append(lax.dot_general(att3[h].astype(BF16), v3.astype(BF16), BNN, preferred_element_type=F32))
            qeb.append(jnp.where(in_e3[e], qe3[:, :, ps], 0.0).astype(BF16))
            u = lax.dot_general(jnp.swapaxes(v3, 1, 2).astype(BF16),
                                jnp.where(in_e3[e], kd3[:, :, ps], 0.0).astype(BF16), BNN,
                                preferred_element_type=F32)
            ut3 = u if ut3 is None else ut3 + u
        st = SgT[p]
        outs = [[], []]
        for c in range(nc):
            stb = st.astype(BF16)
            for e in range(2):
                outs[e].append(o_intra[e][c] + _dot(qeb[e][c], stb, NT))
            st = st * eblast3[c][:, ps] + ut3[c]
        SgT[p] = st
        for e in range(2):
            h = 2 * p + e
            o = outs[e][0] if nc == 1 else jnp.concatenate(outs[e], axis=0)
            gg = zg_ref[:, 2 * HK + BRANCH_W + h * GLA_DV:2 * HK + BRANCH_W + (h + 1) * GLA_DV]
            y = _rms(o, gg_ref[:, h * GLA_DV:(h + 1) * GLA_DV]) * _silu(gg)
            o_ref[:, BRANCH_W + h * GLA_DV:BRANCH_W + (h + 1) * GLA_DV] = y.astype(o_ref.dtype)

    @pl.when(t == pl.num_programs(1) - 1)
    def _():
        for p in range(RET_HEADS // 2):
            for e in range(2):
                sr_ref[0, 2 * p + e] = Sr[p, e * RET_DK:(e + 1) * RET_DK, :]
                sg_ref[0, 2 * p + e] = SgT[p][:, e * GLA_DK:(e + 1) * GLA_DK].T


def _scan(z, la, g_ret, g_gla, tables, s0, *, nseq, seq_rows, T, C, SC, out_dtype):
    dmat, qdec, kdec, tri, cdec = tables
    nT = seq_rows // T
    zero_init = s0 is None
    zw = 2 * RET_HEADS * RET_DK + 2 * BRANCH_W
    const3 = lambda s, t: (0, 0, 0)
    const2 = lambda s, t: (0, 0)
    in_specs = [
        pl.BlockSpec((T, zw), lambda s, t: (s * nT + t, Z_RQ // zw)),
        pl.BlockSpec((T, zw), lambda s, t: (s * nT + t, Z_GQ // zw)),
        pl.BlockSpec((T, GLA_HEADS * GLA_DK), lambda s, t: (s * nT + t, 0)),
        pl.BlockSpec(dmat.shape, const3),
        pl.BlockSpec(qdec.shape, const3),
        pl.BlockSpec(kdec.shape, const3),
        pl.BlockSpec(tri.shape, const2),
        pl.BlockSpec((1, BRANCH_W), const2),
        pl.BlockSpec((1, BRANCH_W), const2),
    ]
    args = [z, z, la, dmat, qdec, kdec, tri, g_ret, g_gla]
    st_spec = pl.BlockSpec((1, RET_HEADS, RET_DK, RET_DV), lambda s, t: (s, 0, 0, 0))
    if not zero_init:
        in_specs += [st_spec, st_spec]
        args += [s0[0], s0[1]]
    st_shape = jax.ShapeDtypeStruct((nseq, RET_HEADS, RET_DK, RET_DV), F32)
    return pl.pallas_call(
        functools.partial(_scan_kernel, T=T, C=C, SC=SC, cdec=cdec, zero_init=zero_init),
        grid=(nseq, nT),
        in_specs=in_specs,
        out_specs=[pl.BlockSpec((T, 2 * BRANCH_W), lambda s, t: (s * nT + t, 0)), st_spec, st_spec],
        out_shape=[jax.ShapeDtypeStruct((nseq * seq_rows, 2 * BRANCH_W), out_dtype), st_shape, st_shape],
        scratch_shapes=[pltpu.VMEM((RET_HEADS // 2, LANES, RET_DV), F32),
                        pltpu.VMEM((GLA_HEADS // 2, GLA_DV, LANES), F32)],
        compiler_params=_cparams(("parallel", "arbitrary")),
        name="scan",
    )(*args)


PAGES_PER_STEP = 16
PAGES_PER_BLOCK = MOBA_BLOCK // PAGE_SIZE
VT_ROWS = MOBA_HD + 16
ATTEND_HEADS = 4


def _kmean_step(page_refs, km_ref):
    for blk in range(PAGES_PER_STEP // PAGES_PER_BLOCK):
        tot = page_refs[blk * PAGES_PER_BLOCK][...]
        for p in range(1, PAGES_PER_BLOCK):
            tot = tot + page_refs[blk * PAGES_PER_BLOCK + p][...]
        km_ref[0, 0, :, :, :, blk:blk + 1] = jnp.sum(tot, axis=-1, keepdims=True) * (1.0 / MOBA_BLOCK)


def _moba_prompt_kernel(*refs, nblk, with_kmean):
    if with_kmean:
        q_ref, k_ref, v_ref = refs[1:4]
        o_ref, kmo_ref, km_scr, ka_scr, vt_scr = refs[4 + PAGES_PER_STEP:]
    else:
        q_ref, k_ref, v_ref, o_ref, km_scr, ka_scr, vt_scr = refs
    t = pl.program_id(2)
    BLK = MOBA_BLOCK
    G = MOBA_GROUP
    lane = lax.broadcasted_iota(jnp.int32, (1, LANES), 1)
    in_e = [lane < MOBA_HD, lane >= MOBA_HD]
    off = [MOBA_HD, 0]

    @pl.when(t == 0)
    def _():
        for blk in range(nblk):
            km_scr[blk:blk + 1, :] = jnp.mean(k_ref[blk * BLK:(blk + 1) * BLK, :], axis=0, keepdims=True)

        def prep(blk, c):
            r0 = pl.multiple_of(blk * BLK, BLK)
            kj = k_ref[pl.ds(r0, BLK), :]
            for e in range(2):
                ka_scr[e, pl.ds(r0, BLK), :] = jnp.where(
                    in_e[e], kj, jnp.where(lane == off[e] + blk, 1.0, 0.0)).astype(BF16)
            vt = v_ref[pl.ds(r0, BLK), :].T.astype(BF16)
            for e in range(2):
                vt_scr[e, 0:MOBA_HD, pl.ds(r0, BLK)] = vt[e * MOBA_HD:(e + 1) * MOBA_HD, :]
                vt_scr[e, MOBA_HD:VT_ROWS, pl.ds(r0, BLK)] = jnp.ones((VT_ROWS - MOBA_HD, BLK), BF16)
            return c
        lax.fori_loop(0, nblk, prep, 0)

    if with_kmean:
        _kmean_step(refs[4:4 + PAGES_PER_STEP], kmo_ref)
    q = q_ref[...]
    q2t = (q * LOG2E).T
    t0 = pl.multiple_of(t * BLK, BLK)
    blkrow = lax.broadcasted_iota(jnp.int32, (nblk, 1), 0)
    rowk = lax.broadcasted_iota(jnp.int32, (BLK, 1), 0)
    laneq = lax.broadcasted_iota(jnp.int32, (1, BLK), 1)
    zpad = jnp.zeros((MOBA_HD - nblk, BLK), F32)
    zhead = jnp.zeros((MOBA_HD, BLK), F32)
    waug, init = [], []
    for e in range(2):
        g = _dot(km_scr[...], jnp.where(in_e[e], q, 0.0), NT, precision=lax.Precision.HIGHEST)
        g = jnp.where(blkrow < t, g, -jnp.inf)
        sel = jnp.zeros(g.shape, jnp.bool_)
        for _ in range(MOBA_TOPK):
            m = jnp.max(g, axis=0, keepdims=True)
            idx = jnp.min(jnp.where(g == m, blkrow, nblk), axis=0, keepdims=True)
            pick = (blkrow == idx) & (m > -jnp.inf)
            sel = sel | pick
            g = jnp.where(pick, -jnp.inf, g)
        pen = jnp.where(sel, 0.0, -MASK_BIG)
        qe = q2t[e * MOBA_HD:(e + 1) * MOBA_HD, :]
        if e == 0:
            waug.append(jnp.concatenate([qe, pen, zpad], axis=0).astype(BF16))
            wown = jnp.concatenate([qe, zhead], axis=0).astype(BF16)
        else:
            waug.append(jnp.concatenate([pen, zpad, qe], axis=0).astype(BF16))
            wown = jnp.concatenate([zhead, qe], axis=0).astype(BF16)
        s = _dot(ka_scr[e, pl.ds(t0, BLK), :], wown)
        s = jnp.where(rowk <= laneq, s, NEG_BIG)
        m0 = jnp.max(s, axis=0, keepdims=True)
        init += [m0, _dot(vt_scr[e, :, pl.ds(t0, BLK)], jnp.exp2(s - m0).astype(BF16))]

    def body(gi, carry):
        r0 = pl.multiple_of(gi * (G * BLK), G * BLK)
        st = list(carry)
        ss = [[_dot(ka_scr[e, pl.ds(r0 + j * BLK, BLK), :], waug[e]) for j in range(G)] for e in range(2)]
        for j in range(G):
            for e in range(2):
                m, acc = st[2 * e:2 * e + 2]
                s = ss[e][j]
                mn = jnp.maximum(m, jnp.max(s, axis=0, keepdims=True))
                p = jnp.exp2(s - mn).astype(BF16)
                st[2 * e:2 * e + 2] = [mn, jnp.exp2(m - mn) * acc + _dot(vt_scr[e, :, pl.ds(r0 + j * BLK, BLK)], p)]
        return tuple(st)

    res = lax.fori_loop(0, (t + G - 1) // G, body, tuple(init))
    outs = [res[2 * e + 1][0:MOBA_HD] / res[2 * e + 1][MOBA_HD:MOBA_HD + 1] for e in range(2)]
    o_ref[...] = jnp.concatenate(outs, axis=0).T.astype(o_ref.dtype)


def _moba_prompt(z, *, nseq, seq_rows, kmean_src=None):
    nblk = seq_rows // MOBA_BLOCK
    npair = MOBA_HEADS // 2
    assert nblk % MOBA_GROUP == 0 and nblk <= MOBA_HD and nblk % SUBLANES == 0
    with_kmean = kmean_src is not None
    in_specs = [
        pl.BlockSpec((MOBA_BLOCK, LANES), lambda b, hp, t, *_: (b * nblk + t, Z_MQ // LANES + hp)),
        pl.BlockSpec((seq_rows, LANES), lambda b, hp, t, *_: (b, Z_MK // LANES + hp)),
        pl.BlockSpec((seq_rows, LANES), lambda b, hp, t, *_: (b, Z_MV // LANES + hp)),
    ]
    out_specs = [pl.BlockSpec((MOBA_BLOCK, LANES), lambda b, hp, t, *_: (b * nblk + t, hp))]
    out_shape = [jax.ShapeDtypeStruct((nseq * seq_rows, BRANCH_W), BF16)]
    args = [z, z, z]
    if with_kmean:
        cache_kt, pt_flat, nseq_s, npages = kmean_src
        depth = cache_kt.shape[1]
        bps = PAGES_PER_STEP // PAGES_PER_BLOCK
        nstep = npages // PAGES_PER_STEP
        assert nseq * npair * nblk == nseq_s * nstep

        def page_spec(i):
            def imap(b, hp, t, pt):
                return (pt[((b * npair + hp) * nblk + t) * PAGES_PER_STEP + i], 0, 0, 0, 0)
            return pl.BlockSpec((None, depth, MOBA_HEADS, MOBA_HD, PAGE_SIZE), imap)

        def km_map(b, hp, t, pt):
            s = (b * npair + hp) * nblk + t
            return (s // nstep, s % nstep, 0, 0, 0, 0)

        in_specs += [page_spec(i) for i in range(PAGES_PER_STEP)]
        args = [pt_flat] + args + [cache_kt] * PAGES_PER_STEP
        out_specs.append(pl.BlockSpec((1, 1, depth, MOBA_HEADS, MOBA_HD, bps), km_map))
        out_shape.append(jax.ShapeDtypeStruct((nseq_s, nstep, depth, MOBA_HEADS, MOBA_HD, bps), F32))
    grid_spec = pltpu.PrefetchScalarGridSpec(
        num_scalar_prefetch=1 if with_kmean else 0,
        grid=(nseq, npair, nblk),
        in_specs=in_specs,
        out_specs=out_specs,
        scratch_shapes=[pltpu.VMEM((nblk, LANES), F32),
                        pltpu.VMEM((2, seq_rows, LANES), BF16),
                        pltpu.VMEM((2, VT_ROWS, seq_rows), BF16)],
    )
    res = pl.pallas_call(
        functools.partial(_moba_prompt_kernel, nblk=nblk, with_kmean=with_kmean),
        grid_spec=grid_spec,
        out_shape=out_shape,
        compiler_params=_cparams(("parallel", "parallel", "arbitrary")),
        name="moba_prompt",
    )(*args)
    if not with_kmean:
        return res[0], None
    km = res[1].transpose(0, 2, 3, 4, 1, 5).reshape(nseq_s, depth, MOBA_HEADS, MOBA_HD, nstep * bps)
    return res[0], km


def _moba_topk_kernel(q_ref, km_ref, idx_ref):
    nfull = km_ref.shape[2]
    lane = lax.broadcasted_iota(jnp.int32, (1, LANES), 1)
    lane_b = lax.broadcasted_iota(jnp.int32, (1, nfull), 1)
    out = jnp.zeros((SAMPLE_ROWS, LANES), jnp.int32)
    for h in range(MOBA_HEADS):
        gsc = _dot(q_ref[h], km_ref[h], precision=lax.Precision.HIGHEST)
        for r in range(MOBA_TOPK):
            m = jnp.max(gsc, axis=-1, keepdims=True)
            idx = jnp.min(jnp.where(gsc == m, lane_b, nfull), axis=-1, keepdims=True)
            out = jnp.where(lane == h * MOBA_TOPK + r, idx, out)
            gsc = jnp.where(lane_b == idx, -jnp.inf, gsc)
    idx_ref[0] = out


def _moba_topk(qh, kmean, *, layer, nseq):
    nfull = kmean.shape[4]
    return pl.pallas_call(
        _moba_topk_kernel,
        grid=(nseq,),
        in_specs=[
            pl.BlockSpec((MOBA_HEADS, SAMPLE_ROWS, MOBA_HD), lambda b: (0, b, 0)),
            pl.BlockSpec((None, None, MOBA_HEADS, MOBA_HD, nfull), lambda b: (b, layer, 0, 0, 0)),
        ],
        out_specs=pl.BlockSpec((1, SAMPLE_ROWS, LANES), lambda b: (b, 0, 0)),
        out_shape=jax.ShapeDtypeStruct((nseq, SAMPLE_ROWS, LANES), jnp.int32),
        compiler_params=_cparams(("parallel",)),
        name="moba_topk",
    )(qh, kmean)


def _moba_attend_kernel(idx_ref, pt_ref, q_ref, kn_ref, vn_ref, ck_hbm, cv_hbm, o_ref, kbuf, vbuf, sem,
                        *, layer, npages, ntok):
    b = pl.program_id(0)
    hg = pl.program_id(1)
    ng = pl.num_programs(1)
    step = b * ng + hg
    nstep = pl.num_programs(0) * ng
    slot = step % 2
    per_head = ntok * MOBA_TOPK * PAGES_PER_BLOCK

    def copies(bb, gg, sl):
        out = []
        for e in range(ATTEND_HEADS):
            hh = gg * ATTEND_HEADS + e
            for t in range(ntok):
                for r in range(MOBA_TOPK):
                    blk = idx_ref[((bb * ntok + t) * MOBA_HEADS + hh) * MOBA_TOPK + r]
                    for p in range(PAGES_PER_BLOCK):
                        page = pt_ref[bb * npages + blk * PAGES_PER_BLOCK + p]
                        i = e * per_head + (t * MOBA_TOPK + r) * PAGES_PER_BLOCK + p
                        out.append(pltpu.make_async_copy(ck_hbm.at[page, layer, hh], kbuf.at[sl, i], sem.at[0, sl]))
                        out.append(pltpu.make_async_copy(cv_hbm.at[page, layer, hh], vbuf.at[sl, i], sem.at[1, sl]))
        return out

    @pl.when(step == 0)
    def _():
        for c in copies(b, hg, slot):
            c.start()

    @pl.when(step + 1 < nstep)
    def _():
        nxt = step + 1
        for c in copies(nxt // ng, nxt % ng, 1 - slot):
            c.start()

    for c in copies(b, hg, slot):
        c.wait()

    half = SAMPLE_ROWS // 2
    rowi = lax.broadcasted_iota(jnp.int32, (SAMPLE_ROWS, 1), 0)
    coln = lax.broadcasted_iota(jnp.int32, (1, SAMPLE_ROWS), 1)
    for e in range(ATTEND_HEADS):
        qb = q_ref[e].astype(BF16)
        vn = vn_ref[e].astype(BF16)
        s_new_all = _dot(qb, kn_ref[e].astype(BF16), NT)
        out = jnp.zeros((SAMPLE_ROWS, MOBA_HD), F32)
        for t in range(ntok):
            qrow = half + t
            s_new = jnp.where((coln >= half) & (coln <= qrow), s_new_all, NEG_BIG)
            ss, vs = [], []
            i0 = e * per_head + t * MOBA_TOPK * PAGES_PER_BLOCK
            for i in range(i0, i0 + MOBA_TOPK * PAGES_PER_BLOCK):
                ss.append(_dot(qb, kbuf[slot, i].astype(BF16)))
                vs.append(vbuf[slot, i].astype(BF16))
            m = jnp.max(s_new, axis=-1, keepdims=True)
            for s in ss:
                m = jnp.maximum(m, jnp.max(s, axis=-1, keepdims=True))
            p_new = jnp.exp(s_new - m)
            l = jnp.sum(p_new, axis=-1, keepdims=True)
            o = _dot(p_new.astype(BF16), vn)
            for s, vb in zip(ss, vs):
                p = jnp.exp(s - m)
                l = l + jnp.sum(p, axis=-1, keepdims=True)
                o = o + _dot(p.astype(BF16), vb, NT)
            out = jnp.where(rowi == qrow, o / l, out)
        o_ref[e] = out


def _moba_attend(qh, knh, vnh, cache_k, cache_v, idx_flat, pt_flat, *, layer, nseq, npages, ntok):
    nbuf = ATTEND_HEADS * ntok * MOBA_TOPK * PAGES_PER_BLOCK
    head_spec = pl.BlockSpec((ATTEND_HEADS, SAMPLE_ROWS, MOBA_HD), lambda b, g, idx, pt: (g, b, 0))
    grid_spec = pltpu.PrefetchScalarGridSpec(
        num_scalar_prefetch=2,
        grid=(nseq, MOBA_HEADS // ATTEND_HEADS),
        in_specs=[head_spec, head_spec, head_spec,
                  pl.BlockSpec(memory_space=pl.ANY), pl.BlockSpec(memory_space=pl.ANY)],
        out_specs=head_spec,
        scratch_shapes=[pltpu.VMEM((2, nbuf, MOBA_HD, PAGE_SIZE), F32),
                        pltpu.VMEM((2, nbuf, MOBA_HD, PAGE_SIZE), F32),
                        pltpu.SemaphoreType.DMA((2, 2))],
    )
    return pl.pallas_call(
        functools.partial(_moba_attend_kernel, layer=layer, npages=npages, ntok=ntok),
        grid_spec=grid_spec,
        out_shape=jax.ShapeDtypeStruct((MOBA_HEADS, nseq * SAMPLE_ROWS, MOBA_HD), F32),
        compiler_params=_cparams(("arbitrary", "arbitrary")),
        name="moba_attend",
    )(idx_flat, pt_flat, qh, knh, vnh, cache_k, cache_v)


def _post_kernel(x_ref, org_ref, om_ref, gl_ref, wb_ref, wo_ref, gpost_ref, gpre_ref, x1_ref, h2_ref):
    merged = None
    for n in range(N_BRANCH):
        if n < 2:
            br = org_ref[:, n * BRANCH_W:(n + 1) * BRANCH_W]
        else:
            br = om_ref[...]
        proj = _dot(br.astype(BF16), wb_ref[n])
        term = _sigmoid(gl_ref[:, n * D_MODEL:(n + 1) * D_MODEL]) * proj
        merged = term if merged is None else merged + term
    y = _dot(merged.astype(BF16), wo_ref[...])
    x1 = x_ref[...] + _rms(y, gpost_ref[...])
    x1_ref[...] = x1
    h2_ref[...] = _rms(x1, gpre_ref[...]).astype(BF16)


def _post(x, org, om, z, wb, wo, gpost, gpre, *, tm):
    n = x.shape[0]
    c2 = lambda i: (0, 0)
    return pl.pallas_call(
        _post_kernel,
        grid=(n // tm,),
        in_specs=[
            pl.BlockSpec((tm, D_MODEL), lambda i: (i, 0)),
            pl.BlockSpec((tm, 2 * BRANCH_W), lambda i: (i, 0)),
            pl.BlockSpec((tm, BRANCH_W), lambda i: (i, 0)),
            pl.BlockSpec((tm, N_BRANCH * D_MODEL), lambda i: (i, 0)),
            pl.BlockSpec((N_BRANCH, BRANCH_W, D_MODEL), lambda i: (0, 0, 0)),
            pl.BlockSpec((D_MODEL, D_MODEL), c2),
            pl.BlockSpec((1, D_MODEL), c2),
            pl.BlockSpec((1, D_MODEL), c2),
        ],
        out_specs=[pl.BlockSpec((tm, D_MODEL), lambda i: (i, 0)),
                   pl.BlockSpec((tm, D_MODEL), lambda i: (i, 0))],
        out_shape=[jax.ShapeDtypeStruct((n, D_MODEL), F32), jax.ShapeDtypeStruct((n, D_MODEL), BF16)],
        compiler_params=_cparams(("parallel",)),
        name="post",
    )(x, org, om, z, wb, wo, gpost, gpre)


FF_CHUNK = 256


def _ffn_kernel(*refs, sample, tiles_per_seq, tail_rows):
    if sample:
        (x1_ref, h2_ref, st_ref, wup_ref, wc_ref, bc_ref, wd_ref, g_ref, x2_ref, tail_ref, act_scr) = refs
    else:
        (x1_ref, h2_ref, halo_ref, wup_ref, wc_ref, bc_ref, wd_ref, g_ref, x2_ref, tail_ref, act_scr) = refs
    i = pl.program_id(0)
    tm = x1_ref.shape[0]
    h2 = h2_ref[...]
    row = lax.broadcasted_iota(jnp.int32, (tm, 1), 0)
    row8 = lax.broadcasted_iota(jnp.int32, (SUBLANES, 1), 0)
    if not sample:
        halo = jnp.where((i % tiles_per_seq) == 0, jnp.zeros_like(halo_ref[...]), halo_ref[...])

    def shifted(u, uh, sh):
        us = pltpu.roll(u, sh, 0)
        if uh is None:
            return us
        hs = pltpu.roll(uh, sh, 0)
        head = jnp.where(row8 < sh, hs, us[:SUBLANES])
        return jnp.concatenate([head, us[SUBLANES:]], axis=0)

    for c in range(D_FF // FF_CHUNK):
        halves = []
        for half in range(2):
            c0 = half * D_FF + c * FF_CHUNK
            w = wup_ref[:, c0:c0 + FF_CHUNK]
            u = _dot(h2, w)
            if sample:
                inj = ((row % SAMPLE_ROWS) >= SAMPLE_ROWS // 2 - (CONV_W - 1)) & ((row % SAMPLE_ROWS) < SAMPLE_ROWS // 2)
                u = jnp.where(inj, st_ref[:, c0:c0 + FF_CHUNK], u)
                uh = None
            else:
                uh = _dot(halo, w)
            tail_ref[0, :, c0:c0 + FF_CHUNK] = u[tm - tail_rows:]
            conv = bc_ref[:, c0:c0 + FF_CHUNK] + u * wc_ref[CONV_W - 1:CONV_W, c0:c0 + FF_CHUNK]
            for sh in range(1, CONV_W):
                conv = conv + shifted(u, uh, sh) * wc_ref[CONV_W - 1 - sh:CONV_W - sh, c0:c0 + FF_CHUNK]
            halves.append(conv)
        a, b = halves
        gelu = 0.5 * a * (1.0 + jnp.tanh(0.7978845608028654 * (a + 0.044715 * a * a * a)))
        act_scr[:, c * FF_CHUNK:(c + 1) * FF_CHUNK] = (gelu * b).astype(BF16)
    x2 = x1_ref[...] + _rms(_dot(act_scr[...], wd_ref[...]), g_ref[...])
    if sample:
        x2 = jnp.where((row % SAMPLE_ROWS) >= SAMPLE_ROWS // 2, x2, 0.0)
    x2_ref[...] = x2


def _ffn(x1, h2, extra, wup, wc, bc, wd, g, *, tm, sample, nseq):
    n = x1.shape[0]
    ntile = n // tm
    tiles_per_seq = ntile // nseq if not sample else 1
    tail_rows = tm if sample else SUBLANES
    ntail = ntile if sample else nseq
    c2 = lambda i: (0, 0)
    if sample:
        extra_spec = pl.BlockSpec((tm, 2 * D_FF), lambda i: (i, 0))
        extra_arg = extra
    else:
        hb = tm // SUBLANES
        extra_spec = pl.BlockSpec((SUBLANES, D_MODEL), lambda i: (jnp.maximum(i * hb - 1, 0), 0))
        extra_arg = h2
    return pl.pallas_call(
        functools.partial(_ffn_kernel, sample=sample, tiles_per_seq=tiles_per_seq, tail_rows=tail_rows),
        grid=(ntile,),
        in_specs=[
            pl.BlockSpec((tm, D_MODEL), lambda i: (i, 0)),
            pl.BlockSpec((tm, D_MODEL), lambda i: (i, 0)),
            extra_spec,
            pl.BlockSpec((D_MODEL, 2 * D_FF), c2, pipeline_mode=pl.Buffered(1)),
            pl.BlockSpec((CONV_W, 2 * D_FF), c2),
            pl.BlockSpec((1, 2 * D_FF), c2),
            pl.BlockSpec((D_FF, D_MODEL), c2, pipeline_mode=pl.Buffered(1)),
            pl.BlockSpec((1, D_MODEL), c2),
        ],
        out_specs=[pl.BlockSpec((tm, D_MODEL), lambda i: (i, 0)),
                   pl.BlockSpec((1, tail_rows, 2 * D_FF), lambda i: (i // tiles_per_seq, 0, 0))],
        out_shape=[jax.ShapeDtypeStruct((n, D_MODEL), F32),
                   jax.ShapeDtypeStruct((ntail, tail_rows, 2 * D_FF), F32)],
        scratch_shapes=[pltpu.VMEM((tm, D_FF), BF16)],
        compiler_params=_cparams(("arbitrary",)),
        name="ffn",
    )(x1, h2, extra_arg, wup, wc, bc, wd, g)


def _kv_out_kernel(*refs, depth):
    z_refs, (kt_ref, vt_ref) = refs[:2 * depth], refs[2 * depth:]
    for l in range(depth):
        kt_ref[l] = z_refs[2 * l][...].T
        vt_ref[l] = z_refs[2 * l + 1][...].T


def _kv_out(zs, *, nseq, seq_rows, tm):
    depth = len(zs)
    W = MOBA_HEADS * MOBA_HD
    nT = seq_rows // tm
    in_specs, args = [], []
    for z in zs:
        in_specs += [pl.BlockSpec((tm, W), lambda b, i: (b * nT + i, Z_MK // W)),
                     pl.BlockSpec((tm, W), lambda b, i: (b * nT + i, Z_MV // W))]
        args += [z, z]
    out_spec = pl.BlockSpec((None, depth, W, tm), lambda b, i: (b, 0, 0, i))
    shape = jax.ShapeDtypeStruct((nseq, depth, W, seq_rows), F32)
    return pl.pallas_call(
        functools.partial(_kv_out_kernel, depth=depth),
        grid=(nseq, nT),
        in_specs=in_specs,
        out_specs=[out_spec, out_spec],
        out_shape=[shape, shape],
        compiler_params=_cparams(("parallel", "parallel")),
        name="kv_out",
    )(*args)


def _rope_tables(pos):
    half = MOBA_HD // 2
    freqs = jnp.power(ROPE_THETA, -jnp.arange(half, dtype=F32) / half)
    ang = pos.astype(F32)[:, None] * freqs[None, :]
    cos, sin = jnp.cos(ang), jnp.sin(ang)
    return (jnp.concatenate([cos, cos, cos, cos], axis=1),
            jnp.concatenate([-sin, sin, -sin, sin], axis=1))


def kernel(x_prompt, x_sample, cache_k, cache_v, page_table, state_ret, state_gla, state_conv, g_pre_mix, w_in, w_gla_a2, b_gla_a, g_ret_norm, g_gla_norm, w_branch, w_out, g_post_mix, g_pre_ffn, w_up, w_conv, b_conv, w_down, g_post_ffn):
    B, S, D = x_prompt.shape
    Bd, T, _ = x_sample.shape
    depth = w_in.shape[0]
    npages = page_table.shape[1]
    past_len = npages * PAGE_SIZE
    half = SAMPLE_ROWS // 2
    assert D == D_MODEL and T == half and S % 2048 == 0
    assert past_len % MOBA_BLOCK == 0 and npages % PAGES_PER_STEP == 0
    assert past_len // MOBA_BLOCK >= MOBA_TOPK
    n_pool = cache_k.shape[0]
    W = MOBA_HEADS * MOBA_HD

    xp = x_prompt.reshape(B * S, D)
    xs = jnp.pad(x_sample, ((0, 0), (half, 0), (0, 0))).reshape(Bd * SAMPLE_ROWS, D)
    ns = Bd * SAMPLE_ROWS

    cos_p, sin_p = _rope_tables(jnp.arange(S, dtype=jnp.int32))
    pos_s = past_len + jnp.tile(jnp.arange(SAMPLE_ROWS, dtype=jnp.int32) - half, Bd)
    cos_s, sin_s = _rope_tables(pos_s)

    tab_p = _decay_tables(SCAN_TILE, np.ones(SCAN_TILE), CHUNK)
    tab_s = _decay_tables(SAMPLE_ROWS, np.concatenate([np.zeros(half), np.ones(half)]), SAMPLE_ROWS)

    pt_flat = page_table.reshape(-1).astype(jnp.int32)
    cache_kt = cache_k.transpose(0, 2, 3, 4, 1)
    cache_vt = cache_v.transpose(0, 2, 3, 4, 1)
    kmean = None

    def per_head(a):
        return a.reshape(a.shape[0], MOBA_HEADS, MOBA_HD).transpose(1, 0, 2)

    outs = {k: [] for k in ("ret_p", "ret_s", "gla_p", "gla_s", "k_p", "k_s", "v_s", "conv_p", "conv_s")}
    for l in range(depth):
        w = w_in[l]
        o_ga = 2 * RET_HEADS * RET_DK + 2 * BRANCH_W + 2 * GLA_HEADS * GLA_DK + 2 * BRANCH_W
        o_m = o_ga + GLA_RANK
        o_gl = o_m + 3 * BRANCH_W
        w_main = jnp.concatenate([w[:, o_gl:], w[:, :o_ga], w[:, o_m:o_gl]], axis=1).astype(BF16)
        w_ga = jnp.pad(w[:, o_ga:o_m], ((0, 0), (0, LANES - GLA_RANK))).astype(BF16)
        w2 = jnp.pad(w_gla_a2[l], ((0, LANES - GLA_RANK), (0, 0))).astype(BF16)
        ba = b_gla_a[l][None, :]
        g_pre = g_pre_mix[l][None, :]
        g_ret = g_ret_norm[l][None, :]
        g_gla = g_gla_norm[l][None, :]
        wb = w_branch[l].astype(BF16)
        wo = w_out[l].astype(BF16)
        gpost = g_post_mix[l][None, :]
        gpre2 = g_pre_ffn[l][None, :]
        wup = w_up[l].astype(BF16)
        wd = w_down[l].astype(BF16)
        wc = w_conv[l]
        bc = b_conv[l][None, :]
        gffn = g_post_ffn[l][None, :]

        zp, lap = _inproj(xp, g_pre, w_main, w_ga, w2, ba, cos_p, sin_p, tm=2048, sample=False)
        orgp, retp, glap = _scan(zp, lap, g_ret, g_gla, tab_p, None, nseq=B, seq_rows=S, T=SCAN_TILE,
                                 C=CHUNK, SC=SUBCHUNK, out_dtype=BF16)
        if kmean is None:
            omp, kmean = _moba_prompt(zp, nseq=B, seq_rows=S, kmean_src=(cache_kt, pt_flat, Bd, npages))
        else:
            omp, _ = _moba_prompt(zp, nseq=B, seq_rows=S)
        x1p, h2p = _post(xp, orgp, omp, zp, wb, wo, gpost, gpre2, tm=512)
        xp, tailp = _ffn(x1p, h2p, None, wup, wc, bc, wd, gffn, tm=1024, sample=False, nseq=B)
        outs["ret_p"].append(retp)
        outs["gla_p"].append(glap)
        outs["k_p"].append(zp)
        outs["conv_p"].append(tailp[:, SUBLANES - (CONV_W - 1):, :])

        zs, las = _inproj(xs, g_pre, w_main, w_ga, w2, ba, cos_s, sin_s, tm=ns, sample=True)
        orgs, rets, glas = _scan(zs, las, g_ret, g_gla, tab_s, (state_ret[l], state_gla[l]), nseq=Bd,
                                 seq_rows=SAMPLE_ROWS, T=SAMPLE_ROWS, C=SAMPLE_ROWS, SC=SAMPLE_ROWS,
                                 out_dtype=F32)
        qh = per_head(zs[:, Z_MQ:Z_MQ + W])
        idx = _moba_topk(qh, kmean, layer=l, nseq=Bd)
        idx_flat = idx[:, half:, :MOBA_HEADS * MOBA_TOPK].reshape(-1)
        oh = _moba_attend(qh, per_head(zs[:, Z_MK:Z_MK + W]), per_head(zs[:, Z_MV:Z_MV + W]), cache_kt, cache_vt,
                          idx_flat, pt_flat, layer=l, nseq=Bd, npages=npages, ntok=T)
        oms = oh.transpose(1, 0, 2).reshape(ns, W)
        x1s, h2s = _post(xs, orgs, oms, zs, wb, wo, gpost, gpre2, tm=ns)
        st = jnp.pad(state_conv[l], ((0, 0), (half - (CONV_W - 1), half), (0, 0))).reshape(ns, 2 * D_FF)
        xs, tails = _ffn(x1s, h2s, st, wup, wc, bc, wd, gffn, tm=ns, sample=True, nseq=Bd)
        zs3 = zs.reshape(Bd, SAMPLE_ROWS, Z_W)
        outs["ret_s"].append(rets)
        outs["gla_s"].append(glas)
        outs["k_s"].append(zs3[:, half:, Z_MK:Z_MK + W].reshape(Bd, T, MOBA_HEADS, MOBA_HD))
        outs["v_s"].append(zs3[:, half:, Z_MV:Z_MV + W].reshape(Bd, T, MOBA_HEADS, MOBA_HD))
        outs["conv_s"].append(tails.reshape(Bd, SAMPLE_ROWS, 2 * D_FF)[:, SAMPLE_ROWS - (CONV_W - 1):, :])

    yp = xp.reshape(B, S, D)
    ys = xs.reshape(Bd, SAMPLE_ROWS, D)[:, half:, :]
    kt, vt = _kv_out(outs["k_p"], nseq=B, seq_rows=S, tm=512)
    k_p = kt.reshape(B, depth, MOBA_HEADS, MOBA_HD, S).transpose(0, 4, 1, 2, 3)
    v_p = vt.reshape(B, depth, MOBA_HEADS, MOBA_HD, S).transpose(0, 4, 1, 2, 3)
    return (yp, ys,
            jnp.stack(outs["ret_p"], axis=0), jnp.stack(outs["ret_s"], axis=0),
            jnp.stack(outs["gla_p"], axis=0), jnp.stack(outs["gla_s"], axis=0),
            k_p, v_p,
            jnp.stack(outs["k_s"], axis=2), jnp.stack(outs["v_s"], axis=2),
            jnp.stack(outs["conv_p"], axis=0), jnp.stack(outs["conv_s"], axis=0))
```
